```python
import jax, jax.numpy as jnp
from jax import lax
import numpy as np

D_MODEL = 1024
BATCH = 16
SEQ = 256
DEPTH = 4
DEC_BATCH = 2
DEC_SEQ = 2048
PAST_LEN = 512

GRID_W = 64
HEAD_DIM = 64
N_Q_HEADS = 12
N_KV_HEADS = 4
Q_GROUP = N_Q_HEADS // N_KV_HEADS
ATTN_W = N_Q_HEADS * HEAD_DIM
KV_W = N_KV_HEADS * HEAD_DIM
N_FOURIER_GROUPS = 4
FOURIER_W = D_MODEL // 4
FOURIER_GROUP_W = FOURIER_W // N_FOURIER_GROUPS
IN_W = ATTN_W + 2 * KV_W + FOURIER_W + 2 * D_MODEL
ROPE_AXIS_DIM = HEAD_DIM // 2
ROPE_THETA = 10000.0
Q_BLOCK = 128
N_EXPERTS = 32
TOP_K = 4
D_EXPERT = D_MODEL
SWIGLU_ALPHA = 1.702
SWIGLU_LIMIT = 7.0
EXPERT_BLOCK = 128
NORM_EPS = 1e-6

kernel_name = 'hybrid_dit_gqa_fnet_moe_step'


def rms_norm(x, g):
    xf = x.astype(jnp.float32)
    y = xf * lax.rsqrt(jnp.mean(xf * xf, axis=-1, keepdims=True) + NORM_EPS)
    return (y * g.astype(jnp.float32)).astype(x.dtype)


def modulate(h, shift, scale):
    return h * (1 + scale[:, None, :]) + shift[:, None, :]


def rope_tables(n_tokens):
    rows = n_tokens // GRID_W
    row = jnp.repeat(jnp.arange(rows, dtype=jnp.float32), GRID_W)
    col = jnp.tile(jnp.arange(GRID_W, dtype=jnp.float32), rows)
    inv_freq = ROPE_THETA ** (-jnp.arange(0, ROPE_AXIS_DIM, 2, dtype=jnp.float32) / ROPE_AXIS_DIM)
    ang = jnp.stack([row[:, None] * inv_freq, col[:, None] * inv_freq], axis=1)
    return jnp.cos(ang), jnp.sin(ang)


def apply_rope_2d(x, cos, sin):
    shp = x.shape
    xf = x.astype(jnp.float32).reshape(shp[0], shp[1], shp[2], 2, ROPE_AXIS_DIM)
    half = ROPE_AXIS_DIM // 2
    x1, x2 = xf[..., :half], xf[..., half:]
    c = cos[None, :, None]
    s = sin[None, :, None]
    out = jnp.concatenate([x1 * c - x2 * s, x1 * s + x2 * c], axis=-1)
    return out.reshape(shp).astype(x.dtype)


def block_attention(q, k, v):
    b, lq = q.shape[0], q.shape[1]
    nb = lq // Q_BLOCK
    qb = q.reshape(b, nb, Q_BLOCK, N_KV_HEADS, Q_GROUP, HEAD_DIM).transpose(1, 0, 2, 3, 4, 5)
    scale = HEAD_DIM ** -0.5

    def one_block(qblk):
        s = jnp.einsum('bqkgd,bskd->bkgqs', qblk, k).astype(jnp.float32) * scale
        p = jax.nn.softmax(s, axis=-1).astype(v.dtype)
        return jnp.einsum('bkgqs,bskd->bqkgd', p, v)

    o = lax.map(one_block, qb)
    return o.transpose(1, 0, 2, 3, 4, 5).reshape(b, lq, ATTN_W)


def fourier_mix(f):
    b, l, _ = f.shape
    fg = f.astype(jnp.float32).reshape(b, l, N_FOURIER_GROUPS, FOURIER_GROUP_W)
    out = jnp.fft.fft2(fg, axes=(1, 3), norm='ortho').real
    return out.reshape(b, l, FOURIER_W).astype(f.dtype)


def token_mixer(h, rope, ctx_k, ctx_v, w_in, q_norm_g, k_norm_g, w_attn_o, w_fourier_o, w_out):
    b, l, _ = h.shape
    proj = h @ w_in
    cuts = [ATTN_W, ATTN_W + KV_W, ATTN_W + 2 * KV_W, ATTN_W + 2 * KV_W + FOURIER_W,
            ATTN_W + 2 * KV_W + FOURIER_W + D_MODEL]
    q, k, v, f, g_attn, g_four = jnp.split(proj, cuts, axis=-1)
    q = rms_norm(q.reshape(b, l, N_Q_HEADS, HEAD_DIM), q_norm_g)
    k = rms_norm(k.reshape(b, l, N_KV_HEADS, HEAD_DIM), k_norm_g)
    v = v.reshape(b, l, N_KV_HEADS, HEAD_DIM)
    if rope is not None:
        q = apply_rope_2d(q, rope[0], rope[1])
        k = apply_rope_2d(k, rope[0], rope[1])
    if ctx_k is None:
        keys, vals = k, v
    else:
        keys = jnp.concatenate([ctx_k, k], axis=1)
        vals = jnp.concatenate([ctx_v, v], axis=1)
    a = block_attention(q, keys, vals)
    fo = fourier_mix(f)
    merged = jax.nn.sigmoid(g_attn) * (a @ w_attn_o) + jax.nn.sigmoid(g_four) * (fo @ w_fourier_o)
    return merged @ w_out, k, v


def clamped_swiglu(hdn):
    x_glu, x_lin = jnp.split(hdn, 2, axis=-1)
    x_glu = jnp.minimum(x_glu, SWIGLU_LIMIT)
    x_lin = jnp.clip(x_lin, -SWIGLU_LIMIT, SWIGLU_LIMIT)
    return x_glu * jax.nn.sigmoid(SWIGLU_ALPHA * x_glu) * (x_lin + 1)


def moe(h, w_router, b_router, w_exp_in, b_exp_in, w_exp_out, b_exp_out):
    b, l, d = h.shape
    t = b * l
    xt = h.reshape(t, d)
    logits = (xt @ w_router + b_router).astype(jnp.float32)
    top_v, top_e = lax.top_k(logits, TOP_K)
    gate = jax.nn.softmax(top_v, axis=-1)
    n_assign = t * TOP_K
    flat_e = top_e.reshape(-1).astype(jnp.int32)
    flat_tok = jnp.arange(n_assign, dtype=jnp.int32) // TOP_K
    flat_w = gate.reshape(-1)
    order = jnp.argsort(flat_e)
    se, stok, sw = flat_e[order], flat_tok[order], flat_w[order]
    counts = jnp.bincount(flat_e, length=N_EXPERTS).astype(jnp.int32)
    padded = (counts + EXPERT_BLOCK - 1) // EXPERT_BLOCK * EXPERT_BLOCK
    start = jnp.cumsum(counts) - counts
    pad_end = jnp.cumsum(padded)
    pad_start = pad_end - padded
    dest = pad_start[se] + (jnp.arange(n_assign, dtype=jnp.int32) - start[se])
    n_blocks = -(-n_assign // EXPERT_BLOCK) + N_EXPERTS
    m_pad = n_blocks * EXPERT_BLOCK
    buf_tok = jnp.full((m_pad,), t, dtype=jnp.int32).at[dest].set(stok)
    x_pad = jnp.concatenate([xt, jnp.zeros((1, d), xt.dtype)], axis=0)
    xb = x_pad[buf_tok].reshape(n_blocks, EXPERT_BLOCK, d)
    blk_start = jnp.arange(n_blocks, dtype=jnp.int32) * EXPERT_BLOCK
    blk_e = jnp.minimum(jnp.searchsorted(pad_end, blk_start, side='right'), N_EXPERTS - 1)

    def expert_block(args):
        xblk, e = args
        hdn = xblk @ w_exp_in[e] + b_exp_in[e]
        return clamped_swiglu(hdn) @ w_exp_out[e] + b_exp_out[e]

    yb = lax.map(expert_block, (xb, blk_e)).reshape(m_pad, d)
    y = jnp.zeros((t, d), yb.dtype).at[stok].add(yb[dest] * sw[:, None].astype(yb.dtype))
    return y.reshape(b, l, d)


def trunk_layer(x, mod, rope, ctx_k, ctx_v, norm1_g, w_in, q_norm_g, k_norm_g, w_attn_o,
                w_fourier_o, w_out, norm2_g, w_router, b_router, w_exp_in, b_exp_in,
                w_exp_out, b_exp_out):
    shift1, scale1, gate1, shift2, scale2, gate2 = jnp.split(mod, 6, axis=-1)
    h = modulate(rms_norm(x, norm1_g), shift1, scale1)
    mix, k, v = token_mixer(h, rope, ctx_k, ctx_v, w_in, q_norm_g, k_norm_g,
                            w_attn_o, w_fourier_o, w_out)
    x = x + gate1[:, None, :] * mix
    h = modulate(rms_norm(x, norm2_g), shift2, scale2)
    x = x + gate2[:, None, :] * moe(h, w_router, b_router, w_exp_in, b_exp_in, w_exp_out, b_exp_out)
    return x, k, v


def setup_inputs(seed: int = 0) -> dict:
    key = jax.random.key(seed)
    ks = jax.random.split(key, 24)
    f32 = jnp.float32
    nrm = lambda k, shp, s: jax.random.normal(k, shp, f32) * s
    D, E, F = D_MODEL, N_EXPERTS, D_EXPERT
    return {
        'x_prompt': nrm(ks[0], (BATCH, SEQ, D), 1.0),
        'x_sample': nrm(ks[1], (DEC_BATCH, DEC_SEQ, D), 1.0),
        'cache_k': nrm(ks[2], (DEC_BATCH, DEPTH, PAST_LEN, N_KV_HEADS, HEAD_DIM), 1.0),
        'cache_v': nrm(ks[3], (DEC_BATCH, DEPTH, PAST_LEN, N_KV_HEADS, HEAD_DIM), 1.0),
        'c': nrm(ks[4], (DEC_BATCH, D), 1.0),
        'c_ctx': nrm(ks[5], (D,), 1.0),
        'w_ada': nrm(ks[6], (DEPTH, D, 6 * D), 0.5 * D ** -0.5),
        'b_ada': nrm(ks[7], (DEPTH, 6 * D), 0.01),
        'norm1_g': 1.0 + nrm(ks[8], (DEPTH, D), 0.02),
        'w_in': nrm(ks[9], (DEPTH, D, IN_W), D ** -0.5),
        'q_norm_g': 1.0 + nrm(ks[10], (DEPTH, HEAD_DIM), 0.02),
        'k_norm_g': 1.0 + nrm(ks[11], (DEPTH, HEAD_DIM), 0.02),
        'w_attn_o': nrm(ks[12], (DEPTH, ATTN_W, D), ATTN_W ** -0.5),
        'w_fourier_o': nrm(ks[13], (DEPTH, FOURIER_W, D), FOURIER_W ** -0.5),
        'w_out': nrm(ks[14], (DEPTH, D, D), D ** -0.5),
        'norm2_g': 1.0 + nrm(ks[15], (DEPTH, D), 0.02),
        'w_router': nrm(ks[16], (DEPTH, D, E), D ** -0.5),
        'b_router': nrm(ks[17], (DEPTH, E), 0.01),
        'w_exp_in': nrm(ks[18], (DEPTH, E, D, 2 * F), D ** -0.5),
        'b_exp_in': nrm(ks[19], (DEPTH, E, 2 * F), 0.01),
        'w_exp_out': nrm(ks[20], (DEPTH, E, F, D), F ** -0.5),
        'b_exp_out': nrm(ks[21], (DEPTH, E, D), 0.01),
        'final_norm_g': 1.0 + nrm(ks[22], (D,), 0.02),
    }


def reference(x_prompt, x_sample, cache_k, cache_v, c, c_ctx, w_ada, b_ada, norm1_g, w_in,
              q_norm_g, k_norm_g, w_attn_o, w_fourier_o, w_out, norm2_g, w_router, b_router,
              w_exp_in, b_exp_in, w_exp_out, b_exp_out, final_norm_g):
    rope = rope_tables(x_sample.shape[1])
    cond_ctx = jax.nn.silu(c_ctx)[None, :]
    cond_lat = jax.nn.silu(c)
    xp, xs = x_prompt, x_sample
    ctx_keys, ctx_vals = [], []
    for l in range(DEPTH):
        lw = (norm1_g[l], w_in[l], q_norm_g[l], k_norm_g[l], w_attn_o[l], w_fourier_o[l],
              w_out[l], norm2_g[l], w_router[l], b_router[l], w_exp_in[l], b_exp_in[l],
              w_exp_out[l], b_exp_out[l])
        mod_ctx = cond_ctx @ w_ada[l] + b_ada[l]
        xp, k_l, v_l = trunk_layer(xp, mod_ctx, None, None, None, *lw)
        ctx_keys.append(k_l)
        ctx_vals.append(v_l)
        mod_lat = cond_lat @ w_ada[l] + b_ada[l]
        xs, _, _ = trunk_layer(xs, mod_lat, rope, cache_k[:, l], cache_v[:, l], *lw)
    y_prompt = rms_norm(xp, final_norm_g)
    y_sample = rms_norm(xs, final_norm_g)
    new_cache_k = jnp.stack(ctx_keys, axis=1)
    new_cache_v = jnp.stack(ctx_vals, axis=1)
    return (y_prompt, y_sample, new_cache_k, new_cache_v)
```

```python
import functools
import math

import jax
import jax.numpy as jnp
from jax import lax
from jax.experimental import pallas as pl
from jax.experimental.pallas import tpu as pltpu

F32 = jnp.float32
BF16 = jnp.bfloat16
I32 = jnp.int32

D_MODEL = 1024
DEPTH = 4
BATCH = 16
SEQ = 256
DEC_BATCH = 2
DEC_SEQ = 2048
PAST_LEN = 512
GRID_W = 64
HEAD_DIM = 64
N_Q_HEADS = 12
N_KV_HEADS = 4
ATTN_W = N_Q_HEADS * HEAD_DIM
KV_W = N_KV_HEADS * HEAD_DIM
FOURIER_W = D_MODEL // 4
N_FOURIER_GROUPS = 4
FOURIER_GROUP_W = FOURIER_W // N_FOURIER_GROUPS
IN_W = ATTN_W + 2 * KV_W + FOURIER_W + 2 * D_MODEL
ROPE_AXIS_DIM = HEAD_DIM // 2
ROPE_THETA = 10000.0
N_EXPERTS = 32
TOP_K = 4
D_EXPERT = D_MODEL
SWIGLU_ALPHA = 1.702
SWIGLU_LIMIT = 7.0
NORM_EPS = 1e-6

N_CTX = BATCH * SEQ
N_LAT = DEC_BATCH * DEC_SEQ
N_TOK = N_CTX + N_LAT
LANES = 128
ROW_TILE = 256
N_TILES = N_TOK // ROW_TILE
CTX_TILES = N_CTX // ROW_TILE
LAT_TILES_PER_BATCH = DEC_SEQ // ROW_TILE
N_GROUPS = 1 + DEC_BATCH
EXPERT_TILE = 256
N_ASSIGN = N_TOK * TOP_K
N_EXPERT_BLOCKS = N_ASSIGN // EXPERT_TILE + N_EXPERTS
M_PAD = N_EXPERT_BLOCKS * EXPERT_TILE
Q_CHUNKS = ATTN_W // LANES
NEG_BIG = -1e30
VMEM_LIMIT = 56 * 1024 * 1024

_NT = (((1,), (1,)), ((), ()))


def _params(sem, vmem=VMEM_LIMIT):
    return pltpu.CompilerParams(dimension_semantics=sem, vmem_limit_bytes=vmem)


def _tile_group(i):
    return jnp.where(i < CTX_TILES, 0, 1 + (i - CTX_TILES) // LAT_TILES_PER_BATCH)


def _rope_block(i):
    return jnp.where(i < CTX_TILES, 0, 1 + (i - CTX_TILES) % LAT_TILES_PER_BATCH)


def _mod_spec(layer, slot):
    return pl.BlockSpec((None, 1, D_MODEL),
                        lambda i: ((layer * N_GROUPS + _tile_group(i)) * 6 + slot, 0, 0))


def _layer_vec_spec(layer, width=D_MODEL):
    return pl.BlockSpec((None, 1, width), lambda *_: (layer, 0, 0))


def _mods_kernel(cond_ref, w_ref, b_ref, o_ref):
    c = cond_ref[...]
    s = c * jax.nn.sigmoid(c)
    o_ref[...] = jnp.dot(s, w_ref[...], preferred_element_type=F32,
                         precision=lax.Precision.HIGHEST) + b_ref[...]


def _mods(cond, w_ada, b_ada):
    tn = 1536
    return pl.pallas_call(
        _mods_kernel,
        grid=(DEPTH, 6 * D_MODEL // tn),
        in_specs=[pl.BlockSpec((8, D_MODEL), lambda l, j: (0, 0)),
                  pl.BlockSpec((None, D_MODEL, tn), lambda l, j: (l, 0, j)),
                  pl.BlockSpec((None, 1, tn), lambda l, j: (l, 0, j))],
        out_specs=pl.BlockSpec((None, 8, tn), lambda l, j: (l, 0, j)),
        out_shape=jax.ShapeDtypeStruct((DEPTH, 8, 6 * D_MODEL), F32),
        compiler_params=_params(("parallel", "parallel")),
        name="adaln_mods",
    )(cond, w_ada, b_ada.reshape(DEPTH, 1, 6 * D_MODEL))


def _inproj_kernel(x_ref, g_ref, sh_ref, sc_ref, w_ref, qg_ref, kg_ref, cos_ref, sin_ref,
                   bd_ref, cs_ref,
                   q_ref, kx_ref, vx_ref, kf_ref, vf_ref, fcs_ref, ga_ref, gf_ref, h_scr):
    x = x_ref[...]
    y = x * lax.rsqrt(jnp.mean(x * x, axis=-1, keepdims=True) + NORM_EPS) * g_ref[...]
    h_scr[...] = (y * (1.0 + sc_ref[...]) + sh_ref[...]).astype(BF16)
    hb = h_scr[...]
    lane = lax.broadcasted_iota(I32, (ROW_TILE, LANES), 1)
    low_half = lane < HEAD_DIM
    rot_first = (lane % ROPE_AXIS_DIM) < (ROPE_AXIS_DIM // 2)
    cos = cos_ref[...]
    sin = sin_ref[...]
    bd = bd_ref[...]

    def proj(lo, width):
        return jnp.dot(hb, w_ref[:, lo:lo + width], preferred_element_type=F32)

    def head_norm_rope(p, gain):
        pp = p * p
        hi = pp.astype(BF16)
        lo = (pp - hi.astype(F32)).astype(BF16)
        msq = (jnp.dot(hi, bd, preferred_element_type=F32)
               + jnp.dot(lo, bd, preferred_element_type=F32))
        n = p * lax.rsqrt(msq + NORM_EPS) * gain
        partner = jnp.where(rot_first, pltpu.roll(n, LANES - ROPE_AXIS_DIM // 2, 1),
                            pltpu.roll(n, ROPE_AXIS_DIM // 2, 1))
        return n * cos + partner * sin

    def head_pairs(r):
        sw = pltpu.roll(r, HEAD_DIM, 1)
        return jnp.where(low_half, r, sw), r, jnp.where(low_half, sw, r)

    scale = HEAD_DIM ** -0.5
    wide = 2 * LANES
    for c2 in range(ATTN_W // wide):
        p = proj(c2 * wide, wide)
        for t in range(2):
            c = 2 * c2 + t
            r = head_norm_rope(p[:, t * LANES:(t + 1) * LANES], qg_ref[...])
            q_ref[:, c * LANES:(c + 1) * LANES] = (r * scale).astype(BF16)

    p = proj(ATTN_W, KV_W)
    for j in range(KV_W // LANES):
        r = head_norm_rope(p[:, j * LANES:(j + 1) * LANES], kg_ref[...])
        kf_ref[:, j * LANES:(j + 1) * LANES] = r
        for t, piece in enumerate(head_pairs(r)):
            c = 3 * j + t
            kx_ref[:, c * LANES:(c + 1) * LANES] = piece.astype(BF16)

    p = proj(ATTN_W + KV_W, KV_W)
    vf_ref[...] = p
    for j in range(KV_W // LANES):
        for t, piece in enumerate(head_pairs(p[:, j * LANES:(j + 1) * LANES])):
            c = 3 * j + t
            vx_ref[:, c * LANES:(c + 1) * LANES] = piece.astype(BF16)

    f = proj(ATTN_W + 2 * KV_W, FOURIER_W)
    fcs_ref[...] = jnp.dot(f.astype(BF16), cs_ref[...], preferred_element_type=F32).astype(BF16)

    gate_lo = ATTN_W + 2 * KV_W + FOURIER_W
    half = D_MODEL // 2
    for j in range(2):
        ga_ref[:, j * half:(j + 1) * half] = jax.nn.sigmoid(proj(gate_lo + j * half, half)).astype(BF16)
        gf_ref[:, j * half:(j + 1) * half] = jax.nn.sigmoid(
            proj(gate_lo + D_MODEL + j * half, half)).astype(BF16)


def _inproj(layer, x, norm_g, mods, w_in, qg, kg, cos_tab, sin_tab, bd, cs):
    row = lambda w: pl.BlockSpec((ROW_TILE, w), lambda i: (i, 0))
    const = lambda a: pl.BlockSpec(a.shape, lambda i: (0,) * a.ndim)
    ext_w = Q_CHUNKS * LANES
    return pl.pallas_call(
        _inproj_kernel,
        grid=(N_TILES,),
        in_specs=[row(D_MODEL), _layer_vec_spec(layer), _mod_spec(layer, 0), _mod_spec(layer, 1),
                  pl.BlockSpec((None, D_MODEL, IN_W), lambda i: (layer, 0, 0)),
                  _layer_vec_spec(layer, LANES), _layer_vec_spec(layer, LANES),
                  pl.BlockSpec((ROW_TILE, LANES), lambda i: (_rope_block(i), 0)),
                  pl.BlockSpec((ROW_TILE, LANES), lambda i: (_rope_block(i), 0)),
                  const(bd), const(cs)],
        out_specs=[row(ATTN_W), row(ext_w), row(ext_w), row(KV_W), row(KV_W),
                   row(2 * FOURIER_W), row(D_MODEL), row(D_MODEL)],
        out_shape=[jax.ShapeDtypeStruct((N_TOK, ATTN_W), BF16),
                   jax.ShapeDtypeStruct((N_TOK, ext_w), BF16),
                   jax.ShapeDtypeStruct((N_TOK, ext_w), BF16),
                   jax.ShapeDtypeStruct((N_TOK, KV_W), F32),
                   jax.ShapeDtypeStruct((N_TOK, KV_W), F32),
                   jax.ShapeDtypeStruct((N_TOK, 2 * FOURIER_W), BF16),
                   jax.ShapeDtypeStruct((N_TOK, D_MODEL), BF16),
                   jax.ShapeDtypeStruct((N_TOK, D_MODEL), BF16)],
        scratch_shapes=[pltpu.VMEM((ROW_TILE, D_MODEL), BF16)],
        compiler_params=_params(("parallel",)),
        name=f"inproj_l{layer}",
    )(x, norm_g, mods, mods, w_in, qg, kg, cos_tab, sin_tab, bd, cs)


def _attn_kernel(*refs, n_parts):
    q_ref = refs[0]
    k_refs = refs[1:1 + n_parts]
    v_refs = refs[1 + n_parts:1 + 2 * n_parts]
    o_ref = refs[-1]
    tq = q_ref.shape[0]
    low_half = lax.broadcasted_iota(I32, (tq, LANES), 1) < HEAD_DIM
    for c in range(Q_CHUNKS):
        cols = slice(c * LANES, (c + 1) * LANES)
        qc = q_ref[:, cols]
        zero = jnp.zeros_like(qc)
        outs = []
        for mask in (low_half, jnp.logical_not(low_half)):
            qm = jnp.where(mask, qc, zero)
            ss = [lax.dot_general(qm, k[:, cols], _NT, preferred_element_type=F32) for k in k_refs]
            m = ss[0].max(axis=-1, keepdims=True)
            for s in ss[1:]:
                m = jnp.maximum(m, s.max(axis=-1, keepdims=True))
            ps = [jnp.exp(s - m) for s in ss]
            den = ps[0].sum(axis=-1, keepdims=True)
            for p in ps[1:]:
                den = den + p.sum(axis=-1, keepdims=True)
            acc = jnp.dot(ps[0].astype(BF16), v_refs[0][:, cols], preferred_element_type=F32)
            for p, v in zip(ps[1:], v_refs[1:]):
                acc = acc + jnp.dot(p.astype(BF16), v[:, cols], preferred_element_type=F32)
            outs.append(acc / den)
        o_ref[:, cols] = jnp.where(low_half, outs[0], outs[1]).astype(BF16)


def _attention_ctx(layer, q, kx, vx):
    w = Q_CHUNKS * LANES
    blk = lambda width: pl.BlockSpec((SEQ, width), lambda b: (b, 0))
    return pl.pallas_call(
        functools.partial(_attn_kernel, n_parts=1),
        grid=(BATCH,),
        in_specs=[blk(ATTN_W), blk(w), blk(w)],
        out_specs=blk(ATTN_W),
        out_shape=jax.ShapeDtypeStruct((N_TOK, ATTN_W), BF16),
        input_output_aliases={0: 0},
        compiler_params=_params(("parallel",)),
        name=f"attn_ctx_l{layer}",
    )(q, kx, vx)


def _attention_lat(layer, q, kx, vx, kcache, vcache):
    w = Q_CHUNKS * LANES
    nq = DEC_SEQ // ROW_TILE
    first = N_CTX // ROW_TILE
    qblk = pl.BlockSpec((ROW_TILE, ATTN_W), lambda b, i: (first + b * nq + i, 0))
    new = pl.BlockSpec((DEC_SEQ, w), lambda b, i: (N_CTX // DEC_SEQ + b, 0))
    old = pl.BlockSpec((None, PAST_LEN, w), lambda b, i: (b, 0, 0))
    return pl.pallas_call(
        functools.partial(_attn_kernel, n_parts=2),
        grid=(DEC_BATCH, nq),
        in_specs=[qblk, old, new, old, new],
        out_specs=qblk,
        out_shape=jax.ShapeDtypeStruct((N_TOK, ATTN_W), BF16),
        input_output_aliases={0: 0},
        compiler_params=_params(("parallel", "parallel")),
        name=f"attn_lat_l{layer}",
    )(q, kcache, kx, vcache, vx)


def _merge_kernel(a_ref, fc_ref, fs_ref, cl_ref, sl_ref, ga_ref, gf_ref, x_ref, gate_ref,
                  wa_ref, wf_ref, wo_ref, o_ref):
    attn = jnp.dot(a_ref[...], wa_ref[...], preferred_element_type=F32)
    fo = (jnp.dot(cl_ref[...], fc_ref[...], preferred_element_type=F32)
          - jnp.dot(sl_ref[...], fs_ref[...], preferred_element_type=F32))
    four = jnp.dot(fo.astype(BF16), wf_ref[...], preferred_element_type=F32)
    merged = ga_ref[...].astype(F32) * attn + gf_ref[...].astype(F32) * four
    mix = jnp.dot(merged.astype(BF16), wo_ref[...], preferred_element_type=F32)
    o_ref[...] = x_ref[...] + gate_ref[...] * mix


def _merge(layer, stream, a, fcs, cl, sl, ga, gf, x, mods, wa, wf, wo):
    if stream == "ctx":
        seq, n_batch, first_tile, first_seq = SEQ, BATCH, 0, 0
    else:
        seq, n_batch, first_tile, first_seq = DEC_SEQ, DEC_BATCH, CTX_TILES, N_CTX // DEC_SEQ
    nt = seq // ROW_TILE
    tile = lambda b, i: first_tile + b * nt + i
    row = lambda w: pl.BlockSpec((ROW_TILE, w), lambda b, i: (tile(b, i), 0))
    wspec = lambda a_: pl.BlockSpec((None,) + a_.shape[1:], lambda b, i: (layer, 0, 0))
    in_specs = [row(ATTN_W),
                pl.BlockSpec((seq, FOURIER_W), lambda b, i: (first_seq + b, 0)),
                pl.BlockSpec((seq, FOURIER_W), lambda b, i: (first_seq + b, 1)),
                pl.BlockSpec((ROW_TILE, seq), lambda b, i: (i, 0)),
                pl.BlockSpec((ROW_TILE, seq), lambda b, i: (i, 0)),
                row(D_MODEL), row(D_MODEL), row(D_MODEL),
                pl.BlockSpec((None, 1, D_MODEL),
                             lambda b, i: ((layer * N_GROUPS + _tile_group(tile(b, i))) * 6 + 2, 0, 0)),
                wspec(wa), wspec(wf), wspec(wo)]
    args = [a, fcs, fcs, cl, sl, ga, gf, x, mods, wa, wf, wo]
    aliases = {7: 0}
    return pl.pallas_call(
        _merge_kernel,
        grid=(n_batch, nt),
        in_specs=in_specs,
        out_specs=row(D_MODEL),
        out_shape=jax.ShapeDtypeStruct((N_TOK, D_MODEL), F32),
        input_output_aliases=aliases,
        compiler_params=_params(("parallel", "parallel")),
        name=f"merge_{stream}_l{layer}",
    )(*args)


def _router_kernel(x_ref, g_ref, sh_ref, sc_ref, wr_ref, br_ref, h_ref, te_ref, tg_ref):
    x = x_ref[...]
    y = x * lax.rsqrt(jnp.mean(x * x, axis=-1, keepdims=True) + NORM_EPS) * g_ref[...]
    h = y * (1.0 + sc_ref[...]) + sh_ref[...]
    h_ref[...] = h
    logits = jnp.dot(h, wr_ref[...], preferred_element_type=F32,
                     precision=lax.Precision.HIGHEST) + br_ref[...]
    lane = lax.broadcasted_iota(I32, logits.shape, 1).astype(F32)
    vals, ids = [], []
    for _ in range(TOP_K):
        m = logits.max(axis=-1, keepdims=True)
        idx = jnp.where(logits == m, lane, float(LANES)).min(axis=-1, keepdims=True)
        vals.append(m)
        ids.append(idx)
        logits = jnp.where(lane == idx, NEG_BIG, logits)
    es = [jnp.exp(v - vals[0]) for v in vals]
    den = es[0] + es[1] + es[2] + es[3]
    te = jnp.zeros_like(lane)
    tg = jnp.zeros_like(lane)
    for k in range(TOP_K):
        te = jnp.where(lane == float(k), ids[k], te)
        tg = jnp.where(lane == float(k), es[k] / den, tg)
    te_ref[...] = te.astype(I32)
    tg_ref[...] = tg


def _router(layer, x1, norm_g, mods, wr, br):
    row = lambda w: pl.BlockSpec((ROW_TILE, w), lambda i: (i, 0))
    return pl.pallas_call(
        _router_kernel,
        grid=(N_TILES,),
        in_specs=[row(D_MODEL), _layer_vec_spec(layer), _mod_spec(layer, 3), _mod_spec(layer, 4),
                  pl.BlockSpec((None, D_MODEL, LANES), lambda i: (layer, 0, 0)),
                  _layer_vec_spec(layer, LANES)],
        out_specs=[row(D_MODEL), row(LANES), row(LANES)],
        out_shape=[jax.ShapeDtypeStruct((N_TOK, D_MODEL), F32),
                   jax.ShapeDtypeStruct((N_TOK, LANES), I32),
                   jax.ShapeDtypeStruct((N_TOK, LANES), F32)],
        compiler_params=_params(("parallel",)),
        name=f"router_l{layer}",
    )(x1, norm_g, mods, mods, wr, br)


def _expert_kernel(be_ref, bv_ref, bt_ref, h_hbm, wi_ref, bi_ref, wo_ref, bo_ref, y_ref,
                   xbuf, wi_bf, wo_bf, sem):
    b = pl.program_id(0)

    @pl.when(bv_ref[b] == 0)
    def _():
        y_ref[...] = jnp.zeros_like(y_ref)

    @pl.when(bv_ref[b] == 1)
    def _():
        base = b * EXPERT_TILE

        def issue(r, carry):
            tok = bt_ref[base + r]
            pltpu.make_async_copy(h_hbm.at[pl.ds(tok, 1), :], xbuf.at[pl.ds(r, 1), :], sem).start()
            return carry

        lax.fori_loop(0, EXPERT_TILE, issue, 0)

        changed = jnp.logical_or(b == 0, be_ref[b] != be_ref[jnp.maximum(b - 1, 0)])

        @pl.when(changed)
        def _():
            wi_bf[...] = wi_ref[...].astype(BF16)
            wo_bf[...] = wo_ref[...].astype(BF16)

        def drain(r, carry):
            pltpu.make_async_copy(h_hbm.at[pl.ds(0, 1), :], xbuf.at[pl.ds(r, 1), :], sem).wait()
            return carry

        lax.fori_loop(0, EXPERT_TILE, drain, 0)

        x = xbuf[...].astype(BF16)
        hdn = jnp.dot(x, wi_bf[...], preferred_element_type=F32) + bi_ref[...]
        glu = jnp.minimum(hdn[:, :D_EXPERT], SWIGLU_LIMIT)
        lin = jnp.clip(hdn[:, D_EXPERT:], -SWIGLU_LIMIT, SWIGLU_LIMIT)
        act = glu * jax.nn.sigmoid(SWIGLU_ALPHA * glu) * (lin + 1.0)
        y_ref[...] = jnp.dot(act.astype(BF16), wo_bf[...], preferred_element_type=F32) + bo_ref[...]


def _experts(layer, blk_e, blk_valid, buf_tok, h2, w_exp_in, b_exp_in, w_exp_out, b_exp_out):
    grid_spec = pltpu.PrefetchScalarGridSpec(
        num_scalar_prefetch=3,
        grid=(N_EXPERT_BLOCKS,),
        in_specs=[pl.BlockSpec(memory_space=pl.ANY),
                  pl.BlockSpec((None, None, D_MODEL, 2 * D_EXPERT), lambda b, be, bv, bt: (layer, be[b], 0, 0)),
                  pl.BlockSpec((None, None, 1, 2 * D_EXPERT), lambda b, be, bv, bt: (layer, be[b], 0, 0)),
                  pl.BlockSpec((None, None, D_EXPERT, D_MODEL), lambda b, be, bv, bt: (layer, be[b], 0, 0)),
                  pl.BlockSpec((None, None, 1, D_MODEL), lambda b, be, bv, bt: (layer, be[b], 0, 0))],
        out_specs=pl.BlockSpec((EXPERT_TILE, D_MODEL), lambda b, be, bv, bt: (b, 0)),
        scratch_shapes=[pltpu.VMEM((EXPERT_TILE, D_MODEL), F32),
                        pltpu.VMEM((D_MODEL, 2 * D_EXPERT), BF16),
                        pltpu.VMEM((D_EXPERT, D_MODEL), BF16),
                        pltpu.SemaphoreType.DMA(())],
    )
    return pl.pallas_call(
        _expert_kernel,
        grid_spec=grid_spec,
        out_shape=jax.ShapeDtypeStruct((M_PAD, D_MODEL), F32),
        compiler_params=_params(("arbitrary",)),
        name=f"experts_l{layer}",
    )(blk_e, blk_valid, buf_tok, h2, w_exp_in,
      b_exp_in.reshape(DEPTH, N_EXPERTS, 1, 2 * D_EXPERT), w_exp_out,
      b_exp_out.reshape(DEPTH, N_EXPERTS, 1, D_MODEL))


def _combine_kernel(dest_ref, y_hbm, x_ref, tg_ref, gate_ref, *rest, final):
    if final:
        fg_ref, o_ref, n_ref, buf, sem = rest
    else:
        o_ref, buf, sem = rest
    base = pl.program_id(0) * ROW_TILE * TOP_K

    def issue(r, carry):
        for k in range(TOP_K):
            d = dest_ref[base + r * TOP_K + k]
            pltpu.make_async_copy(y_hbm.at[pl.ds(d, 1), :], buf.at[k, pl.ds(r, 1), :], sem).start()
        return carry

    lax.fori_loop(0, ROW_TILE, issue, 0)

    def drain(r, carry):
        for k in range(TOP_K):
            pltpu.make_async_copy(y_hbm.at[pl.ds(0, 1), :], buf.at[k, pl.ds(r, 1), :], sem).wait()
        return carry

    lax.fori_loop(0, ROW_TILE, drain, 0)

    tg = tg_ref[...]
    acc = tg[:, 0:1] * buf[0]
    for k in range(1, TOP_K):
        acc = acc + tg[:, k:k + 1] * buf[k]
    out = x_ref[...] + gate_ref[...] * acc
    o_ref[...] = out
    if final:
        n_ref[...] = out * lax.rsqrt(jnp.mean(out * out, axis=-1, keepdims=True) + NORM_EPS) * fg_ref[...]


def _combine(layer, dest, yb, x1, tg, mods, final_g=None):
    final = final_g is not None
    row = lambda w: pl.BlockSpec((ROW_TILE, w), lambda i, d: (i, 0))
    in_specs = [pl.BlockSpec(memory_space=pl.ANY), row(D_MODEL), row(LANES),
                pl.BlockSpec((None, 1, D_MODEL),
                             lambda i, d: ((layer * N_GROUPS + _tile_group(i)) * 6 + 5, 0, 0))]
    args = [dest, yb, x1, tg, mods]
    out_specs = [row(D_MODEL)]
    out_shape = [jax.ShapeDtypeStruct((N_TOK, D_MODEL), F32)]
    if final:
        in_specs.append(pl.BlockSpec((1, D_MODEL), lambda i, d: (0, 0)))
        args.append(final_g)
        out_specs.append(row(D_MODEL))
        out_shape.append(jax.ShapeDtypeStruct((N_TOK, D_MODEL), F32))
    grid_spec = pltpu.PrefetchScalarGridSpec(
        num_scalar_prefetch=1,
        grid=(N_TILES,),
        in_specs=in_specs,
        out_specs=out_specs,
        scratch_shapes=[pltpu.VMEM((TOP_K, ROW_TILE, D_MODEL), F32), pltpu.SemaphoreType.DMA(())],
    )
    return pl.pallas_call(
        functools.partial(_combine_kernel, final=final),
        grid_spec=grid_spec,
        out_shape=out_shape,
        compiler_params=_params(("arbitrary",)),
        name=f"combine_l{layer}",
    )(*args)


def _rope_tables():
    rows = DEC_SEQ // GRID_W
    row = jnp.repeat(jnp.arange(rows, dtype=F32), GRID_W)
    col = jnp.tile(jnp.arange(GRID_W, dtype=F32), rows)
    inv_freq = ROPE_THETA ** (-jnp.arange(0, ROPE_AXIS_DIM, 2, dtype=F32) / ROPE_AXIS_DIM)
    ang = jnp.stack([row[:, None] * inv_freq, col[:, None] * inv_freq], axis=1)
    cos, sin = jnp.cos(ang), jnp.sin(ang)
    cos_h = jnp.concatenate([cos, cos], axis=-1).reshape(DEC_SEQ, HEAD_DIM)
    sin_h = jnp.concatenate([-sin, sin], axis=-1).reshape(DEC_SEQ, HEAD_DIM)
    reps = LANES // HEAD_DIM
    cos_t = jnp.concatenate([jnp.ones((ROW_TILE, LANES), F32), jnp.tile(cos_h, (1, reps))], axis=0)
    sin_t = jnp.concatenate([jnp.zeros((ROW_TILE, LANES), F32), jnp.tile(sin_h, (1, reps))], axis=0)
    return cos_t, sin_t


def _dft_tables(n, scale):
    idx = jnp.arange(n, dtype=I32)
    ang = ((idx[:, None] * idx[None, :]) % n).astype(F32) * (2.0 * math.pi / n)
    return jnp.cos(ang) * scale, jnp.sin(ang) * scale


def _channel_dft():
    c, s = _dft_tables(FOURIER_GROUP_W, 1.0)
    eye = jnp.eye(N_FOURIER_GROUPS, dtype=F32)
    return jnp.concatenate([jnp.kron(eye, c), jnp.kron(eye, s)], axis=1).astype(BF16)


def _extend_heads(t):
    h = [t[..., i * HEAD_DIM:(i + 1) * HEAD_DIM] for i in range(N_KV_HEADS)]
    order = [0, 0, 0, 1, 1, 1, 2, 2, 2, 3, 3, 3]
    return jnp.concatenate([h[i] for i in order], axis=-1)


def _plan(top_e):
    flat_e = top_e[:, :TOP_K].reshape(-1)
    onehot = (flat_e[:, None] == jnp.arange(N_EXPERTS, dtype=I32)[None, :]).astype(I32)
    csum = jnp.cumsum(onehot, axis=0)
    rank = jnp.sum(csum * onehot, axis=1) - 1
    counts = csum[-1]
    padded = (counts + EXPERT_TILE - 1) // EXPERT_TILE * EXPERT_TILE
    pad_end = jnp.cumsum(padded)
    pad_start = pad_end - padded
    dest = (pad_start[flat_e] + rank).astype(I32)
    tok = jnp.arange(N_ASSIGN, dtype=I32) // TOP_K
    buf_tok = jnp.zeros((M_PAD,), I32).at[dest].set(tok)
    blk_start = jnp.arange(N_EXPERT_BLOCKS, dtype=I32) * EXPERT_TILE
    blk_e = jnp.minimum(jnp.searchsorted(pad_end, blk_start, side="right"), N_EXPERTS - 1).astype(I32)
    blk_valid = (blk_start < pad_end[-1]).astype(I32)
    return dest, buf_tok, blk_e, blk_valid


def kernel(x_prompt, x_sample, cache_k, cache_v, c, c_ctx, w_ada, b_ada, norm1_g, w_in, q_norm_g,
           k_norm_g, w_attn_o, w_fourier_o, w_out, norm2_g, w_router, b_router, w_exp_in, b_exp_in,
           w_exp_out, b_exp_out, final_norm_g):
    x = jnp.concatenate([x_prompt.reshape(N_CTX, D_MODEL), x_sample.reshape(N_LAT, D_MODEL)], axis=0)

    cond = jnp.concatenate([c_ctx[None, :], c, jnp.zeros((8 - N_GROUPS, D_MODEL), F32)], axis=0)
    mods = _mods(cond, w_ada, b_ada)[:, :N_GROUPS].reshape(DEPTH * N_GROUPS * 6, 1, D_MODEL)

    cos_t, sin_t = _rope_tables()
    bd = jnp.kron(jnp.eye(LANES // HEAD_DIM, dtype=F32),
                  jnp.full((HEAD_DIM, HEAD_DIM), 1.0 / HEAD_DIM, F32)).astype(BF16)
    cs = _channel_dft()
    dft_ctx = [t.astype(BF16) for t in _dft_tables(SEQ, (SEQ * FOURIER_GROUP_W) ** -0.5)]
    dft_lat = [t.astype(BF16) for t in _dft_tables(DEC_SEQ, (DEC_SEQ * FOURIER_GROUP_W) ** -0.5)]

    w_in_b = w_in.astype(BF16)
    wa_b = w_attn_o.astype(BF16)
    wf_b = w_fourier_o.astype(BF16)
    wo_b = w_out.astype(BF16)
    norm1 = norm1_g.reshape(DEPTH, 1, D_MODEL)
    norm2 = norm2_g.reshape(DEPTH, 1, D_MODEL)
    qg = jnp.tile(q_norm_g, (1, LANES // HEAD_DIM)).reshape(DEPTH, 1, LANES)
    kg = jnp.tile(k_norm_g, (1, LANES // HEAD_DIM)).reshape(DEPTH, 1, LANES)
    wr = jnp.pad(w_router, ((0, 0), (0, 0), (0, LANES - N_EXPERTS)))
    br = jnp.pad(b_router, ((0, 0), (0, LANES - N_EXPERTS)), constant_values=NEG_BIG).reshape(DEPTH, 1, LANES)
    kcache = _extend_heads(cache_k.reshape(DEC_BATCH, DEPTH, PAST_LEN, KV_W)).astype(BF16)
    vcache = _extend_heads(cache_v.reshape(DEC_BATCH, DEPTH, PAST_LEN, KV_W)).astype(BF16)

    keys, vals = [], []
    y_norm = None
    for l in range(DEPTH):
        q, kx, vx, kf, vf, fcs, ga, gf = _inproj(l, x, norm1, mods, w_in_b, qg, kg, cos_t, sin_t, bd, cs)
        keys.append(kf[:N_CTX])
        vals.append(vf[:N_CTX])
        a = _attention_ctx(l, q, kx, vx)
        a = _attention_lat(l, a, kx, vx, kcache[:, l], vcache[:, l])
        x1 = _merge(l, "ctx", a, fcs, dft_ctx[0], dft_ctx[1], ga, gf, x, mods, wa_b, wf_b, wo_b)
        x1 = _merge(l, "lat", a, fcs, dft_lat[0], dft_lat[1], ga, gf, x1, mods, wa_b, wf_b, wo_b)
        h2, top_e, top_g = _router(l, x1, norm2, mods, wr, br)
        dest, buf_tok, blk_e, blk_valid = _plan(top_e)
        yb = _experts(l, blk_e, blk_valid, buf_tok, h2, w_exp_in, b_exp_in, w_exp_out, b_exp_out)
        if l == DEPTH - 1:
            x, y_norm = _combine(l, dest, yb, x1, top_g, mods, final_norm_g.reshape(1, D_MODEL))
        else:
            (x,) = _combine(l, dest, yb, x1, top_g, mods)

    y_prompt = y_norm[:N_CTX].reshape(BATCH, SEQ, D_MODEL)
    y_sample = y_norm[N_CTX:].reshape(DEC_BATCH, DEC_SEQ, D_MODEL)
    shape = (BATCH, SEQ, N_KV_HEADS, HEAD_DIM)
    new_k = jnp.stack([k.reshape(shape) for k in keys], axis=1)
    new_v = jnp.stack([v.reshape(shape) for v in vals], axis=1)
    return (y_prompt, y_sample, new_k, new_v)
```

```python
import functools
import math

import jax
import jax.numpy as jnp
from jax import lax
from jax.experimental import pallas as pl
from jax.experimental.pallas import tpu as pltpu

F32 = jnp.float32
BF16 = jnp.bfloat16
I32 = jnp.int32

D_MODEL = 1024
DEPTH = 4
BATCH = 16
SEQ = 256
DEC_BATCH = 2
DEC_SEQ = 2048
PAST_LEN = 512
GRID_W = 64
HEAD_DIM = 64
N_Q_HEADS = 12
N_KV_HEADS = 4
ATTN_W = N_Q_HEADS * HEAD_DIM
KV_W = N_KV_HEADS * HEAD_DIM
FOURIER_W = D_MODEL // 4
N_FOURIER_GROUPS = 4
FOURIER_GROUP_W = FOURIER_W // N_FOURIER_GROUPS
IN_W = ATTN_W + 2 * KV_W + FOURIER_W + 2 * D_MODEL
ROPE_AXIS_DIM = HEAD_DIM // 2
ROPE_THETA = 10000.0
N_EXPERTS = 32
TOP_K = 4
D_EXPERT = D_MODEL
SWIGLU_ALPHA = 1.702
SWIGLU_LIMIT = 7.0
NORM_EPS = 1e-6

N_CTX = BATCH * SEQ
N_LAT = DEC_BATCH * DEC_SEQ
N_TOK = N_CTX + N_LAT
LANES = 128
ROW_TILE = 256
N_TILES = N_TOK // ROW_TILE
CTX_TILES = N_CTX // ROW_TILE
LAT_TILES_PER_BATCH = DEC_SEQ // ROW_TILE
N_GROUPS = 1 + DEC_BATCH
EXPERT_TILE = 256
N_ASSIGN = N_TOK * TOP_K
N_EXPERT_BLOCKS = N_ASSIGN // EXPERT_TILE + N_EXPERTS
M_PAD = N_EXPERT_BLOCKS * EXPERT_TILE
Q_CHUNKS = ATTN_W // LANES
NEG_BIG = -1e30
VMEM_LIMIT = 56 * 1024 * 1024

_NT = (((1,), (1,)), ((), ()))


def _params(sem, vmem=VMEM_LIMIT):
    return pltpu.CompilerParams(dimension_semantics=sem, vmem_limit_bytes=vmem)


def _tile_group(i):
    return jnp.where(i < CTX_TILES, 0, 1 + (i - CTX_TILES) // LAT_TILES_PER_BATCH)


def _rope_block(i):
    return jnp.where(i < CTX_TILES, 0, 1 + (i - CTX_TILES) % LAT_TILES_PER_BATCH)


def _mod_spec(layer, slot):
    return pl.BlockSpec((None, 1, D_MODEL),
                        lambda i: ((layer * N_GROUPS + _tile_group(i)) * 6 + slot, 0, 0))


def _layer_vec_spec(layer, width=D_MODEL):
    return pl.BlockSpec((None, 1, width), lambda *_: (layer, 0, 0))


def _mods_kernel(cond_ref, w_ref, b_ref, o_ref):
    c = cond_ref[...]
    s = c * jax.nn.sigmoid(c)
    o_ref[...] = jnp.dot(s, w_ref[...], preferred_element_type=F32,
                         precision=lax.Precision.HIGHEST) + b_ref[...]


def _mods(cond, w_ada, b_ada):
    tn = 1536
    return pl.pallas_call(
        _mods_kernel,
        grid=(DEPTH, 6 * D_MODEL // tn),
        in_specs=[pl.BlockSpec((8, D_MODEL), lambda l, j: (0, 0)),
                  pl.BlockSpec((None, D_MODEL, tn), lambda l, j: (l, 0, j)),
                  pl.BlockSpec((None, 1, tn), lambda l, j: (l, 0, j))],
        out_specs=pl.BlockSpec((None, 8, tn), lambda l, j: (l, 0, j)),
        out_shape=jax.ShapeDtypeStruct((DEPTH, 8, 6 * D_MODEL), F32),
        compiler_params=_params(("parallel", "parallel")),
        name="adaln_mods",
    )(cond, w_ada, b_ada.reshape(DEPTH, 1, 6 * D_MODEL))


def _inproj_kernel(x_ref, g_ref, sh_ref, sc_ref, w_ref, qg_ref, kg_ref, cos_ref, sin_ref,
                   bd_ref, cs_ref,
                   q_ref, kx_ref, vx_ref, kf_ref, vf_ref, fcs_ref, ga_ref, gf_ref, h_scr):
    x = x_ref[...]
    y = x * lax.rsqrt(jnp.mean(x * x, axis=-1, keepdims=True) + NORM_EPS) * g_ref[...]
    h_scr[...] = (y * (1.0 + sc_ref[...]) + sh_ref[...]).astype(BF16)
    hb = h_scr[...]
    lane = lax.broadcasted_iota(I32, (ROW_TILE, LANES), 1)
    low_half = lane < HEAD_DIM
    rot_first = (lane % ROPE_AXIS_DIM) < (ROPE_AXIS_DIM // 2)
    cos = cos_ref[...]
    sin = sin_ref[...]
    bd = bd_ref[...]

    def proj(lo, width):
        return jnp.dot(hb, w_ref[:, lo:lo + width], preferred_element_type=F32)

    def head_norm_rope(p, gain):
        pp = p * p
        hi = pp.astype(BF16)
        lo = (pp - hi.astype(F32)).astype(BF16)
        msq = (jnp.dot(hi, bd, preferred_element_type=F32)
               + jnp.dot(lo, bd, preferred_element_type=F32))
        n = p * lax.rsqrt(msq + NORM_EPS) * gain
        partner = jnp.where(rot_first, pltpu.roll(n, LANES - ROPE_AXIS_DIM // 2, 1),
                            pltpu.roll(n, ROPE_AXIS_DIM // 2, 1))
        return n * cos + partner * sin

    def head_pairs(r):
        sw = pltpu.roll(r, HEAD_DIM, 1)
        return jnp.where(low_half, r, sw), r, jnp.where(low_half, sw, r)

    scale = HEAD_DIM ** -0.5
    wide = 2 * LANES
    for c2 in range(ATTN_W // wide):
        p = proj(c2 * wide, wide)
        for t in range(2):
            c = 2 * c2 + t
            r = head_norm_rope(p[:, t * LANES:(t + 1) * LANES], qg_ref[...])
            q_ref[:, c * LANES:(c + 1) * LANES] = (r * scale).astype(BF16)

    p = proj(ATTN_W, KV_W)
    for j in range(KV_W // LANES):
        r = head_norm_rope(p[:, j * LANES:(j + 1) * LANES], kg_ref[...])
        kf_ref[:, j * LANES:(j + 1) * LANES] = r
        for t, piece in enumerate(head_pairs(r)):
            c = 3 * j + t
            kx_ref[:, c * LANES:(c + 1) * LANES] = piece.astype(BF16)

    p = proj(ATTN_W + KV_W, KV_W)
    vf_ref[...] = p
    for j in range(KV_W // LANES):
        for t, piece in enumerate(head_pairs(p[:, j * LANES:(j + 1) * LANES])):
            c = 3 * j + t
            vx_ref[:, c * LANES:(c + 1) * LANES] = piece.astype(BF16)

    f = proj(ATTN_W + 2 * KV_W, FOURIER_W)
    fcs_ref[...] = jnp.dot(f.astype(BF16), cs_ref[...], preferred_element_type=F32).astype(BF16)

    gate_lo = ATTN_W + 2 * KV_W + FOURIER_W
    half = D_MODEL // 2
    for j in range(2):
        ga_ref[:, j * half:(j + 1) * half] = jax.nn.sigmoid(proj(gate_lo + j * half, half)).astype(BF16)
        gf_ref[:, j * half:(j + 1) * half] = jax.nn.sigmoid(
            proj(gate_lo + D_MODEL + j * half, half)).astype(BF16)


def _inproj(layer, x, norm_g, mods, w_in, qg, kg, cos_tab, sin_tab, bd, cs):
    row = lambda w: pl.BlockSpec((ROW_TILE, w), lambda i: (i, 0))
    const = lambda a: pl.BlockSpec(a.shape, lambda i: (0,) * a.ndim)
    ext_w = Q_CHUNKS * LANES
    return pl.pallas_call(
        _inproj_kernel,
        grid=(N_TILES,),
        in_specs=[row(D_MODEL), _layer_vec_spec(layer), _mod_spec(layer, 0), _mod_spec(layer, 1),
                  pl.BlockSpec((None, D_MODEL, IN_W), lambda i: (layer, 0, 0)),
                  _layer_vec_spec(layer, LANES), _layer_vec_spec(layer, LANES),
                  pl.BlockSpec((ROW_TILE, LANES), lambda i: (_rope_block(i), 0)),
                  pl.BlockSpec((ROW_TILE, LANES), lambda i: (_rope_block(i), 0)),
                  const(bd), const(cs)],
        out_specs=[row(ATTN_W), row(ext_w), row(ext_w), row(KV_W), row(KV_W),
                   row(2 * FOURIER_W), row(D_MODEL), row(D_MODEL)],
        out_shape=[jax.ShapeDtypeStruct((N_TOK, ATTN_W), BF16),
                   jax.ShapeDtypeStruct((N_TOK, ext_w), BF16),
                   jax.ShapeDtypeStruct((N_TOK, ext_w), BF16),
                   jax.ShapeDtypeStruct((N_TOK, KV_W), F32),
                   jax.ShapeDtypeStruct((N_TOK, KV_W), F32),
                   jax.ShapeDtypeStruct((N_TOK, 2 * FOURIER_W), BF16),
                   jax.ShapeDtypeStruct((N_TOK, D_MODEL), BF16),
                   jax.ShapeDtypeStruct((N_TOK, D_MODEL), BF16)],
        scratch_shapes=[pltpu.VMEM((ROW_TILE, D_MODEL), BF16)],
        compiler_params=_params(("parallel",)),
        name=f"inproj_l{layer}",
    )(x, norm_g, mods, mods, w_in, qg, kg, cos_tab, sin_tab, bd, cs)


def _attn_kernel(*refs, n_parts):
    q_ref = refs[0]
    k_refs = refs[1:1 + n_parts]
    v_refs = refs[1 + n_parts:1 + 2 * n_parts]
    o_ref = refs[-1]
    tq = q_ref.shape[0]
    low_half = lax.broadcasted_iota(I32, (tq, LANES), 1) < HEAD_DIM
    for c in range(Q_CHUNKS):
        cols = slice(c * LANES, (c + 1) * LANES)
        qc = q_ref[:, cols]
        zero = jnp.zeros_like(qc)
        outs = []
        for mask in (low_half, jnp.logical_not(low_half)):
            qm = jnp.where(mask, qc, zero)
            ss = [lax.dot_general(qm, k[:, cols], _NT, preferred_element_type=F32) for k in k_refs]
            m = ss[0].max(axis=-1, keepdims=True)
            for s in ss[1:]:
                m = jnp.maximum(m, s.max(axis=-1, keepdims=True))
            ps = [jnp.exp(s - m) for s in ss]
            den = ps[0].sum(axis=-1, keepdims=True)
            for p in ps[1:]:
                den = den + p.sum(axis=-1, keepdims=True)
            acc = jnp.dot(ps[0].astype(BF16), v_refs[0][:, cols], preferred_element_type=F32)
            for p, v in zip(ps[1:], v_refs[1:]):
                acc = acc + jnp.dot(p.astype(BF16), v[:, cols], preferred_element_type=F32)
            outs.append(acc / den)
        o_ref[:, cols] = jnp.where(low_half, outs[0], outs[1]).astype(BF16)


def _attention_ctx(layer, q, kx, vx):
    w = Q_CHUNKS * LANES
    blk = lambda width: pl.BlockSpec((SEQ, width), lambda b: (b, 0))
    return pl.pallas_call(
        functools.partial(_attn_kernel, n_parts=1),
        grid=(BATCH,),
        in_specs=[blk(ATTN_W), blk(w), blk(w)],
        out_specs=blk(ATTN_W),
        out_shape=jax.ShapeDtypeStruct((N_TOK, ATTN_W), BF16),
        input_output_aliases={0: 0},
        compiler_params=_params(("parallel",)),
        name=f"attn_ctx_l{layer}",
    )(q, kx, vx)


def _attention_lat(layer, q, kx, vx, kcache, vcache):
    w = Q_CHUNKS * LANES
    nq = DEC_SEQ // ROW_TILE
    first = N_CTX // ROW_TILE
    qblk = pl.BlockSpec((ROW_TILE, ATTN_W), lambda b, i: (first + b * nq + i, 0))
    new = pl.BlockSpec((DEC_SEQ, w), lambda b, i: (N_CTX // DEC_SEQ + b, 0))
    old = pl.BlockSpec((None, PAST_LEN, w), lambda b, i: (b, 0, 0))
    return pl.pallas_call(
        functools.partial(_attn_kernel, n_parts=2),
        grid=(DEC_BATCH, nq),
        in_specs=[qblk, old, new, old, new],
        out_specs=qblk,
        out_shape=jax.ShapeDtypeStruct((N_TOK, ATTN_W), BF16),
        input_output_aliases={0: 0},
        compiler_params=_params(("parallel", "parallel")),
        name=f"attn_lat_l{layer}",
    )(q, kcache, kx, vcache, vx)


def _merge_kernel(a_ref, fc_ref, fs_ref, cl_ref, sl_ref, ga_ref, gf_ref, x_ref, gate_ref,
                  wa_ref, wf_ref, wo_ref, o_ref):
    attn = jnp.dot(a_ref[...], wa_ref[...], preferred_element_type=F32)
    fo = (jnp.dot(cl_ref[...], fc_ref[...], preferred_element_type=F32)
          - jnp.dot(sl_ref[...], fs_ref[...], preferred_element_type=F32))
    four = jnp.dot(fo.astype(BF16), wf_ref[...], preferred_element_type=F32)
    merged = ga_ref[...].astype(F32) * attn + gf_ref[...].astype(F32) * four
    mix = jnp.dot(merged.astype(BF16), wo_ref[...], preferred_element_type=F32)
    o_ref[...] = x_ref[...] + gate_ref[...] * mix


def _merge(layer, stream, a, fcs, cl, sl, ga, gf, x, mods, wa, wf, wo):
    if stream == "ctx":
        seq, n_batch, first_tile, first_seq = SEQ, BATCH, 0, 0
    else:
        seq, n_batch, first_tile, first_seq = DEC_SEQ, DEC_BATCH, CTX_TILES, N_CTX // DEC_SEQ
    nt = seq // ROW_TILE
    tile = lambda b, i: first_tile + b * nt + i
    row = lambda w: pl.BlockSpec((ROW_TILE, w), lambda b, i: (tile(b, i), 0))
    wspec = lambda a_: pl.BlockSpec((None,) + a_.shape[1:], lambda b, i: (layer, 0, 0))
    in_specs = [row(ATTN_W),
                pl.BlockSpec((seq, FOURIER_W), lambda b, i: (first_seq + b, 0)),
                pl.BlockSpec((seq, FOURIER_W), lambda b, i: (first_seq + b, 1)),
                pl.BlockSpec((ROW_TILE, seq), lambda b, i: (i, 0)),
                pl.BlockSpec((ROW_TILE, seq), lambda b, i: (i, 0)),
                row(D_MODEL), row(D_MODEL), row(D_MODEL),
                pl.BlockSpec((None, 1, D_MODEL),
                             lambda b, i: ((layer * N_GROUPS + _tile_group(tile(b, i))) * 6 + 2, 0, 0)),
                wspec(wa), wspec(wf), wspec(wo)]
    args = [a, fcs, fcs, cl, sl, ga, gf, x, mods, wa, wf, wo]
    aliases = {7: 0}
    return pl.pallas_call(
        _merge_kernel,
        grid=(n_batch, nt),
        in_specs=in_specs,
        out_specs=row(D_MODEL),
        out_shape=jax.ShapeDtypeStruct((N_TOK, D_MODEL), F32),
        input_output_aliases=aliases,
        compiler_params=_params(("parallel", "parallel")),
        name=f"merge_{stream}_l{layer}",
    )(*args)


def _router_kernel(x_ref, g_ref, sh_ref, sc_ref, wr_ref, br_ref, tri_ref,
                   h_ref, te_ref, tg_ref, rk_ref, cnt_ref, carry):
    @pl.when(pl.program_id(0) == 0)
    def _():
        carry[...] = jnp.zeros_like(carry)

    x = x_ref[...]
    y = x * lax.rsqrt(jnp.mean(x * x, axis=-1, keepdims=True) + NORM_EPS) * g_ref[...]
    h = y * (1.0 + sc_ref[...]) + sh_ref[...]
    h_ref[...] = h
    logits = lax.dot_general(wr_ref[...], h, _NT, preferred_element_type=F32,
                             precision=lax.Precision.HIGHEST) + br_ref[...]
    sub = lax.broadcasted_iota(I32, logits.shape, 0).astype(F32)
    vals, ids = [], []
    for _ in range(TOP_K):
        m = logits.max(axis=0, keepdims=True)
        idx = jnp.where(logits == m, sub, float(LANES)).min(axis=0, keepdims=True)
        vals.append(m)
        ids.append(idx)
        logits = jnp.where(sub == idx, NEG_BIG, logits)
    es = [jnp.exp(v - vals[0]) for v in vals]
    den = es[0] + es[1] + es[2] + es[3]

    chosen = [sub == idx for idx in ids]
    member = jnp.zeros_like(sub)
    for ch in chosen:
        member = member + ch.astype(F32)
    upto = jnp.dot(member.astype(BF16), tri_ref[...], preferred_element_type=F32)
    before = carry[...] + upto - member
    for k in range(TOP_K):
        te_ref[k:k + 1, :] = ids[k].astype(I32)
        tg_ref[k:k + 1, :] = es[k] / den
        rk_ref[k:k + 1, :] = jnp.where(chosen[k], before, 0.0).sum(axis=0, keepdims=True).astype(I32)
    carry[...] = carry[...] + member.sum(axis=1, keepdims=True)
    cnt_ref[...] = carry[:, :LANES].astype(I32)


def _router(layer, x1, norm_g, mods, wr_t, br_t, tri):
    row = lambda w: pl.BlockSpec((ROW_TILE, w), lambda i: (i, 0))
    col = pl.BlockSpec((TOP_K, ROW_TILE), lambda i: (0, i))
    return pl.pallas_call(
        _router_kernel,
        grid=(N_TILES,),
        in_specs=[row(D_MODEL), _layer_vec_spec(layer), _mod_spec(layer, 3), _mod_spec(layer, 4),
                  pl.BlockSpec((None, LANES, D_MODEL), lambda i: (layer, 0, 0)),
                  pl.BlockSpec((None, LANES, 1), lambda i: (layer, 0, 0)),
                  pl.BlockSpec((ROW_TILE, ROW_TILE), lambda i: (0, 0))],
        out_specs=[row(D_MODEL), col, col, col, pl.BlockSpec((LANES, LANES), lambda i: (0, 0))],
        out_shape=[jax.ShapeDtypeStruct((N_TOK, D_MODEL), F32),
                   jax.ShapeDtypeStruct((TOP_K, N_TOK), I32),
                   jax.ShapeDtypeStruct((TOP_K, N_TOK), F32),
                   jax.ShapeDtypeStruct((TOP_K, N_TOK), I32),
                   jax.ShapeDtypeStruct((LANES, LANES), I32)],
        scratch_shapes=[pltpu.VMEM((LANES, ROW_TILE), F32)],
        compiler_params=_params(("arbitrary",)),
        name=f"router_l{layer}",
    )(x1, norm_g, mods, mods, wr_t, br_t, tri)


def _dispatch_kernel(dest_ref, zblk_ref, nblk_ref, h_ref, xs_hbm, zeros, sem):
    i = pl.program_id(0)

    def zero_fill(block):
        rows = pl.ds(pl.multiple_of(block * EXPERT_TILE, EXPERT_TILE), EXPERT_TILE)
        return pltpu.make_async_copy(zeros, xs_hbm.at[rows, :], sem)

    @pl.when(i == 0)
    def _():
        zeros[...] = jnp.zeros_like(zeros)
        for wait in (False, True):
            for e in range(N_EXPERTS):
                @pl.when(zblk_ref[e] >= 0)
                def _():
                    cp = zero_fill(zblk_ref[e])
                    cp.wait() if wait else cp.start()

            def tail(j, carry):
                cp = zero_fill(j)
                cp.wait() if wait else cp.start()
                return carry

            lax.fori_loop(nblk_ref[0], N_EXPERT_BLOCKS, tail, 0)

    copies = []
    for k in range(TOP_K):
        for r in range(ROW_TILE):
            d = dest_ref[k * N_TOK + i * ROW_TILE + r]
            copies.append(pltpu.make_async_copy(h_ref.at[pl.ds(r, 1), :], xs_hbm.at[pl.ds(d, 1), :], sem))
    for cp in copies:
        cp.start()
    for cp in copies:
        cp.wait()


def _dispatch(layer, dest, zblk, nblk, h2):
    grid_spec = pltpu.PrefetchScalarGridSpec(
        num_scalar_prefetch=3,
        grid=(N_TILES,),
        in_specs=[pl.BlockSpec((ROW_TILE, D_MODEL), lambda i, d, z, n: (i, 0))],
        out_specs=pl.BlockSpec(memory_space=pl.ANY),
        scratch_shapes=[pltpu.VMEM((EXPERT_TILE, D_MODEL), F32), pltpu.SemaphoreType.DMA(())],
    )
    return pl.pallas_call(
        _dispatch_kernel,
        grid_spec=grid_spec,
        out_shape=jax.ShapeDtypeStruct((M_PAD, D_MODEL), F32),
        compiler_params=_params(("arbitrary",)),
        name=f"dispatch_l{layer}",
    )(dest, zblk, nblk, h2)


def _expert_kernel(be_ref, bv_ref, x_ref, wi_ref, bi_ref, wo_ref, bo_ref, y_ref, wi_bf, wo_bf):
    b = pl.program_id(0)

    @pl.when(bv_ref[b] == 0)
    def _():
        y_ref[...] = jnp.zeros_like(y_ref)

    @pl.when(bv_ref[b] == 1)
    def _():
        changed = jnp.logical_or(b == 0, be_ref[b] != be_ref[jnp.maximum(b - 1, 0)])

        @pl.when(changed)
        def _():
            wi_bf[...] = wi_ref[...].astype(BF16)
            wo_bf[...] = wo_ref[...].astype(BF16)

        x = x_ref[...].astype(BF16)
        hdn = jnp.dot(x, wi_bf[...], preferred_element_type=F32) + bi_ref[...]
        glu = jnp.minimum(hdn[:, :D_EXPERT], SWIGLU_LIMIT)
        lin = jnp.clip(hdn[:, D_EXPERT:], -SWIGLU_LIMIT, SWIGLU_LIMIT)
        act = glu * jax.nn.sigmoid(SWIGLU_ALPHA * glu) * (lin + 1.0)
        y_ref[...] = jnp.dot(act.astype(BF16), wo_bf[...], preferred_element_type=F32) + bo_ref[...]


def _experts(layer, blk_e, blk_valid, xs, w_exp_in, b_exp_in, w_exp_out, b_exp_out):
    wspec = lambda rows, cols: pl.BlockSpec((None, None, rows, cols), lambda b, be, bv: (layer, be[b], 0, 0))
    blk = pl.BlockSpec((EXPERT_TILE, D_MODEL), lambda b, be, bv: (b, 0))
    grid_spec = pltpu.PrefetchScalarGridSpec(
        num_scalar_prefetch=2,
        grid=(N_EXPERT_BLOCKS,),
        in_specs=[blk, wspec(D_MODEL, 2 * D_EXPERT), wspec(1, 2 * D_EXPERT),
                  wspec(D_EXPERT, D_MODEL), wspec(1, D_MODEL)],
        out_specs=blk,
        scratch_shapes=[pltpu.VMEM((D_MODEL, 2 * D_EXPERT), BF16),
                        pltpu.VMEM((D_EXPERT, D_MODEL), BF16)],
    )
    return pl.pallas_call(
        _expert_kernel,
        grid_spec=grid_spec,
        out_shape=jax.ShapeDtypeStruct((M_PAD, D_MODEL), F32),
        compiler_params=_params(("arbitrary",)),
        name=f"experts_l{layer}",
    )(blk_e, blk_valid, xs, w_exp_in,
      b_exp_in.reshape(DEPTH, N_EXPERTS, 1, 2 * D_EXPERT), w_exp_out,
      b_exp_out.reshape(DEPTH, N_EXPERTS, 1, D_MODEL))


def _combine_kernel(dest_ref, y_hbm, x_ref, tg_ref, gate_ref, *rest, final):
    if final:
        fg_ref, o_ref, n_ref, buf, sem = rest
    else:
        o_ref, buf, sem = rest
    first = pl.program_id(0) * ROW_TILE
    copies = []
    for k in range(TOP_K):
        for r in range(ROW_TILE):
            d = dest_ref[k * N_TOK + first + r]
            copies.append(pltpu.make_async_copy(y_hbm.at[pl.ds(d, 1), :], buf.at[k, pl.ds(r, 1), :], sem))
    for cp in copies:
        cp.start()
    for cp in copies:
        cp.wait()

    tg = tg_ref[...]
    acc = tg[:, 0:1] * buf[0]
    for k in range(1, TOP_K):
        acc = acc + tg[:, k:k + 1] * buf[k]
    out = x_ref[...] + gate_ref[...] * acc
    o_ref[...] = out
    if final:
        n_ref[...] = out * lax.rsqrt(jnp.mean(out * out, axis=-1, keepdims=True) + NORM_EPS) * fg_ref[...]


def _combine(layer, dest, yb, x1, tg, mods, final_g=None):
    final = final_g is not None
    row = lambda w: pl.BlockSpec((ROW_TILE, w), lambda i, d: (i, 0))
    in_specs = [pl.BlockSpec(memory_space=pl.ANY), row(D_MODEL), row(TOP_K),
                pl.BlockSpec((None, 1, D_MODEL),
                             lambda i, d: ((layer * N_GROUPS + _tile_group(i)) * 6 + 5, 0, 0))]
    args = [dest, yb, x1, tg, mods]
    out_specs = [row(D_MODEL)]
    out_shape = [jax.ShapeDtypeStruct((N_TOK, D_MODEL), F32)]
    if final:
        in_specs.append(pl.BlockSpec((1, D_MODEL), lambda i, d: (0, 0)))
        args.append(final_g)
        out_specs.append(row(D_MODEL))
        out_shape.append(jax.ShapeDtypeStruct((N_TOK, D_MODEL), F32))
    grid_spec = pltpu.PrefetchScalarGridSpec(
        num_scalar_prefetch=1,
        grid=(N_TILES,),
        in_specs=in_specs,
        out_specs=out_specs,
        scratch_shapes=[pltpu.VMEM((TOP_K, ROW_TILE, D_MODEL), F32), pltpu.SemaphoreType.DMA(())],
    )
    return pl.pallas_call(
        functools.partial(_combine_kernel, final=final),
        grid_spec=grid_spec,
        out_shape=out_shape,
        compiler_params=_params(("arbitrary",)),
        name=f"combine_l{layer}",
    )(*args)


def _rope_tables():
    rows = DEC_SEQ // GRID_W
    row = jnp.repeat(jnp.arange(rows, dtype=F32), GRID_W)
    col = jnp.tile(jnp.arange(GRID_W, dtype=F32), rows)
    inv_freq = ROPE_THETA ** (-jnp.arange(0, ROPE_AXIS_DIM, 2, dtype=F32) / ROPE_AXIS_DIM)
    ang = jnp.stack([row[:, None] * inv_freq, col[:, None] * inv_freq], axis=1)
    cos, sin = jnp.cos(ang), jnp.sin(ang)
    cos_h = jnp.concatenate([cos, cos], axis=-1).reshape(DEC_SEQ, HEAD_DIM)
    sin_h = jnp.concatenate([-sin, sin], axis=-1).reshape(DEC_SEQ, HEAD_DIM)
    reps = LANES // HEAD_DIM
    cos_t = jnp.concatenate([jnp.ones((ROW_TILE, LANES), F32), jnp.tile(cos_h, (1, reps))], axis=0)
    sin_t = jnp.concatenate([jnp.zeros((ROW_TILE, LANES), F32), jnp.tile(sin_h, (1, reps))], axis=0)
    return cos_t, sin_t


def _dft_tables(n, scale):
    idx = jnp.arange(n, dtype=I32)
    ang = ((idx[:, None] * idx[None, :]) % n).astype(F32) * (2.0 * math.pi / n)
    return jnp.cos(ang) * scale, jnp.sin(ang) * scale


def _channel_dft():
    c, s = _dft_tables(FOURIER_GROUP_W, 1.0)
    eye = jnp.eye(N_FOURIER_GROUPS, dtype=F32)
    return jnp.concatenate([jnp.kron(eye, c), jnp.kron(eye, s)], axis=1).astype(BF16)


def _extend_heads(t):
    h = [t[..., i * HEAD_DIM:(i + 1) * HEAD_DIM] for i in range(N_KV_HEADS)]
    order = [0, 0, 0, 1, 1, 1, 2, 2, 2, 3, 3, 3]
    return jnp.concatenate([h[i] for i in order], axis=-1)


def _plan(top_e_t, rank_t, counts):
    cnt = counts[:N_EXPERTS, 0]
    padded = (cnt + EXPERT_TILE - 1) // EXPERT_TILE * EXPERT_TILE
    pad_end = jnp.cumsum(padded)
    pad_start = pad_end - padded
    dest = (pad_start[top_e_t] + rank_t).reshape(-1).astype(I32)
    blk_start = jnp.arange(N_EXPERT_BLOCKS, dtype=I32) * EXPERT_TILE
    blk_e = jnp.minimum(jnp.searchsorted(pad_end, blk_start, side="right"), N_EXPERTS - 1).astype(I32)
    blk_valid = (blk_start < pad_end[-1]).astype(I32)
    zero_blk = jnp.where(cnt % EXPERT_TILE != 0, (pad_start + cnt) // EXPERT_TILE, -1).astype(I32)
    n_blk = (pad_end[-1:] // EXPERT_TILE).astype(I32)
    return dest, blk_e, blk_valid, zero_blk, n_blk


def kernel(x_prompt, x_sample, cache_k, cache_v, c, c_ctx, w_ada, b_ada, norm1_g, w_in, q_norm_g,
           k_norm_g, w_attn_o, w_fourier_o, w_out, norm2_g, w_router, b_router, w_exp_in, b_exp_in,
           w_exp_out, b_exp_out, final_norm_g):
    x = jnp.concatenate([x_prompt.reshape(N_CTX, D_MODEL), x_sample.reshape(N_LAT, D_MODEL)], axis=0)

    cond = jnp.concatenate([c_ctx[None, :], c, jnp.zeros((8 - N_GROUPS, D_MODEL), F32)], axis=0)
    mods = _mods(cond, w_ada, b_ada)[:, :N_GROUPS].reshape(DEPTH * N_GROUPS * 6, 1, D_MODEL)

    cos_t, sin_t = _rope_tables()
    bd = jnp.kron(jnp.eye(LANES // HEAD_DIM, dtype=F32),
                  jnp.full((HEAD_DIM, HEAD_DIM), 1.0 / HEAD_DIM, F32)).astype(BF16)
    cs = _channel_dft()
    dft_ctx = [t.astype(BF16) for t in _dft_tables(SEQ, (SEQ * FOURIER_GROUP_W) ** -0.5)]
    dft_lat = [t.astype(BF16) for t in _dft_tables(DEC_SEQ, (DEC_SEQ * FOURIER_GROUP_W) ** -0.5)]

    w_in_b = w_in.astype(BF16)
    wa_b = w_attn_o.astype(BF16)
    wf_b = w_fourier_o.astype(BF16)
    wo_b = w_out.astype(BF16)
    norm1 = norm1_g.reshape(DEPTH, 1, D_MODEL)
    norm2 = norm2_g.reshape(DEPTH, 1, D_MODEL)
    qg = jnp.tile(q_norm_g, (1, LANES // HEAD_DIM)).reshape(DEPTH, 1, LANES)
    kg = jnp.tile(k_norm_g, (1, LANES // HEAD_DIM)).reshape(DEPTH, 1, LANES)
    wr_t = jnp.pad(jnp.swapaxes(w_router, 1, 2), ((0, 0), (0, LANES - N_EXPERTS), (0, 0)))
    br_t = jnp.pad(b_router, ((0, 0), (0, LANES - N_EXPERTS)), constant_values=NEG_BIG).reshape(DEPTH, LANES, 1)
    tok = jnp.arange(ROW_TILE, dtype=I32)
    tri = (tok[:, None] <= tok[None, :]).astype(BF16)
    kcache = _extend_heads(cache_k.reshape(DEC_BATCH, DEPTH, PAST_LEN, KV_W)).astype(BF16)
    vcache = _extend_heads(cache_v.reshape(DEC_BATCH, DEPTH, PAST_LEN, KV_W)).astype(BF16)

    keys, vals = [], []
    y_norm = None
    for l in range(DEPTH):
        q, kx, vx, kf, vf, fcs, ga, gf = _inproj(l, x, norm1, mods, w_in_b, qg, kg, cos_t, sin_t, bd, cs)
        keys.append(kf[:N_CTX])
        vals.append(vf[:N_CTX])
        a = _attention_ctx(l, q, kx, vx)
        a = _attention_lat(l, a, kx, vx, kcache[:, l], vcache[:, l])
        x1 = _merge(l, "ctx", a, fcs, dft_ctx[0], dft_ctx[1], ga, gf, x, mods, wa_b, wf_b, wo_b)
        x1 = _merge(l, "lat", a, fcs, dft_lat[0], dft_lat[1], ga, gf, x1, mods, wa_b, wf_b, wo_b)
        h2, top_e, top_g, rank, counts = _router(l, x1, norm2, mods, wr_t, br_t, tri)
        dest, blk_e, blk_valid, zero_blk, n_blk = _plan(top_e, rank, counts)
        xs = _dispatch(l, dest, zero_blk, n_blk, h2)
        yb = _experts(l, blk_e, blk_valid, xs, w_exp_in, b_exp_in, w_exp_out, b_exp_out)
        top_g = top_g.T
        if l == DEPTH - 1:
            x, y_norm = _combine(l, dest, yb, x1, top_g, mods, final_norm_g.reshape(1, D_MODEL))
        else:
            (x,) = _combine(l, dest, yb, x1, top_g, mods)

    y_prompt = y_norm[:N_CTX].reshape(BATCH, SEQ, D_MODEL)
    y_sample = y_norm[N_CTX:].reshape(DEC_BATCH, DEC_SEQ, D_MODEL)
    shape = (BATCH, SEQ, N_KV_HEADS, HEAD_DIM)
    new_k = jnp.stack([k.reshape(shape) for k in keys], axis=1)
    new_v = jnp.stack([v.reshape(shape) for v in vals], axis=1)
    return (y_prompt, y_sample, new_k, new_v)
```

```python
import functools
import math

import jax
import jax.numpy as jnp
from jax import lax
from jax.experimental import pallas as pl
from jax.experimental.pallas import tpu as pltpu

F32 = jnp.float32
BF16 = jnp.bfloat16
I32 = jnp.int32

D_MODEL = 1024
DEPTH = 4
BATCH = 16
SEQ = 256
DEC_BATCH = 2
DEC_SEQ = 2048
PAST_LEN = 512
GRID_W = 64
HEAD_DIM = 64
N_Q_HEADS = 12
N_KV_HEADS = 4
ATTN_W = N_Q_HEADS * HEAD_DIM
KV_W = N_KV_HEADS * HEAD_DIM
FOURIER_W = D_MODEL // 4
N_FOURIER_GROUPS = 4
FOURIER_GROUP_W = FOURIER_W // N_FOURIER_GROUPS
IN_W = ATTN_W + 2 * KV_W + FOURIER_W + 2 * D_MODEL
ROPE_AXIS_DIM = HEAD_DIM // 2
ROPE_THETA = 10000.0
N_EXPERTS = 32
TOP_K = 4
D_EXPERT = D_MODEL
SWIGLU_ALPHA = 1.702
SWIGLU_LIMIT = 7.0
NORM_EPS = 1e-6

N_CTX = BATCH * SEQ
N_LAT = DEC_BATCH * DEC_SEQ
N_TOK = N_CTX + N_LAT
LANES = 128
ROW_TILE = 256
N_TILES = N_TOK // ROW_TILE
CTX_TILES = N_CTX // ROW_TILE
LAT_TILES_PER_BATCH = DEC_SEQ // ROW_TILE
N_GROUPS = 1 + DEC_BATCH
EXPERT_TILE = 256
N_ASSIGN = N_TOK * TOP_K
N_EXPERT_BLOCKS = N_ASSIGN // EXPERT_TILE + N_EXPERTS
M_PAD = N_EXPERT_BLOCKS * EXPERT_TILE
DMA_CHUNK = 32
Q_CHUNKS = ATTN_W // LANES
NEG_BIG = -1e30
VMEM_LIMIT = 56 * 1024 * 1024

_NT = (((1,), (1,)), ((), ()))


def _params(sem, vmem=VMEM_LIMIT):
    return pltpu.CompilerParams(dimension_semantics=sem, vmem_limit_bytes=vmem)


def _tile_group(i):
    return jnp.where(i < CTX_TILES, 0, 1 + (i - CTX_TILES) // LAT_TILES_PER_BATCH)


def _rope_block(i):
    return jnp.where(i < CTX_TILES, 0, 1 + (i - CTX_TILES) % LAT_TILES_PER_BATCH)


def _mod_spec(layer, slot):
    return pl.BlockSpec((None, 1, D_MODEL),
                        lambda i: ((layer * N_GROUPS + _tile_group(i)) * 6 + slot, 0, 0))


def _layer_vec_spec(layer, width=D_MODEL):
    return pl.BlockSpec((None, 1, width), lambda *_: (layer, 0, 0))


ROW_SUBLANES = D_MODEL // LANES


def _store_token_rows(ref, value):
    n = value.shape[0]
    for s in range(ROW_SUBLANES):
        ref[pl.ds(s, n, stride=ROW_SUBLANES), :] = value[:, s * LANES:(s + 1) * LANES]


def _load_token_rows(ref):
    n = ref.shape[0] // ROW_SUBLANES
    return jnp.concatenate([ref[pl.ds(s, n, stride=ROW_SUBLANES), :] for s in range(ROW_SUBLANES)], axis=1)


def _token_rows(token):
    return pl.ds(pl.multiple_of(token * ROW_SUBLANES, ROW_SUBLANES), ROW_SUBLANES)


def _mods_kernel(cond_ref, w_ref, b_ref, o_ref):
    c = cond_ref[...]
    s = c * jax.nn.sigmoid(c)
    o_ref[...] = jnp.dot(s, w_ref[...], preferred_element_type=F32,
                         precision=lax.Precision.HIGHEST) + b_ref[...]


def _mods(cond, w_ada, b_ada):
    tn = 1536
    return pl.pallas_call(
        _mods_kernel,
        grid=(DEPTH, 6 * D_MODEL // tn),
        in_specs=[pl.BlockSpec((8, D_MODEL), lambda l, j: (0, 0)),
                  pl.BlockSpec((None, D_MODEL, tn), lambda l, j: (l, 0, j)),
                  pl.BlockSpec((None, 1, tn), lambda l, j: (l, 0, j))],
        out_specs=pl.BlockSpec((None, 8, tn), lambda l, j: (l, 0, j)),
        out_shape=jax.ShapeDtypeStruct((DEPTH, 8, 6 * D_MODEL), F32),
        compiler_params=_params(("parallel", "parallel")),
        name="adaln_mods",
    )(cond, w_ada, b_ada.reshape(DEPTH, 1, 6 * D_MODEL))


def _inproj_kernel(x_ref, g_ref, sh_ref, sc_ref, w_ref, qg_ref, kg_ref, cos_ref, sin_ref,
                   bd_ref, cs_ref,
                   q_ref, kx_ref, vx_ref, kf_ref, vf_ref, fcs_ref, ga_ref, gf_ref, h_scr):
    x = x_ref[...]
    y = x * lax.rsqrt(jnp.mean(x * x, axis=-1, keepdims=True) + NORM_EPS) * g_ref[...]
    h_scr[...] = (y * (1.0 + sc_ref[...]) + sh_ref[...]).astype(BF16)
    hb = h_scr[...]
    lane = lax.broadcasted_iota(I32, (ROW_TILE, LANES), 1)
    low_half = lane < HEAD_DIM
    rot_first = (lane % ROPE_AXIS_DIM) < (ROPE_AXIS_DIM // 2)
    cos = cos_ref[...]
    sin = sin_ref[...]
    bd = bd_ref[...]

    def proj(lo, width):
        return jnp.dot(hb, w_ref[:, lo:lo + width], preferred_element_type=F32)

    def head_norm_rope(p, gain):
        pp = p * p
        hi = pp.astype(BF16)
        lo = (pp - hi.astype(F32)).astype(BF16)
        msq = (jnp.dot(hi, bd, preferred_element_type=F32)
               + jnp.dot(lo, bd, preferred_element_type=F32))
        n = p * lax.rsqrt(msq + NORM_EPS) * gain
        partner = jnp.where(rot_first, pltpu.roll(n, LANES - ROPE_AXIS_DIM // 2, 1),
                            pltpu.roll(n, ROPE_AXIS_DIM // 2, 1))
        return n * cos + partner * sin

    def head_pairs(r):
        sw = pltpu.roll(r, HEAD_DIM, 1)
        return jnp.where(low_half, r, sw), r, jnp.where(low_half, sw, r)

    scale = HEAD_DIM ** -0.5
    wide = 2 * LANES
    for c2 in range(ATTN_W // wide):
        p = proj(c2 * wide, wide)
        for t in range(2):
            c = 2 * c2 + t
            r = head_norm_rope(p[:, t * LANES:(t + 1) * LANES], qg_ref[...])
            q_ref[:, c * LANES:(c + 1) * LANES] = (r * scale).astype(BF16)

    p = proj(ATTN_W, KV_W)
    for j in range(KV_W // LANES):
        r = head_norm_rope(p[:, j * LANES:(j + 1) * LANES], kg_ref[...])
        kf_ref[:, j * LANES:(j + 1) * LANES] = r
        for t, piece in enumerate(head_pairs(r)):
            c = 3 * j + t
            kx_ref[:, c * LANES:(c + 1) * LANES] = piece.astype(BF16)

    p = proj(ATTN_W + KV_W, KV_W)
    vf_ref[...] = p
    for j in range(KV_W // LANES):
        for t, piece in enumerate(head_pairs(p[:, j * LANES:(j + 1) * LANES])):
            c = 3 * j + t
            vx_ref[:, c * LANES:(c + 1) * LANES] = piece.astype(BF16)

    f = proj(ATTN_W + 2 * KV_W, FOURIER_W)
    fcs_ref[...] = jnp.dot(f.astype(BF16), cs_ref[...], preferred_element_type=F32).astype(BF16)

    gate_lo = ATTN_W + 2 * KV_W + FOURIER_W
    half = D_MODEL // 2
    for j in range(2):
        ga_ref[:, j * half:(j + 1) * half] = jax.nn.sigmoid(proj(gate_lo + j * half, half)).astype(BF16)
        gf_ref[:, j * half:(j + 1) * half] = jax.nn.sigmoid(
            proj(gate_lo + D_MODEL + j * half, half)).astype(BF16)


def _inproj(layer, x, norm_g, mods, w_in, qg, kg, cos_tab, sin_tab, bd, cs):
    row = lambda w: pl.BlockSpec((ROW_TILE, w), lambda i: (i, 0))
    const = lambda a: pl.BlockSpec(a.shape, lambda i: (0,) * a.ndim)
    ext_w = Q_CHUNKS * LANES
    return pl.pallas_call(
        _inproj_kernel,
        grid=(N_TILES,),
        in_specs=[row(D_MODEL), _layer_vec_spec(layer), _mod_spec(layer, 0), _mod_spec(layer, 1),
                  pl.BlockSpec((None, D_MODEL, IN_W), lambda i: (layer, 0, 0)),
                  _layer_vec_spec(layer, LANES), _layer_vec_spec(layer, LANES),
                  pl.BlockSpec((ROW_TILE, LANES), lambda i: (_rope_block(i), 0)),
                  pl.BlockSpec((ROW_TILE, LANES), lambda i: (_rope_block(i), 0)),
                  const(bd), const(cs)],
        out_specs=[row(ATTN_W), row(ext_w), row(ext_w), row(KV_W), row(KV_W),
                   row(2 * FOURIER_W), row(D_MODEL), row(D_MODEL)],
        out_shape=[jax.ShapeDtypeStruct((N_TOK, ATTN_W), BF16),
                   jax.ShapeDtypeStruct((N_TOK, ext_w), BF16),
                   jax.ShapeDtypeStruct((N_TOK, ext_w), BF16),
                   jax.ShapeDtypeStruct((N_TOK, KV_W), F32),
                   jax.ShapeDtypeStruct((N_TOK, KV_W), F32),
                   jax.ShapeDtypeStruct((N_TOK, 2 * FOURIER_W), BF16),
                   jax.ShapeDtypeStruct((N_TOK, D_MODEL), BF16),
                   jax.ShapeDtypeStruct((N_TOK, D_MODEL), BF16)],
        scratch_shapes=[pltpu.VMEM((ROW_TILE, D_MODEL), BF16)],
        compiler_params=_params(("parallel",)),
        name=f"inproj_l{layer}",
    )(x, norm_g, mods, mods, w_in, qg, kg, cos_tab, sin_tab, bd, cs)


def _attn_kernel(*refs, n_parts):
    q_ref = refs[0]
    k_refs = refs[1:1 + n_parts]
    v_refs = refs[1 + n_parts:1 + 2 * n_parts]
    o_ref = refs[-1]
    tq = q_ref.shape[0]
    low_half = lax.broadcasted_iota(I32, (tq, LANES), 1) < HEAD_DIM
    for c in range(Q_CHUNKS):
        cols = slice(c * LANES, (c + 1) * LANES)
        qc = q_ref[:, cols]
        zero = jnp.zeros_like(qc)
        outs = []
        for mask in (low_half, jnp.logical_not(low_half)):
            qm = jnp.where(mask, qc, zero)
            ss = [lax.dot_general(qm, k[:, cols], _NT, preferred_element_type=F32) for k in k_refs]
            m = ss[0].max(axis=-1, keepdims=True)
            for s in ss[1:]:
                m = jnp.maximum(m, s.max(axis=-1, keepdims=True))
            ps = [jnp.exp(s - m) for s in ss]
            den = ps[0].sum(axis=-1, keepdims=True)
            for p in ps[1:]:
                den = den + p.sum(axis=-1, keepdims=True)
            acc = jnp.dot(ps[0].astype(BF16), v_refs[0][:, cols], preferred_element_type=F32)
            for p, v in zip(ps[1:], v_refs[1:]):
                acc = acc + jnp.dot(p.astype(BF16), v[:, cols], preferred_element_type=F32)
            outs.append(acc / den)
        o_ref[:, cols] = jnp.where(low_half, outs[0], outs[1]).astype(BF16)


def _attention_ctx(layer, q, kx, vx):
    w = Q_CHUNKS * LANES
    blk = lambda width: pl.BlockSpec((SEQ, width), lambda b: (b, 0))
    return pl.pallas_call(
        functools.partial(_attn_kernel, n_parts=1),
        grid=(BATCH,),
        in_specs=[blk(ATTN_W), blk(w), blk(w)],
        out_specs=blk(ATTN_W),
        out_shape=jax.ShapeDtypeStruct((N_TOK, ATTN_W), BF16),
        input_output_aliases={0: 0},
        compiler_params=_params(("parallel",)),
        name=f"attn_ctx_l{layer}",
    )(q, kx, vx)


def _attention_lat(layer, q, kx, vx, kcache, vcache):
    w = Q_CHUNKS * LANES
    nq = DEC_SEQ // ROW_TILE
    first = N_CTX // ROW_TILE
    qblk = pl.BlockSpec((ROW_TILE, ATTN_W), lambda b, i: (first + b * nq + i, 0))
    new = pl.BlockSpec((DEC_SEQ, w), lambda b, i: (N_CTX // DEC_SEQ + b, 0))
    old = pl.BlockSpec((None, PAST_LEN, w), lambda b, i: (b, 0, 0))
    return pl.pallas_call(
        functools.partial(_attn_kernel, n_parts=2),
        grid=(DEC_BATCH, nq),
        in_specs=[qblk, old, new, old, new],
        out_specs=qblk,
        out_shape=jax.ShapeDtypeStruct((N_TOK, ATTN_W), BF16),
        input_output_aliases={0: 0},
        compiler_params=_params(("parallel", "parallel")),
        name=f"attn_lat_l{layer}",
    )(q, kcache, kx, vcache, vx)


def _merge_kernel(a_ref, fc_ref, fs_ref, cl_ref, sl_ref, ga_ref, gf_ref, x_ref, gate_ref,
                  wa_ref, wf_ref, wo_ref, o_ref):
    attn = jnp.dot(a_ref[...], wa_ref[...], preferred_element_type=F32)
    fo = (jnp.dot(cl_ref[...], fc_ref[...], preferred_element_type=F32)
          - jnp.dot(sl_ref[...], fs_ref[...], preferred_element_type=F32))
    four = jnp.dot(fo.astype(BF16), wf_ref[...], preferred_element_type=F32)
    merged = ga_ref[...].astype(F32) * attn + gf_ref[...].astype(F32) * four
    mix = jnp.dot(merged.astype(BF16), wo_ref[...], preferred_element_type=F32)
    o_ref[...] = x_ref[...] + gate_ref[...] * mix


def _merge(layer, stream, a, fcs, cl, sl, ga, gf, x, mods, wa, wf, wo):
    if stream == "ctx":
        seq, n_batch, first_tile, first_seq = SEQ, BATCH, 0, 0
    else:
        seq, n_batch, first_tile, first_seq = DEC_SEQ, DEC_BATCH, CTX_TILES, N_CTX // DEC_SEQ
    nt = seq // ROW_TILE
    tile = lambda b, i: first_tile + b * nt + i
    row = lambda w: pl.BlockSpec((ROW_TILE, w), lambda b, i: (tile(b, i), 0))
    wspec = lambda a_: pl.BlockSpec((None,) + a_.shape[1:], lambda b, i: (layer, 0, 0))
    in_specs = [row(ATTN_W),
                pl.BlockSpec((seq, FOURIER_W), lambda b, i: (first_seq + b, 0)),
                pl.BlockSpec((seq, FOURIER_W), lambda b, i: (first_seq + b, 1)),
                pl.BlockSpec((ROW_TILE, seq), lambda b, i: (i, 0)),
                pl.BlockSpec((ROW_TILE, seq), lambda b, i: (i, 0)),
                row(D_MODEL), row(D_MODEL), row(D_MODEL),
                pl.BlockSpec((None, 1, D_MODEL),
                             lambda b, i: ((layer * N_GROUPS + _tile_group(tile(b, i))) * 6 + 2, 0, 0)),
                wspec(wa), wspec(wf), wspec(wo)]
    args = [a, fcs, fcs, cl, sl, ga, gf, x, mods, wa, wf, wo]
    aliases = {7: 0}
    return pl.pallas_call(
        _merge_kernel,
        grid=(n_batch, nt),
        in_specs=in_specs,
        out_specs=row(D_MODEL),
        out_shape=jax.ShapeDtypeStruct((N_TOK, D_MODEL), F32),
        input_output_aliases=aliases,
        compiler_params=_params(("parallel", "parallel")),
        name=f"merge_{stream}_l{layer}",
    )(*args)


def _router_kernel(x_ref, g_ref, sh_ref, sc_ref, wr_ref, br_ref, tri_ref,
                   h_ref, te_ref, tg_ref, rk_ref, cnt_ref, carry):
    @pl.when(pl.program_id(0) == 0)
    def _():
        carry[...] = jnp.zeros_like(carry)

    x = x_ref[...]
    y = x * lax.rsqrt(jnp.mean(x * x, axis=-1, keepdims=True) + NORM_EPS) * g_ref[...]
    h = y * (1.0 + sc_ref[...]) + sh_ref[...]
    _store_token_rows(h_ref, h)
    logits = lax.dot_general(wr_ref[...], h, _NT, preferred_element_type=F32,
                             precision=lax.Precision.HIGHEST) + br_ref[...]
    sub = lax.broadcasted_iota(I32, logits.shape, 0).astype(F32)
    vals, ids = [], []
    for _ in range(TOP_K):
        m = logits.max(axis=0, keepdims=True)
        idx = jnp.where(logits == m, sub, float(LANES)).min(axis=0, keepdims=True)
        vals.append(m)
        ids.append(idx)
        logits = jnp.where(sub == idx, NEG_BIG, logits)
    es = [jnp.exp(v - vals[0]) for v in vals]
    den = es[0] + es[1] + es[2] + es[3]

    chosen = [sub == idx for idx in ids]
    member = jnp.zeros_like(sub)
    for ch in chosen:
        member = member + ch.astype(F32)
    upto = jnp.dot(member.astype(BF16), tri_ref[...], preferred_element_type=F32)
    before = carry[...] + upto - member
    for k in range(TOP_K):
        te_ref[k:k + 1, :] = ids[k].astype(I32)
        tg_ref[k:k + 1, :] = es[k] / den
        rk_ref[k:k + 1, :] = jnp.where(chosen[k], before, 0.0).sum(axis=0, keepdims=True).astype(I32)
    carry[...] = carry[...] + member.sum(axis=1, keepdims=True)
    cnt_ref[...] = carry[:, :LANES].astype(I32)


def _router(layer, x1, norm_g, mods, wr_t, br_t, tri):
    row = lambda w: pl.BlockSpec((ROW_TILE, w), lambda i: (i, 0))
    col = pl.BlockSpec((TOP_K, ROW_TILE), lambda i: (0, i))
    return pl.pallas_call(
        _router_kernel,
        grid=(N_TILES,),
        in_specs=[row(D_MODEL), _layer_vec_spec(layer), _mod_spec(layer, 3), _mod_spec(layer, 4),
                  pl.BlockSpec((None, LANES, D_MODEL), lambda i: (layer, 0, 0)),
                  pl.BlockSpec((None, LANES, 1), lambda i: (layer, 0, 0)),
                  pl.BlockSpec((ROW_TILE, ROW_TILE), lambda i: (0, 0))],
        out_specs=[pl.BlockSpec((ROW_TILE * ROW_SUBLANES, LANES), lambda i: (i, 0)), col, col, col,
                   pl.BlockSpec((LANES, LANES), lambda i: (0, 0))],
        out_shape=[jax.ShapeDtypeStruct((N_TOK * ROW_SUBLANES, LANES), F32),
                   jax.ShapeDtypeStruct((TOP_K, N_TOK), I32),
                   jax.ShapeDtypeStruct((TOP_K, N_TOK), F32),
                   jax.ShapeDtypeStruct((TOP_K, N_TOK), I32),
                   jax.ShapeDtypeStruct((LANES, LANES), I32)],
        scratch_shapes=[pltpu.VMEM((LANES, ROW_TILE), F32)],
        compiler_params=_params(("arbitrary",)),
        name=f"router_l{layer}",
    )(x1, norm_g, mods, mods, wr_t, br_t, tri)


def _dispatch_kernel(dest_ref, zblk_ref, nblk_ref, h_hbm, xs_hbm, zeros, sem):
    i = pl.program_id(0)
    block_rows = EXPERT_TILE * ROW_SUBLANES

    def zero_fill(block):
        rows = pl.ds(pl.multiple_of(block * block_rows, block_rows), block_rows)
        return pltpu.make_async_copy(zeros, xs_hbm.at[rows, :], sem)

    @pl.when(i == 0)
    def _():
        zeros[...] = jnp.zeros_like(zeros)
        for wait in (False, True):
            for e in range(N_EXPERTS):
                @pl.when(zblk_ref[e] >= 0)
                def _():
                    cp = zero_fill(zblk_ref[e])
                    cp.wait() if wait else cp.start()

            def tail(j, carry):
                cp = zero_fill(j)
                cp.wait() if wait else cp.start()
                return carry

            lax.fori_loop(nblk_ref[0], N_EXPERT_BLOCKS, tail, 0)

    def row_copies(chunk):
        copies = []
        for j in range(DMA_CHUNK):
            tok = i * ROW_TILE + chunk * DMA_CHUNK + j
            for k in range(TOP_K):
                copies.append(pltpu.make_async_copy(h_hbm.at[_token_rows(tok), :],
                                                    xs_hbm.at[_token_rows(dest_ref[k * N_TOK + tok]), :], sem))
        return copies

    def start(chunk, carry):
        for cp in row_copies(chunk):
            cp.start()
        return carry

    def wait(chunk, carry):
        for cp in row_copies(chunk):
            cp.wait()
        return carry

    lax.fori_loop(0, ROW_TILE // DMA_CHUNK, start, 0)
    lax.fori_loop(0, ROW_TILE // DMA_CHUNK, wait, 0)


def _dispatch(layer, dest, zblk, nblk, h2):
    grid_spec = pltpu.PrefetchScalarGridSpec(
        num_scalar_prefetch=3,
        grid=(N_TILES,),
        in_specs=[pl.BlockSpec(memory_space=pl.ANY)],
        out_specs=pl.BlockSpec(memory_space=pl.ANY),
        scratch_shapes=[pltpu.VMEM((EXPERT_TILE * ROW_SUBLANES, LANES), F32), pltpu.SemaphoreType.DMA(())],
    )
    return pl.pallas_call(
        _dispatch_kernel,
        grid_spec=grid_spec,
        out_shape=jax.ShapeDtypeStruct((M_PAD * ROW_SUBLANES, LANES), F32),
        compiler_params=_params(("arbitrary",)),
        name=f"dispatch_l{layer}",
    )(dest, zblk, nblk, h2)


def _expert_kernel(be_ref, bv_ref, x_ref, wi_ref, bi_ref, wo_ref, bo_ref, y_ref, wi_bf, wo_bf):
    b = pl.program_id(0)

    @pl.when(bv_ref[b] == 0)
    def _():
        y_ref[...] = jnp.zeros_like(y_ref)

    @pl.when(bv_ref[b] == 1)
    def _():
        changed = jnp.logical_or(b == 0, be_ref[b] != be_ref[jnp.maximum(b - 1, 0)])

        @pl.when(changed)
        def _():
            wi_bf[...] = wi_ref[...].astype(BF16)
            wo_bf[...] = wo_ref[...].astype(BF16)

        x = _load_token_rows(x_ref).astype(BF16)
        hdn = jnp.dot(x, wi_bf[...], preferred_element_type=F32) + bi_ref[...]
        glu = jnp.minimum(hdn[:, :D_EXPERT], SWIGLU_LIMIT)
        lin = jnp.clip(hdn[:, D_EXPERT:], -SWIGLU_LIMIT, SWIGLU_LIMIT)
        act = glu * jax.nn.sigmoid(SWIGLU_ALPHA * glu) * (lin + 1.0)
        y = jnp.dot(act.astype(BF16), wo_bf[...], preferred_element_type=F32) + bo_ref[...]
        _store_token_rows(y_ref, y)


def _experts(layer, blk_e, blk_valid, xs, w_exp_in, b_exp_in, w_exp_out, b_exp_out):
    wspec = lambda rows, cols: pl.BlockSpec((None, None, rows, cols), lambda b, be, bv: (layer, be[b], 0, 0))
    blk = pl.BlockSpec((EXPERT_TILE * ROW_SUBLANES, LANES), lambda b, be, bv: (b, 0))
    grid_spec = pltpu.PrefetchScalarGridSpec(
        num_scalar_prefetch=2,
        grid=(N_EXPERT_BLOCKS,),
        in_specs=[blk, wspec(D_MODEL, 2 * D_EXPERT), wspec(1, 2 * D_EXPERT),
                  wspec(D_EXPERT, D_MODEL), wspec(1, D_MODEL)],
        out_specs=blk,
        scratch_shapes=[pltpu.VMEM((D_MODEL, 2 * D_EXPERT), BF16),
                        pltpu.VMEM((D_EXPERT, D_MODEL), BF16)],
    )
    return pl.pallas_call(
        _expert_kernel,
        grid_spec=grid_spec,
        out_shape=jax.ShapeDtypeStruct((M_PAD * ROW_SUBLANES, LANES), F32),
        compiler_params=_params(("arbitrary",)),
        name=f"experts_l{layer}",
    )(blk_e, blk_valid, xs, w_exp_in,
      b_exp_in.reshape(DEPTH, N_EXPERTS, 1, 2 * D_EXPERT), w_exp_out,
      b_exp_out.reshape(DEPTH, N_EXPERTS, 1, D_MODEL))


def _combine_kernel(dest_ref, y_hbm, x_ref, tg_ref, gate_ref, *rest, final):
    if final:
        fg_ref, o_ref, n_ref, buf, sem = rest
    else:
        o_ref, buf, sem = rest
    first = pl.program_id(0) * ROW_TILE

    def row_copies(chunk):
        copies = []
        for j in range(DMA_CHUNK):
            r = chunk * DMA_CHUNK + j
            for k in range(TOP_K):
                copies.append(pltpu.make_async_copy(y_hbm.at[_token_rows(dest_ref[k * N_TOK + first + r]), :],
                                                    buf.at[k, _token_rows(r), :], sem))
        return copies

    def start(chunk, carry):
        for cp in row_copies(chunk):
            cp.start()
        return carry

    def wait(chunk, carry):
        for cp in row_copies(chunk):
            cp.wait()
        return carry

    lax.fori_loop(0, ROW_TILE // DMA_CHUNK, start, 0)
    lax.fori_loop(0, ROW_TILE // DMA_CHUNK, wait, 0)

    tg = tg_ref[...]
    acc = tg[:, 0:1] * _load_token_rows(buf.at[0])
    for k in range(1, TOP_K):
        acc = acc + tg[:, k:k + 1] * _load_token_rows(buf.at[k])
    out = x_ref[...] + gate_ref[...] * acc
    o_ref[...] = out
    if final:
        n_ref[...] = out * lax.rsqrt(jnp.mean(out * out, axis=-1, keepdims=True) + NORM_EPS) * fg_ref[...]


def _combine(layer, dest, yb, x1, tg, mods, final_g=None):
    final = final_g is not None
    row = lambda w: pl.BlockSpec((ROW_TILE, w), lambda i, d: (i, 0))
    in_specs = [pl.BlockSpec(memory_space=pl.ANY), row(D_MODEL), row(TOP_K),
                pl.BlockSpec((None, 1, D_MODEL),
                             lambda i, d: ((layer * N_GROUPS + _tile_group(i)) * 6 + 5, 0, 0))]
    args = [dest, yb, x1, tg, mods]
    out_specs = [row(D_MODEL)]
    out_shape = [jax.ShapeDtypeStruct((N_TOK, D_MODEL), F32)]
    if final:
        in_specs.append(pl.BlockSpec((1, D_MODEL), lambda i, d: (0, 0)))
        args.append(final_g)
        out_specs.append(row(D_MODEL))
        out_shape.append(jax.ShapeDtypeStruct((N_TOK, D_MODEL), F32))
    grid_spec = pltpu.PrefetchScalarGridSpec(
        num_scalar_prefetch=1,
        grid=(N_TILES,),
        in_specs=in_specs,
        out_specs=out_specs,
        scratch_shapes=[pltpu.VMEM((TOP_K, ROW_TILE * ROW_SUBLANES, LANES), F32),
                        pltpu.SemaphoreType.DMA(())],
    )
    return pl.pallas_call(
        functools.partial(_combine_kernel, final=final),
        grid_spec=grid_spec,
        out_shape=out_shape,
        compiler_params=_params(("arbitrary",)),
        name=f"combine_l{layer}",
    )(*args)


def _rope_tables():
    rows = DEC_SEQ // GRID_W
    row = jnp.repeat(jnp.arange(rows, dtype=F32), GRID_W)
    col = jnp.tile(jnp.arange(GRID_W, dtype=F32), rows)
    inv_freq = ROPE_THETA ** (-jnp.arange(0, ROPE_AXIS_DIM, 2, dtype=F32) / ROPE_AXIS_DIM)
    ang = jnp.stack([row[:, None] * inv_freq, col[:, None] * inv_freq], axis=1)
    cos, sin = jnp.cos(ang), jnp.sin(ang)
    cos_h = jnp.concatenate([cos, cos], axis=-1).reshape(DEC_SEQ, HEAD_DIM)
    sin_h = jnp.concatenate([-sin, sin], axis=-1).reshape(DEC_SEQ, HEAD_DIM)
    reps = LANES // HEAD_DIM
    cos_t = jnp.concatenate([jnp.ones((ROW_TILE, LANES), F32), jnp.tile(cos_h, (1, reps))], axis=0)
    sin_t = jnp.concatenate([jnp.zeros((ROW_TILE, LANES), F32), jnp.tile(sin_h, (1, reps))], axis=0)
    return cos_t, sin_t


def _dft_tables(n, scale):
    idx = jnp.arange(n, dtype=I32)
    ang = ((idx[:, None] * idx[None, :]) % n).astype(F32) * (2.0 * math.pi / n)
    return jnp.cos(ang) * scale, jnp.sin(ang) * scale


def _channel_dft():
    c, s = _dft_tables(FOURIER_GROUP_W, 1.0)
    eye = jnp.eye(N_FOURIER_GROUPS, dtype=F32)
    return jnp.concatenate([jnp.kron(eye, c), jnp.kron(eye, s)], axis=1).astype(BF16)


def _extend_heads(t):
    h = [t[..., i * HEAD_DIM:(i + 1) * HEAD_DIM] for i in range(N_KV_HEADS)]
    order = [0, 0, 0, 1, 1, 1, 2, 2, 2, 3, 3, 3]
    return jnp.concatenate([h[i] for i in order], axis=-1)


def _plan(top_e_t, rank_t, counts):
    cnt = counts[:N_EXPERTS, 0]
    padded = (cnt + EXPERT_TILE - 1) // EXPERT_TILE * EXPERT_TILE
    pad_end = jnp.cumsum(padded)
    pad_start = pad_end - padded
    experts = jnp.arange(N_EXPERTS, dtype=I32)
    start_of = jnp.sum(jnp.where(top_e_t[..., None] == experts, pad_start, 0), axis=-1)
    dest = (start_of + rank_t).reshape(-1).astype(I32)
    blk_start = jnp.arange(N_EXPERT_BLOCKS, dtype=I32) * EXPERT_TILE
    blk_e = jnp.minimum(jnp.sum(pad_end[None, :] <= blk_start[:, None], axis=1), N_EXPERTS - 1).astype(I32)
    blk_valid = (blk_start < pad_end[-1]).astype(I32)
    zero_blk = jnp.where(cnt % EXPERT_TILE != 0, (pad_start + cnt) // EXPERT_TILE, -1).astype(I32)
    n_blk = (pad_end[-1:] // EXPERT_TILE).astype(I32)
    return dest, blk_e, blk_valid, zero_blk, n_blk


def kernel(x_prompt, x_sample, cache_k, cache_v, c, c_ctx, w_ada, b_ada, norm1_g, w_in, q_norm_g,
           k_norm_g, w_attn_o, w_fourier_o, w_out, norm2_g, w_router, b_router, w_exp_in, b_exp_in,
           w_exp_out, b_exp_out, final_norm_g):
    x = jnp.concatenate([x_prompt.reshape(N_CTX, D_MODEL), x_sample.reshape(N_LAT, D_MODEL)], axis=0)

    cond = jnp.concatenate([c_ctx[None, :], c, jnp.zeros((8 - N_GROUPS, D_MODEL), F32)], axis=0)
    mods = _mods(cond, w_ada, b_ada)[:, :N_GROUPS].reshape(DEPTH * N_GROUPS * 6, 1, D_MODEL)

    cos_t, sin_t = _rope_tables()
    bd = jnp.kron(jnp.eye(LANES // HEAD_DIM, dtype=F32),
                  jnp.full((HEAD_DIM, HEAD_DIM), 1.0 / HEAD_DIM, F32)).astype(BF16)
    cs = _channel_dft()
    dft_ctx = [t.astype(BF16) for t in _dft_tables(SEQ, (SEQ * FOURIER_GROUP_W) ** -0.5)]
    dft_lat = [t.astype(BF16) for t in _dft_tables(DEC_SEQ, (DEC_SEQ * FOURIER_GROUP_W) ** -0.5)]

    w_in_b = w_in.astype(BF16)
    wa_b = w_attn_o.astype(BF16)
    wf_b = w_fourier_o.astype(BF16)
    wo_b = w_out.astype(BF16)
    norm1 = norm1_g.reshape(DEPTH, 1, D_MODEL)
    norm2 = norm2_g.reshape(DEPTH, 1, D_MODEL)
    qg = jnp.tile(q_norm_g, (1, LANES // HEAD_DIM)).reshape(DEPTH, 1, LANES)
    kg = jnp.tile(k_norm_g, (1, LANES // HEAD_DIM)).reshape(DEPTH, 1, LANES)
    wr_t = jnp.pad(jnp.swapaxes(w_router, 1, 2), ((0, 0), (0, LANES - N_EXPERTS), (0, 0)))
    br_t = jnp.pad(b_router, ((0, 0), (0, LANES - N_EXPERTS)), constant_values=NEG_BIG).reshape(DEPTH, LANES, 1)
    tok = jnp.arange(ROW_TILE, dtype=I32)
    tri = (tok[:, None] <= tok[None, :]).astype(BF16)
    kcache = _extend_heads(cache_k.reshape(DEC_BATCH, DEPTH, PAST_LEN, KV_W)).astype(BF16)
    vcache = _extend_heads(cache_v.reshape(DEC_BATCH, DEPTH, PAST_LEN, KV_W)).astype(BF16)

    keys, vals = [], []
    y_norm = None
    for l in range(DEPTH):
        q, kx, vx, kf, vf, fcs, ga, gf = _inproj(l, x, norm1, mods, w_in_b, qg, kg, cos_t, sin_t, bd, cs)
        keys.append(kf[:N_CTX])
        vals.append(vf[:N_CTX])
        a = _attention_ctx(l, q, kx, vx)
        a = _attention_lat(l, a, kx, vx, kcache[:, l], vcache[:, l])
        x1 = _merge(l, "ctx", a, fcs, dft_ctx[0], dft_ctx[1], ga, gf, x, mods, wa_b, wf_b, wo_b)
        x1 = _merge(l, "lat", a, fcs, dft_lat[0], dft_lat[1], ga, gf, x1, mods, wa_b, wf_b, wo_b)
        h2, top_e, top_g, rank, counts = _router(l, x1, norm2, mods, wr_t, br_t, tri)
        dest, blk_e, blk_valid, zero_blk, n_blk = _plan(top_e, rank, counts)
        xs = _dispatch(l, dest, zero_blk, n_blk, h2)
        yb = _experts(l, blk_e, blk_valid, xs, w_exp_in, b_exp_in, w_exp_out, b_exp_out)
        top_g = top_g.T
        if l == DEPTH - 1:
            x, y_norm = _combine(l, dest, yb, x1, top_g, mods, final_norm_g.reshape(1, D_MODEL))
        else:
            (x,) = _combine(l, dest, yb, x1, top_g, mods)

    y_prompt = y_norm[:N_CTX].reshape(BATCH, SEQ, D_MODEL)
    y_sample = y_norm[N_CTX:].reshape(DEC_BATCH, DEC_SEQ, D_MODEL)
    shape = (BATCH, SEQ, N_KV_HEADS, HEAD_DIM)
    new_k = jnp.stack([k.reshape(shape) for k in keys], axis=1)
    new_v = jnp.stack([v.reshape(shape) for v in vals], axis=1)
    return (y_prompt, y_sample, new_k, new_v)
```

```python
import functools
import math

import jax
import jax.numpy as jnp
from jax import lax
from jax.experimental import pallas as pl
from jax.experimental.pallas import tpu as pltpu

F32 = jnp.float32
BF16 = jnp.bfloat16
I32 = jnp.int32

D_MODEL = 1024
DEPTH = 4
BATCH = 16
SEQ = 256
DEC_BATCH = 2
DEC_SEQ = 2048
PAST_LEN = 512
GRID_W = 64
HEAD_DIM = 64
N_Q_HEADS = 12
N_KV_HEADS = 4
ATTN_W = N_Q_HEADS * HEAD_DIM
KV_W = N_KV_HEADS * HEAD_DIM
FOURIER_W = D_MODEL // 4
N_FOURIER_GROUPS = 4
FOURIER_GROUP_W = FOURIER_W // N_FOURIER_GROUPS
IN_W = ATTN_W + 2 * KV_W + FOURIER_W + 2 * D_MODEL
ROPE_AXIS_DIM = HEAD_DIM // 2
ROPE_THETA = 10000.0
N_EXPERTS = 32
TOP_K = 4
D_EXPERT = D_MODEL
SWIGLU_ALPHA = 1.702
SWIGLU_LIMIT = 7.0
NORM_EPS = 1e-6

N_CTX = BATCH * SEQ
N_LAT = DEC_BATCH * DEC_SEQ
N_TOK = N_CTX + N_LAT
LANES = 128
ROW_TILE = 256
N_TILES = N_TOK // ROW_TILE
CTX_TILES = N_CTX // ROW_TILE
LAT_TILES_PER_BATCH = DEC_SEQ // ROW_TILE
N_GROUPS = 1 + DEC_BATCH
EXPERT_TILE = 256
N_ASSIGN = N_TOK * TOP_K
N_EXPERT_BLOCKS = N_ASSIGN // EXPERT_TILE + N_EXPERTS
M_PAD = N_EXPERT_BLOCKS * EXPERT_TILE
DMA_CHUNK = 32
Q_CHUNKS = ATTN_W // LANES
NEG_BIG = -1e30
VMEM_LIMIT = 56 * 1024 * 1024

_NT = (((1,), (1,)), ((), ()))


def _params(sem, vmem=VMEM_LIMIT):
    return pltpu.CompilerParams(dimension_semantics=sem, vmem_limit_bytes=vmem)


def _tile_group(i):
    return jnp.where(i < CTX_TILES, 0, 1 + (i - CTX_TILES) // LAT_TILES_PER_BATCH)


def _rope_block(i):
    return jnp.where(i < CTX_TILES, 0, 1 + (i - CTX_TILES) % LAT_TILES_PER_BATCH)


def _mod_spec(layer, slot):
    return pl.BlockSpec((None, 1, D_MODEL),
                        lambda i: ((layer * N_GROUPS + _tile_group(i)) * 6 + slot, 0, 0))


def _layer_vec_spec(layer, width=D_MODEL):
    return pl.BlockSpec((None, 1, width), lambda *_: (layer, 0, 0))


ROW_SUBLANES = D_MODEL // LANES


def _store_token_rows(ref, value):
    n = value.shape[0]
    for s in range(ROW_SUBLANES):
        ref[pl.ds(s, n, stride=ROW_SUBLANES), :] = value[:, s * LANES:(s + 1) * LANES]


def _load_token_rows(ref):
    n = ref.shape[0] // ROW_SUBLANES
    return jnp.concatenate([ref[pl.ds(s, n, stride=ROW_SUBLANES), :] for s in range(ROW_SUBLANES)], axis=1)


def _token_rows(token):
    return pl.ds(pl.multiple_of(token * ROW_SUBLANES, ROW_SUBLANES), ROW_SUBLANES)


def _mods_kernel(cond_ref, w_ref, b_ref, o_ref):
    c = cond_ref[...]
    s = c * jax.nn.sigmoid(c)
    o_ref[...] = jnp.dot(s, w_ref[...], preferred_element_type=F32,
                         precision=lax.Precision.HIGHEST) + b_ref[...]


def _mods(cond, w_ada, b_ada):
    tn = 1536
    return pl.pallas_call(
        _mods_kernel,
        grid=(DEPTH, 6 * D_MODEL // tn),
        in_specs=[pl.BlockSpec((8, D_MODEL), lambda l, j: (0, 0)),
                  pl.BlockSpec((None, D_MODEL, tn), lambda l, j: (l, 0, j)),
                  pl.BlockSpec((None, 1, tn), lambda l, j: (l, 0, j))],
        out_specs=pl.BlockSpec((None, 8, tn), lambda l, j: (l, 0, j)),
        out_shape=jax.ShapeDtypeStruct((DEPTH, 8, 6 * D_MODEL), F32),
        compiler_params=_params(("parallel", "parallel")),
        name="adaln_mods",
    )(cond, w_ada, b_ada.reshape(DEPTH, 1, 6 * D_MODEL))


def _inproj_kernel(x_ref, g_ref, sh_ref, sc_ref, w_ref, qg_ref, kg_ref, cos_ref, sin_ref,
                   bd_ref, cs_ref,
                   q_ref, kx_ref, vx_ref, kf_ref, vf_ref, fcs_ref, ga_ref, gf_ref, h_scr):
    x = x_ref[...]
    y = x * lax.rsqrt(jnp.mean(x * x, axis=-1, keepdims=True) + NORM_EPS) * g_ref[...]
    h_scr[...] = (y * (1.0 + sc_ref[...]) + sh_ref[...]).astype(BF16)
    hb = h_scr[...]
    lane = lax.broadcasted_iota(I32, (ROW_TILE, LANES), 1)
    low_half = lane < HEAD_DIM
    rot_first = (lane % ROPE_AXIS_DIM) < (ROPE_AXIS_DIM // 2)
    cos = cos_ref[...]
    sin = sin_ref[...]
    bd = bd_ref[...]

    def proj(lo, width):
        return jnp.dot(hb, w_ref[:, lo:lo + width], preferred_element_type=F32)

    def head_norm_rope(p, gain):
        pp = p * p
        hi = pp.astype(BF16)
        lo = (pp - hi.astype(F32)).astype(BF16)
        msq = (jnp.dot(hi, bd, preferred_element_type=F32)
               + jnp.dot(lo, bd, preferred_element_type=F32))
        n = p * lax.rsqrt(msq + NORM_EPS) * gain
        partner = jnp.where(rot_first, pltpu.roll(n, LANES - ROPE_AXIS_DIM // 2, 1),
                            pltpu.roll(n, ROPE_AXIS_DIM // 2, 1))
        return n * cos + partner * sin

    def head_pairs(r):
        sw = pltpu.roll(r, HEAD_DIM, 1)
        return jnp.where(low_half, r, sw), r, jnp.where(low_half, sw, r)

    scale = HEAD_DIM ** -0.5
    wide = 2 * LANES
    for c2 in range(ATTN_W // wide):
        p = proj(c2 * wide, wide)
        for t in range(2):
            c = 2 * c2 + t
            r = head_norm_rope(p[:, t * LANES:(t + 1) * LANES], qg_ref[...])
            q_ref[:, c * LANES:(c + 1) * LANES] = (r * scale).astype(BF16)

    p = proj(ATTN_W, KV_W)
    for j in range(KV_W // LANES):
        r = head_norm_rope(p[:, j * LANES:(j + 1) * LANES], kg_ref[...])
        kf_ref[:, j * LANES:(j + 1) * LANES] = r
        for t, piece in enumerate(head_pairs(r)):
            c = 3 * j + t
            kx_ref[:, c * LANES:(c + 1) * LANES] = piece.astype(BF16)

    p = proj(ATTN_W + KV_W, KV_W)
    vf_ref[...] = p
    for j in range(KV_W // LANES):
        for t, piece in enumerate(head_pairs(p[:, j * LANES:(j + 1) * LANES])):
            c = 3 * j + t
            vx_ref[:, c * LANES:(c + 1) * LANES] = piece.astype(BF16)

    f = proj(ATTN_W + 2 * KV_W, FOURIER_W)
    fcs_ref[...] = jnp.dot(f.astype(BF16), cs_ref[...], preferred_element_type=F32).astype(BF16)

    gate_lo = ATTN_W + 2 * KV_W + FOURIER_W
    half = D_MODEL // 2
    for j in range(2):
        ga_ref[:, j * half:(j + 1) * half] = jax.nn.sigmoid(proj(gate_lo + j * half, half)).astype(BF16)
        gf_ref[:, j * half:(j + 1) * half] = jax.nn.sigmoid(
            proj(gate_lo + D_MODEL + j * half, half)).astype(BF16)


def _inproj(layer, x, norm_g, mods, w_in, qg, kg, cos_tab, sin_tab, bd, cs):
    row = lambda w: pl.BlockSpec((ROW_TILE, w), lambda i: (i, 0))
    const = lambda a: pl.BlockSpec(a.shape, lambda i: (0,) * a.ndim)
    ext_w = Q_CHUNKS * LANES
    return pl.pallas_call(
        _inproj_kernel,
        grid=(N_TILES,),
        in_specs=[row(D_MODEL), _layer_vec_spec(layer), _mod_spec(layer, 0), _mod_spec(layer, 1),
                  pl.BlockSpec((None, D_MODEL, IN_W), lambda i: (layer, 0, 0)),
                  _layer_vec_spec(layer, LANES), _layer_vec_spec(layer, LANES),
                  pl.BlockSpec((ROW_TILE, LANES), lambda i: (_rope_block(i), 0)),
                  pl.BlockSpec((ROW_TILE, LANES), lambda i: (_rope_block(i), 0)),
                  const(bd), const(cs)],
        out_specs=[row(ATTN_W), row(ext_w), row(ext_w), row(KV_W), row(KV_W),
                   row(2 * FOURIER_W), row(D_MODEL), row(D_MODEL)],
        out_shape=[jax.ShapeDtypeStruct((N_TOK, ATTN_W), BF16),
                   jax.ShapeDtypeStruct((N_TOK, ext_w), BF16),
                   jax.ShapeDtypeStruct((N_TOK, ext_w), BF16),
                   jax.ShapeDtypeStruct((N_TOK, KV_W), F32),
                   jax.ShapeDtypeStruct((N_TOK, KV_W), F32),
                   jax.ShapeDtypeStruct((N_TOK, 2 * FOURIER_W), BF16),
                   jax.ShapeDtypeStruct((N_TOK, D_MODEL), BF16),
                   jax.ShapeDtypeStruct((N_TOK, D_MODEL), BF16)],
        scratch_shapes=[pltpu.VMEM((ROW_TILE, D_MODEL), BF16)],
        compiler_params=_params(("parallel",)),
        name=f"inproj_l{layer}",
    )(x, norm_g, mods, mods, w_in, qg, kg, cos_tab, sin_tab, bd, cs)


def _attn_kernel(*refs, n_parts):
    q_ref = refs[0]
    k_refs = refs[1:1 + n_parts]
    v_refs = refs[1 + n_parts:1 + 2 * n_parts]
    o_ref = refs[-1]
    tq = q_ref.shape[0]
    low_half = lax.broadcasted_iota(I32, (tq, LANES), 1) < HEAD_DIM
    for c in range(Q_CHUNKS):
        cols = slice(c * LANES, (c + 1) * LANES)
        qc = q_ref[:, cols]
        zero = jnp.zeros_like(qc)
        outs = []
        for mask in (low_half, jnp.logical_not(low_half)):
            qm = jnp.where(mask, qc, zero)
            ss = [lax.dot_general(qm, k[:, cols], _NT, preferred_element_type=F32) for k in k_refs]
            m = ss[0].max(axis=-1, keepdims=True)
            for s in ss[1:]:
                m = jnp.maximum(m, s.max(axis=-1, keepdims=True))
            ps = [jnp.exp(s - m) for s in ss]
            den = ps[0].sum(axis=-1, keepdims=True)
            for p in ps[1:]:
                den = den + p.sum(axis=-1, keepdims=True)
            acc = jnp.dot(ps[0].astype(BF16), v_refs[0][:, cols], preferred_element_type=F32)
            for p, v in zip(ps[1:], v_refs[1:]):
                acc = acc + jnp.dot(p.astype(BF16), v[:, cols], preferred_element_type=F32)
            outs.append(acc / den)
        o_ref[:, cols] = jnp.where(low_half, outs[0], outs[1]).astype(BF16)


def _attention_ctx(layer, q, kx, vx):
    w = Q_CHUNKS * LANES
    blk = lambda width: pl.BlockSpec((SEQ, width), lambda b: (b, 0))
    return pl.pallas_call(
        functools.partial(_attn_kernel, n_parts=1),
        grid=(BATCH,),
        in_specs=[blk(ATTN_W), blk(w), blk(w)],
        out_specs=blk(ATTN_W),
        out_shape=jax.ShapeDtypeStruct((N_TOK, ATTN_W), BF16),
        input_output_aliases={0: 0},
        compiler_params=_params(("parallel",)),
        name=f"attn_ctx_l{layer}",
    )(q, kx, vx)


def _attention_lat(layer, q, kx, vx, kcache, vcache):
    w = Q_CHUNKS * LANES
    nq = DEC_SEQ // ROW_TILE
    first = N_CTX // ROW_TILE
    qblk = pl.BlockSpec((ROW_TILE, ATTN_W), lambda b, i: (first + b * nq + i, 0))
    new = pl.BlockSpec((DEC_SEQ, w), lambda b, i: (N_CTX // DEC_SEQ + b, 0))
    old = pl.BlockSpec((None, PAST_LEN, w), lambda b, i: (b, 0, 0))
    return pl.pallas_call(
        functools.partial(_attn_kernel, n_parts=2),
        grid=(DEC_BATCH, nq),
        in_specs=[qblk, old, new, old, new],
        out_specs=qblk,
        out_shape=jax.ShapeDtypeStruct((N_TOK, ATTN_W), BF16),
        input_output_aliases={0: 0},
        compiler_params=_params(("parallel", "parallel")),
        name=f"attn_lat_l{layer}",
    )(q, kcache, kx, vcache, vx)


def _merge_kernel(a_ref, fc_ref, fs_ref, cl_ref, sl_ref, ga_ref, gf_ref, x_ref, gate_ref,
                  wa_ref, wf_ref, wo_ref, o_ref):
    attn = jnp.dot(a_ref[...], wa_ref[...], preferred_element_type=F32)
    fo = (jnp.dot(cl_ref[...], fc_ref[...], preferred_element_type=F32)
          - jnp.dot(sl_ref[...], fs_ref[...], preferred_element_type=F32))
    four = jnp.dot(fo.astype(BF16), wf_ref[...], preferred_element_type=F32)
    merged = ga_ref[...].astype(F32) * attn + gf_ref[...].astype(F32) * four
    mix = jnp.dot(merged.astype(BF16), wo_ref[...], preferred_element_type=F32)
    o_ref[...] = x_ref[...] + gate_ref[...] * mix


def _merge(layer, stream, a, fcs, cl, sl, ga, gf, x, mods, wa, wf, wo):
    if stream == "ctx":
        seq, n_batch, first_tile, first_seq = SEQ, BATCH, 0, 0
    else:
        seq, n_batch, first_tile, first_seq = DEC_SEQ, DEC_BATCH, CTX_TILES, N_CTX // DEC_SEQ
    nt = seq // ROW_TILE
    tile = lambda b, i: first_tile + b * nt + i
    row = lambda w: pl.BlockSpec((ROW_TILE, w), lambda b, i: (tile(b, i), 0))
    wspec = lambda a_: pl.BlockSpec((None,) + a_.shape[1:], lambda b, i: (layer, 0, 0))
    in_specs = [row(ATTN_W),
                pl.BlockSpec((seq, FOURIER_W), lambda b, i: (first_seq + b, 0)),
                pl.BlockSpec((seq, FOURIER_W), lambda b, i: (first_seq + b, 1)),
                pl.BlockSpec((ROW_TILE, seq), lambda b, i: (i, 0)),
                pl.BlockSpec((ROW_TILE, seq), lambda b, i: (i, 0)),
                row(D_MODEL), row(D_MODEL), row(D_MODEL),
                pl.BlockSpec((None, 1, D_MODEL),
                             lambda b, i: ((layer * N_GROUPS + _tile_group(tile(b, i))) * 6 + 2, 0, 0)),
                wspec(wa), wspec(wf), wspec(wo)]
    args = [a, fcs, fcs, cl, sl, ga, gf, x, mods, wa, wf, wo]
    aliases = {7: 0}
    return pl.pallas_call(
        _merge_kernel,
        grid=(n_batch, nt),
        in_specs=in_specs,
        out_specs=row(D_MODEL),
        out_shape=jax.ShapeDtypeStruct((N_TOK, D_MODEL), F32),
        input_output_aliases=aliases,
        compiler_params=_params(("parallel", "parallel")),
        name=f"merge_{stream}_l{layer}",
    )(*args)


def _router_kernel(x_ref, g_ref, sh_ref, sc_ref, wr_ref, br_ref, tri_ref,
                   h_ref, te_ref, tg_ref, rk_ref, cnt_ref, carry):
    @pl.when(pl.program_id(0) == 0)
    def _():
        carry[...] = jnp.zeros_like(carry)

    x = x_ref[...]
    y = x * lax.rsqrt(jnp.mean(x * x, axis=-1, keepdims=True) + NORM_EPS) * g_ref[...]
    h = y * (1.0 + sc_ref[...]) + sh_ref[...]
    _store_token_rows(h_ref, h)
    logits = lax.dot_general(wr_ref[...], h, _NT, preferred_element_type=F32,
                             precision=lax.Precision.HIGHEST) + br_ref[...]
    sub = lax.broadcasted_iota(I32, logits.shape, 0).astype(F32)
    vals, ids = [], []
    for _ in range(TOP_K):
        m = logits.max(axis=0, keepdims=True)
        idx = jnp.where(logits == m, sub, float(LANES)).min(axis=0, keepdims=True)
        vals.append(m)
        ids.append(idx)
        logits = jnp.where(sub == idx, NEG_BIG, logits)
    es = [jnp.exp(v - vals[0]) for v in vals]
    den = es[0] + es[1] + es[2] + es[3]

    chosen = [sub == idx for idx in ids]
    member = jnp.zeros_like(sub)
    for ch in chosen:
        member = member + ch.astype(F32)
    upto = jnp.dot(member.astype(BF16), tri_ref[...], preferred_element_type=F32)
    before = carry[...] + upto - member
    for k in range(TOP_K):
        te_ref[k:k + 1, :] = ids[k].astype(I32)
        tg_ref[k:k + 1, :] = es[k] / den
        rk_ref[k:k + 1, :] = jnp.where(chosen[k], before, 0.0).sum(axis=0, keepdims=True).astype(I32)
    carry[...] = carry[...] + member.sum(axis=1, keepdims=True)
    cnt_ref[...] = carry[:, :LANES].astype(I32)


def _router(layer, x1, norm_g, mods, wr_t, br_t, tri):
    row = lambda w: pl.BlockSpec((ROW_TILE, w), lambda i: (i, 0))
    col = pl.BlockSpec((TOP_K, ROW_TILE), lambda i: (0, i))
    return pl.pallas_call(
        _router_kernel,
        grid=(N_TILES,),
        in_specs=[row(D_MODEL), _layer_vec_spec(layer), _mod_spec(layer, 3), _mod_spec(layer, 4),
                  pl.BlockSpec((None, LANES, D_MODEL), lambda i: (layer, 0, 0)),
                  pl.BlockSpec((None, LANES, 1), lambda i: (layer, 0, 0)),
                  pl.BlockSpec((ROW_TILE, ROW_TILE), lambda i: (0, 0))],
        out_specs=[pl.BlockSpec((ROW_TILE * ROW_SUBLANES, LANES), lambda i: (i, 0)), col, col, col,
                   pl.BlockSpec((LANES, LANES), lambda i: (0, 0))],
        out_shape=[jax.ShapeDtypeStruct((N_TOK * ROW_SUBLANES, LANES), F32),
                   jax.ShapeDtypeStruct((TOP_K, N_TOK), I32),
                   jax.ShapeDtypeStruct((TOP_K, N_TOK), F32),
                   jax.ShapeDtypeStruct((TOP_K, N_TOK), I32),
                   jax.ShapeDtypeStruct((LANES, LANES), I32)],
        scratch_shapes=[pltpu.VMEM((LANES, ROW_TILE), F32)],
        compiler_params=_params(("arbitrary",)),
        name=f"router_l{layer}",
    )(x1, norm_g, mods, mods, wr_t, br_t, tri)


def _dispatch_kernel(dest_ref, zblk_ref, nblk_ref, h_ref, xs_hbm, zeros, sem):
    i = pl.program_id(0)
    block_rows = EXPERT_TILE * ROW_SUBLANES

    def zero_fill(block):
        rows = pl.ds(pl.multiple_of(block * block_rows, block_rows), block_rows)
        return pltpu.make_async_copy(zeros, xs_hbm.at[rows, :], sem)

    @pl.when(i == 0)
    def _():
        zeros[...] = jnp.zeros_like(zeros)
        for wait in (False, True):
            for e in range(N_EXPERTS):
                @pl.when(zblk_ref[e] >= 0)
                def _():
                    cp = zero_fill(zblk_ref[e])
                    cp.wait() if wait else cp.start()

            def tail(j, carry):
                cp = zero_fill(j)
                cp.wait() if wait else cp.start()
                return carry

            lax.fori_loop(nblk_ref[0], N_EXPERT_BLOCKS, tail, 0)

    def row_copies(chunk):
        copies = []
        for j in range(DMA_CHUNK):
            r = chunk * DMA_CHUNK + j
            for k in range(TOP_K):
                d = dest_ref[k * N_TOK + i * ROW_TILE + r]
                copies.append(pltpu.make_async_copy(h_ref.at[_token_rows(r), :],
                                                    xs_hbm.at[_token_rows(d), :], sem))
        return copies

    def start(chunk, carry):
        for n, cp in enumerate(row_copies(chunk)):
            cp.start(priority=n % 2)
        return carry

    def wait(chunk, carry):
        for cp in row_copies(chunk):
            cp.wait()
        return carry

    lax.fori_loop(0, ROW_TILE // DMA_CHUNK, start, 0)
    lax.fori_loop(0, ROW_TILE // DMA_CHUNK, wait, 0)


def _dispatch(layer, dest, zblk, nblk, h2):
    grid_spec = pltpu.PrefetchScalarGridSpec(
        num_scalar_prefetch=3,
        grid=(N_TILES,),
        in_specs=[pl.BlockSpec((ROW_TILE * ROW_SUBLANES, LANES), lambda i, d, z, n: (i, 0))],
        out_specs=pl.BlockSpec(memory_space=pl.ANY),
        scratch_shapes=[pltpu.VMEM((EXPERT_TILE * ROW_SUBLANES, LANES), F32), pltpu.SemaphoreType.DMA(())],
    )
    return pl.pallas_call(
        _dispatch_kernel,
        grid_spec=grid_spec,
        out_shape=jax.ShapeDtypeStruct((M_PAD * ROW_SUBLANES, LANES), F32),
        compiler_params=_params(("arbitrary",)),
        name=f"dispatch_l{layer}",
    )(dest, zblk, nblk, h2)


def _expert_kernel(be_ref, kind_ref, slot_ref, next_ref, x_ref, wi_hbm, bi_ref, wo_hbm, bo_ref, y_ref,
                   wi_f32, wo_f32, wi_bf, wo_bf, sems, *, layer):
    b = pl.program_id(0)

    def weight_copies(expert, slot):
        return (pltpu.make_async_copy(wi_hbm.at[layer, expert], wi_f32.at[slot], sems.at[0, slot]),
                pltpu.make_async_copy(wo_hbm.at[layer, expert], wo_f32.at[slot], sems.at[1, slot]))

    @pl.when(kind_ref[b] == 0)
    def _():
        y_ref[...] = jnp.zeros_like(y_ref)

    @pl.when(kind_ref[b] == 2)
    def _():
        slot = slot_ref[b]
        own = weight_copies(be_ref[b], slot)

        @pl.when(b == 0)
        def _():
            for cp in own:
                cp.start()

        for cp in own:
            cp.wait()

        @pl.when(next_ref[b] >= 0)
        def _():
            for cp in weight_copies(next_ref[b], 1 - slot):
                cp.start()

        wi_bf[...] = wi_f32[slot].astype(BF16)
        wo_bf[...] = wo_f32[slot].astype(BF16)

    @pl.when(kind_ref[b] > 0)
    def _():
        x = _load_token_rows(x_ref).astype(BF16)
        hdn = jnp.dot(x, wi_bf[...], preferred_element_type=F32) + bi_ref[...]
        glu = jnp.minimum(hdn[:, :D_EXPERT], SWIGLU_LIMIT)
        lin = jnp.clip(hdn[:, D_EXPERT:], -SWIGLU_LIMIT, SWIGLU_LIMIT)
        act = glu * jax.nn.sigmoid(SWIGLU_ALPHA * glu) * (lin + 1.0)
        y = jnp.dot(act.astype(BF16), wo_bf[...], preferred_element_type=F32) + bo_ref[...]
        _store_token_rows(y_ref, y)


def _experts(layer, blk_e, blk_kind, blk_slot, blk_next, xs, w_exp_in, b_exp_in, w_exp_out, b_exp_out):
    bias = lambda cols: pl.BlockSpec((None, None, 1, cols), lambda b, be, *_: (layer, be[b], 0, 0))
    blk = pl.BlockSpec((EXPERT_TILE * ROW_SUBLANES, LANES), lambda b, *_: (b, 0))
    hbm = pl.BlockSpec(memory_space=pl.ANY)
    grid_spec = pltpu.PrefetchScalarGridSpec(
        num_scalar_prefetch=4,
        grid=(N_EXPERT_BLOCKS,),
        in_specs=[blk, hbm, bias(2 * D_EXPERT), hbm, bias(D_MODEL)],
        out_specs=blk,
        scratch_shapes=[pltpu.VMEM((2, D_MODEL, 2 * D_EXPERT), F32),
                        pltpu.VMEM((2, D_EXPERT, D_MODEL), F32),
                        pltpu.VMEM((D_MODEL, 2 * D_EXPERT), BF16),
                        pltpu.VMEM((D_EXPERT, D_MODEL), BF16),
                        pltpu.SemaphoreType.DMA((2, 2))],
    )
    return pl.pallas_call(
        functools.partial(_expert_kernel, layer=layer),
        grid_spec=grid_spec,
        out_shape=jax.ShapeDtypeStruct((M_PAD * ROW_SUBLANES, LANES), F32),
        compiler_params=_params(("arbitrary",)),
        name=f"experts_l{layer}",
    )(blk_e, blk_kind, blk_slot, blk_next, xs, w_exp_in,
      b_exp_in.reshape(DEPTH, N_EXPERTS, 1, 2 * D_EXPERT), w_exp_out,
      b_exp_out.reshape(DEPTH, N_EXPERTS, 1, D_MODEL))


def _combine_kernel(dest_ref, y_hbm, x_ref, tg_ref, gate_ref, *rest, final):
    if final:
        fg_ref, o_ref, n_ref, buf, sem = rest
    else:
        o_ref, buf, sem = rest
    first = pl.program_id(0) * ROW_TILE

    def row_copies(chunk):
        copies = []
        for j in range(DMA_CHUNK):
            r = chunk * DMA_CHUNK + j
            for k in range(TOP_K):
                copies.append(pltpu.make_async_copy(y_hbm.at[_token_rows(dest_ref[k * N_TOK + first + r]), :],
                                                    buf.at[k, _token_rows(r), :], sem))
        return copies

    def start(chunk, carry):
        for n, cp in enumerate(row_copies(chunk)):
            cp.start(priority=n % 2)
        return carry

    def wait(chunk, carry):
        for cp in row_copies(chunk):
            cp.wait()
        return carry

    lax.fori_loop(0, ROW_TILE // DMA_CHUNK, start, 0)
    lax.fori_loop(0, ROW_TILE // DMA_CHUNK, wait, 0)

    tg = tg_ref[...]
    acc = tg[:, 0:1] * _load_token_rows(buf.at[0])
    for k in range(1, TOP_K):
        acc = acc + tg[:, k:k + 1] * _load_token_rows(buf.at[k])
    out = x_ref[...] + gate_ref[...] * acc
    o_ref[...] = out
    if final:
        n_ref[...] = out * lax.rsqrt(jnp.mean(out * out, axis=-1, keepdims=True) + NORM_EPS) * fg_ref[...]


def _combine(layer, dest, yb, x1, tg, mods, final_g=None):
    final = final_g is not None
    row = lambda w: pl.BlockSpec((ROW_TILE, w), lambda i, d: (i, 0))
    in_specs = [pl.BlockSpec(memory_space=pl.ANY), row(D_MODEL), row(TOP_K),
                pl.BlockSpec((None, 1, D_MODEL),
                             lambda i, d: ((layer * N_GROUPS + _tile_group(i)) * 6 + 5, 0, 0))]
    args = [dest, yb, x1, tg, mods]
    out_specs = [row(D_MODEL)]
    out_shape = [jax.ShapeDtypeStruct((N_TOK, D_MODEL), F32)]
    if final:
        in_specs.append(pl.BlockSpec((1, D_MODEL), lambda i, d: (0, 0)))
        args.append(final_g)
        out_specs.append(row(D_MODEL))
        out_shape.append(jax.ShapeDtypeStruct((N_TOK, D_MODEL), F32))
    grid_spec = pltpu.PrefetchScalarGridSpec(
        num_scalar_prefetch=1,
        grid=(N_TILES,),
        in_specs=in_specs,
        out_specs=out_specs,
        scratch_shapes=[pltpu.VMEM((TOP_K, ROW_TILE * ROW_SUBLANES, LANES), F32),
                        pltpu.SemaphoreType.DMA(())],
    )
    return pl.pallas_call(
        functools.partial(_combine_kernel, final=final),
        grid_spec=grid_spec,
        out_shape=out_shape,
        compiler_params=_params(("arbitrary",)),
        name=f"combine_l{layer}",
    )(*args)


def _rope_tables():
    rows = DEC_SEQ // GRID_W
    row = jnp.repeat(jnp.arange(rows, dtype=F32), GRID_W)
    col = jnp.tile(jnp.arange(GRID_W, dtype=F32), rows)
    inv_freq = ROPE_THETA ** (-jnp.arange(0, ROPE_AXIS_DIM, 2, dtype=F32) / ROPE_AXIS_DIM)
    ang = jnp.stack([row[:, None] * inv_freq, col[:, None] * inv_freq], axis=1)
    cos, sin = jnp.cos(ang), jnp.sin(ang)
    cos_h = jnp.concatenate([cos, cos], axis=-1).reshape(DEC_SEQ, HEAD_DIM)
    sin_h = jnp.concatenate([-sin, sin], axis=-1).reshape(DEC_SEQ, HEAD_DIM)
    reps = LANES // HEAD_DIM
    cos_t = jnp.concatenate([jnp.ones((ROW_TILE, LANES), F32), jnp.tile(cos_h, (1, reps))], axis=0)
    sin_t = jnp.concatenate([jnp.zeros((ROW_TILE, LANES), F32), jnp.tile(sin_h, (1, reps))], axis=0)
    return cos_t, sin_t


def _dft_tables(n, scale):
    idx = jnp.arange(n, dtype=I32)
    ang = ((idx[:, None] * idx[None, :]) % n).astype(F32) * (2.0 * math.pi / n)
    return jnp.cos(ang) * scale, jnp.sin(ang) * scale


def _channel_dft():
    c, s = _dft_tables(FOURIER_GROUP_W, 1.0)
    eye = jnp.eye(N_FOURIER_GROUPS, dtype=F32)
    return jnp.concatenate([jnp.kron(eye, c), jnp.kron(eye, s)], axis=1).astype(BF16)


def _extend_heads(t):
    h = [t[..., i * HEAD_DIM:(i + 1) * HEAD_DIM] for i in range(N_KV_HEADS)]
    order = [0, 0, 0, 1, 1, 1, 2, 2, 2, 3, 3, 3]
    return jnp.concatenate([h[i] for i in order], axis=-1)


def _plan(top_e_t, rank_t, counts):
    cnt = counts[:N_EXPERTS, 0]
    padded = (cnt + EXPERT_TILE - 1) // EXPERT_TILE * EXPERT_TILE
    pad_end = jnp.cumsum(padded)
    pad_start = pad_end - padded
    experts = jnp.arange(N_EXPERTS, dtype=I32)
    start_of = jnp.sum(jnp.where(top_e_t[..., None] == experts, pad_start, 0), axis=-1)
    dest = (start_of + rank_t).reshape(-1).astype(I32)
    blk_start = jnp.arange(N_EXPERT_BLOCKS, dtype=I32) * EXPERT_TILE
    blk_e = jnp.minimum(jnp.sum(pad_end[None, :] <= blk_start[:, None], axis=1), N_EXPERTS - 1).astype(I32)
    valid = blk_start < pad_end[-1]
    first = jnp.logical_and(valid, jnp.concatenate([jnp.ones((1,), bool), blk_e[1:] != blk_e[:-1]]))
    blk_kind = (valid.astype(I32) + first.astype(I32)).astype(I32)
    blk_slot = ((jnp.cumsum(first.astype(I32)) - 1) % 2).astype(I32)
    later = jnp.logical_and(experts[None, :] > experts[:, None], cnt[None, :] > 0)
    next_of = jnp.min(jnp.where(later, experts[None, :], N_EXPERTS), axis=1)
    next_of = jnp.where(next_of == N_EXPERTS, -1, next_of)
    blk_next = jnp.sum(jnp.where(blk_e[:, None] == experts[None, :], next_of[None, :], 0), axis=1).astype(I32)
    zero_blk = jnp.where(cnt % EXPERT_TILE != 0, (pad_start + cnt) // EXPERT_TILE, -1).astype(I32)
    n_blk = (pad_end[-1:] // EXPERT_TILE).astype(I32)
    return dest, (blk_e, blk_kind, blk_slot, blk_next), zero_blk, n_blk


def kernel(x_prompt, x_sample, cache_k, cache_v, c, c_ctx, w_ada, b_ada, norm1_g, w_in, q_norm_g,
           k_norm_g, w_attn_o, w_fourier_o, w_out, norm2_g, w_router, b_router, w_exp_in, b_exp_in,
           w_exp_out, b_exp_out, final_norm_g):
    x = jnp.concatenate([x_prompt.reshape(N_CTX, D_MODEL), x_sample.reshape(N_LAT, D_MODEL)], axis=0)

    cond = jnp.concatenate([c_ctx[None, :], c, jnp.zeros((8 - N_GROUPS, D_MODEL), F32)], axis=0)
    mods = _mods(cond, w_ada, b_ada)[:, :N_GROUPS].reshape(DEPTH * N_GROUPS * 6, 1, D_MODEL)

    cos_t, sin_t = _rope_tables()
    bd = jnp.kron(jnp.eye(LANES // HEAD_DIM, dtype=F32),
                  jnp.full((HEAD_DIM, HEAD_DIM), 1.0 / HEAD_DIM, F32)).astype(BF16)
    cs = _channel_dft()
    dft_ctx = [t.astype(BF16) for t in _dft_tables(SEQ, (SEQ * FOURIER_GROUP_W) ** -0.5)]
    dft_lat = [t.astype(BF16) for t in _dft_tables(DEC_SEQ, (DEC_SEQ * FOURIER_GROUP_W) ** -0.5)]

    w_in_b = w_in.astype(BF16)
    wa_b = w_attn_o.astype(BF16)
    wf_b = w_fourier_o.astype(BF16)
    wo_b = w_out.astype(BF16)
    norm1 = norm1_g.reshape(DEPTH, 1, D_MODEL)
    norm2 = norm2_g.reshape(DEPTH, 1, D_MODEL)
    qg = jnp.tile(q_norm_g, (1, LANES // HEAD_DIM)).reshape(DEPTH, 1, LANES)
    kg = jnp.tile(k_norm_g, (1, LANES // HEAD_DIM)).reshape(DEPTH, 1, LANES)
    wr_t = jnp.pad(jnp.swapaxes(w_router, 1, 2), ((0, 0), (0, LANES - N_EXPERTS), (0, 0)))
    br_t = jnp.pad(b_router, ((0, 0), (0, LANES - N_EXPERTS)), constant_values=NEG_BIG).reshape(DEPTH, LANES, 1)
    tok = jnp.arange(ROW_TILE, dtype=I32)
    tri = (tok[:, None] <= tok[None, :]).astype(BF16)
    kcache = _extend_heads(cache_k.reshape(DEC_BATCH, DEPTH, PAST_LEN, KV_W)).astype(BF16)
    vcache = _extend_heads(cache_v.reshape(DEC_BATCH, DEPTH, PAST_LEN, KV_W)).astype(BF16)

    keys, vals = [], []
    y_norm = None
    for l in range(DEPTH):
        q, kx, vx, kf, vf, fcs, ga, gf = _inproj(l, x, norm1, mods, w_in_b, qg, kg, cos_t, sin_t, bd, cs)
        keys.append(kf[:N_CTX])
        vals.append(vf[:N_CTX])
        a = _attention_ctx(l, q, kx, vx)
        a = _attention_lat(l, a, kx, vx, kcache[:, l], vcache[:, l])
        x1 = _merge(l, "ctx", a, fcs, dft_ctx[0], dft_ctx[1], ga, gf, x, mods, wa_b, wf_b, wo_b)
        x1 = _merge(l, "lat", a, fcs, dft_lat[0], dft_lat[1], ga, gf, x1, mods, wa_b, wf_b, wo_b)
        h2, top_e, top_g, rank, counts = _router(l, x1, norm2, mods, wr_t, br_t, tri)
        dest, blk_tables, zero_blk, n_blk = _plan(top_e, rank, counts)
        xs = _dispatch(l, dest, zero_blk, n_blk, h2)
        yb = _experts(l, *blk_tables, xs, w_exp_in, b_exp_in, w_exp_out, b_exp_out)
        top_g = top_g.T
        if l == DEPTH - 1:
            x, y_norm = _combine(l, dest, yb, x1, top_g, mods, final_norm_g.reshape(1, D_MODEL))
        else:
            (x,) = _combine(l, dest, yb, x1, top_g, mods)

    y_prompt = y_norm[:N_CTX].reshape(BATCH, SEQ, D_MODEL)
    y_sample = y_norm[N_CTX:].reshape(DEC_BATCH, DEC_SEQ, D_MODEL)
    shape = (BATCH, SEQ, N_KV_HEADS, HEAD_DIM)
    new_k = jnp.stack([k.reshape(shape) for k in keys], axis=1)
    new_v = jnp.stack([v.reshape(shape) for v in vals], axis=1)
    return (y_prompt, y_sample, new_k, new_v)
```

```python
import functools
import math

import jax
import jax.numpy as jnp
from jax import lax
from jax.experimental import pallas as pl
from jax.experimental.pallas import tpu as pltpu

F32 = jnp.float32
BF16 = jnp.bfloat16
I32 = jnp.int32

D_MODEL = 1024
DEPTH = 4
BATCH = 16
SEQ = 256
DEC_BATCH = 2
DEC_SEQ = 2048
PAST_LEN = 512
GRID_W = 64
HEAD_DIM = 64
N_Q_HEADS = 12
N_KV_HEADS = 4
ATTN_W = N_Q_HEADS * HEAD_DIM
KV_W = N_KV_HEADS * HEAD_DIM
FOURIER_W = D_MODEL // 4
N_FOURIER_GROUPS = 4
FOURIER_GROUP_W = FOURIER_W // N_FOURIER_GROUPS
IN_W = ATTN_W + 2 * KV_W + FOURIER_W + 2 * D_MODEL
ROPE_AXIS_DIM = HEAD_DIM // 2
ROPE_THETA = 10000.0
N_EXPERTS = 32
TOP_K = 4
D_EXPERT = D_MODEL
SWIGLU_ALPHA = 1.702
SWIGLU_LIMIT = 7.0
NORM_EPS = 1e-6

N_CTX = BATCH * SEQ
N_LAT = DEC_BATCH * DEC_SEQ
N_TOK = N_CTX + N_LAT
LANES = 128
ROW_TILE = 256
IN_TILE = 512
LAT_Q_TILE = 256
N_TILES = N_TOK // ROW_TILE
CTX_TILES = N_CTX // ROW_TILE
LAT_TILES_PER_BATCH = DEC_SEQ // ROW_TILE
N_GROUPS = 1 + DEC_BATCH
EXPERT_TILE = 256
N_ASSIGN = N_TOK * TOP_K
N_EXPERT_BLOCKS = N_ASSIGN // EXPERT_TILE + N_EXPERTS
M_PAD = N_EXPERT_BLOCKS * EXPERT_TILE
DMA_CHUNK = 32
DFT_ROW_BLOCK = 64
Q_CHUNKS = ATTN_W // LANES
NEG_BIG = -1e30
VMEM_LIMIT = 56 * 1024 * 1024

_NT = (((1,), (1,)), ((), ()))


def _params(sem, vmem=VMEM_LIMIT):
    return pltpu.CompilerParams(dimension_semantics=sem, vmem_limit_bytes=vmem)


def _tile_group(i, tile=ROW_TILE):
    ctx_tiles = N_CTX // tile
    return jnp.where(i < ctx_tiles, 0, 1 + (i - ctx_tiles) // (DEC_SEQ // tile))


def _rope_block(i, tile):
    ctx_tiles = N_CTX // tile
    return jnp.where(i < ctx_tiles, 0, 1 + (i - ctx_tiles) % (DEC_SEQ // tile))


def _mod_spec(layer, slot, tile=ROW_TILE):
    return pl.BlockSpec((None, 1, D_MODEL),
                        lambda i: ((layer * N_GROUPS + _tile_group(i, tile)) * 6 + slot, 0, 0))


def _layer_vec_spec(layer, width=D_MODEL):
    return pl.BlockSpec((None, 1, width), lambda *_: (layer, 0, 0))


ROW_SUBLANES = D_MODEL // LANES


def _store_token_rows(ref, value):
    n = value.shape[0]
    for s in range(ROW_SUBLANES):
        ref[pl.ds(s, n, stride=ROW_SUBLANES), :] = value[:, s * LANES:(s + 1) * LANES]


def _load_token_rows(ref):
    n = ref.shape[0] // ROW_SUBLANES
    return jnp.concatenate([ref[pl.ds(s, n, stride=ROW_SUBLANES), :] for s in range(ROW_SUBLANES)], axis=1)


def _token_rows(token):
    return pl.ds(pl.multiple_of(token * ROW_SUBLANES, ROW_SUBLANES), ROW_SUBLANES)


def _mods_kernel(cond_ref, w_ref, b_ref, o_ref):
    c = cond_ref[...]
    s = c * jax.nn.sigmoid(c)
    o_ref[...] = jnp.dot(s, w_ref[...], preferred_element_type=F32,
                         precision=lax.Precision.HIGHEST) + b_ref[...]


def _mods(cond, w_ada, b_ada):
    tn = 1536
    return pl.pallas_call(
        _mods_kernel,
        grid=(DEPTH, 6 * D_MODEL // tn),
        in_specs=[pl.BlockSpec((8, D_MODEL), lambda l, j: (0, 0)),
                  pl.BlockSpec((None, D_MODEL, tn), lambda l, j: (l, 0, j)),
                  pl.BlockSpec((None, 1, tn), lambda l, j: (l, 0, j))],
        out_specs=pl.BlockSpec((None, 8, tn), lambda l, j: (l, 0, j)),
        out_shape=jax.ShapeDtypeStruct((DEPTH, 8, 6 * D_MODEL), F32),
        compiler_params=_params(("parallel", "parallel")),
        name="adaln_mods",
    )(cond, w_ada, b_ada.reshape(DEPTH, 1, 6 * D_MODEL))


def _inproj_kernel(x_ref, g_ref, sh_ref, sc_ref, w_ref, qg_ref, kg_ref, cos_ref, sin_ref,
                   bd_ref, cs_ref,
                   q_ref, kx_ref, vx_ref, kf_ref, vf_ref, fcs_ref, ga_ref, gf_ref, h_scr):
    x = x_ref[...]
    y = x * lax.rsqrt(jnp.mean(x * x, axis=-1, keepdims=True) + NORM_EPS) * g_ref[...]
    h_scr[...] = (y * (1.0 + sc_ref[...]) + sh_ref[...]).astype(BF16)
    hb = h_scr[...]
    lane = lax.broadcasted_iota(I32, (x_ref.shape[0], LANES), 1)
    low_half = lane < HEAD_DIM
    rot_first = (lane % ROPE_AXIS_DIM) < (ROPE_AXIS_DIM // 2)
    cos = cos_ref[...]
    sin = sin_ref[...]
    bd = bd_ref[...]

    def proj(lo, width):
        return jnp.dot(hb, w_ref[:, lo:lo + width], preferred_element_type=F32)

    def head_norm_rope(p, gain):
        pp = p * p
        hi = pp.astype(BF16)
        lo = (pp - hi.astype(F32)).astype(BF16)
        msq = (jnp.dot(hi, bd, preferred_element_type=F32)
               + jnp.dot(lo, bd, preferred_element_type=F32))
        n = p * lax.rsqrt(msq + NORM_EPS) * gain
        partner = jnp.where(rot_first, pltpu.roll(n, LANES - ROPE_AXIS_DIM // 2, 1),
                            pltpu.roll(n, ROPE_AXIS_DIM // 2, 1))
        return n * cos + partner * sin

    def head_pairs(r):
        sw = pltpu.roll(r, HEAD_DIM, 1)
        return jnp.where(low_half, r, sw), r, jnp.where(low_half, sw, r)

    scale = HEAD_DIM ** -0.5
    wide = 2 * LANES
    for c2 in range(ATTN_W // wide):
        p = proj(c2 * wide, wide)
        for t in range(2):
            c = 2 * c2 + t
            r = head_norm_rope(p[:, t * LANES:(t + 1) * LANES], qg_ref[...])
            q_ref[:, c * LANES:(c + 1) * LANES] = (r * scale).astype(BF16)

    p = proj(ATTN_W, KV_W)
    for j in range(KV_W // LANES):
        r = head_norm_rope(p[:, j * LANES:(j + 1) * LANES], kg_ref[...])
        kf_ref[:, j * LANES:(j + 1) * LANES] = r
        for t, piece in enumerate(head_pairs(r)):
            c = 3 * j + t
            kx_ref[:, c * LANES:(c + 1) * LANES] = piece.astype(BF16)

    p = proj(ATTN_W + KV_W, KV_W)
    vf_ref[...] = p
    for j in range(KV_W // LANES):
        for t, piece in enumerate(head_pairs(p[:, j * LANES:(j + 1) * LANES])):
            c = 3 * j + t
            vx_ref[:, c * LANES:(c + 1) * LANES] = piece.astype(BF16)

    f = proj(ATTN_W + 2 * KV_W, FOURIER_W)
    fcs_ref[...] = jnp.dot(f.astype(BF16), cs_ref[...], preferred_element_type=F32).astype(BF16)

    gate_lo = ATTN_W + 2 * KV_W + FOURIER_W
    half = D_MODEL // 2
    for j in range(2):
        ga_ref[:, j * half:(j + 1) * half] = jax.nn.sigmoid(proj(gate_lo + j * half, half)).astype(BF16)
        gf_ref[:, j * half:(j + 1) * half] = jax.nn.sigmoid(
            proj(gate_lo + D_MODEL + j * half, half)).astype(BF16)


def _inproj(layer, x, norm_g, mods, w_in, qg, kg, cos_tab, sin_tab, bd, cs):
    row = lambda w: pl.BlockSpec((IN_TILE, w), lambda i: (i, 0))
    const = lambda a: pl.BlockSpec(a.shape, lambda i: (0,) * a.ndim)
    rope = pl.BlockSpec((IN_TILE, LANES), lambda i: (_rope_block(i, IN_TILE), 0))
    ext_w = Q_CHUNKS * LANES
    return pl.pallas_call(
        _inproj_kernel,
        grid=(N_TOK // IN_TILE,),
        in_specs=[row(D_MODEL), _layer_vec_spec(layer), _mod_spec(layer, 0, IN_TILE), _mod_spec(layer, 1, IN_TILE),
                  pl.BlockSpec((None, D_MODEL, IN_W), lambda i: (layer, 0, 0)),
                  _layer_vec_spec(layer, LANES), _layer_vec_spec(layer, LANES), rope, rope,
                  const(bd), const(cs)],
        out_specs=[row(ATTN_W), row(ext_w), row(ext_w), row(KV_W), row(KV_W),
                   row(2 * FOURIER_W), row(D_MODEL), row(D_MODEL)],
        out_shape=[jax.ShapeDtypeStruct((N_TOK, ATTN_W), BF16),
                   jax.ShapeDtypeStruct((N_TOK, ext_w), BF16),
                   jax.ShapeDtypeStruct((N_TOK, ext_w), BF16),
                   jax.ShapeDtypeStruct((N_TOK, KV_W), F32),
                   jax.ShapeDtypeStruct((N_TOK, KV_W), F32),
                   jax.ShapeDtypeStruct((N_TOK, 2 * FOURIER_W), BF16),
                   jax.ShapeDtypeStruct((N_TOK, D_MODEL), BF16),
                   jax.ShapeDtypeStruct((N_TOK, D_MODEL), BF16)],
        scratch_shapes=[pltpu.VMEM((IN_TILE, D_MODEL), BF16)],
        compiler_params=_params(("parallel",)),
        name=f"inproj_l{layer}",
    )(x, norm_g, mods, mods, w_in, qg, kg, cos_tab, sin_tab, bd, cs)


def _attn_kernel(*refs, n_parts):
    q_ref = refs[0]
    k_refs = refs[1:1 + n_parts]
    v_refs = refs[1 + n_parts:1 + 2 * n_parts]
    o_ref = refs[-1]
    tq = q_ref.shape[0]
    low_half = lax.broadcasted_iota(I32, (tq, LANES), 1) < HEAD_DIM
    for c in range(Q_CHUNKS):
        cols = slice(c * LANES, (c + 1) * LANES)
        qc = q_ref[:, cols]
        zero = jnp.zeros_like(qc)
        outs = []
        for mask in (low_half, jnp.logical_not(low_half)):
            qm = jnp.where(mask, qc, zero)
            ss = [lax.dot_general(qm, k[:, cols], _NT, preferred_element_type=F32) for k in k_refs]
            m = ss[0].max(axis=-1, keepdims=True)
            for s in ss[1:]:
                m = jnp.maximum(m, s.max(axis=-1, keepdims=True))
            ps = [jnp.exp(s - m) for s in ss]
            den = ps[0].sum(axis=-1, keepdims=True)
            for p in ps[1:]:
                den = den + p.sum(axis=-1, keepdims=True)
            acc = jnp.dot(ps[0].astype(BF16), v_refs[0][:, cols], preferred_element_type=F32)
            for p, v in zip(ps[1:], v_refs[1:]):
                acc = acc + jnp.dot(p.astype(BF16), v[:, cols], preferred_element_type=F32)
            outs.append(acc / den)
        o_ref[:, cols] = jnp.where(low_half, outs[0], outs[1]).astype(BF16)


def _attention_ctx(layer, q, kx, vx):
    w = Q_CHUNKS * LANES
    blk = lambda width: pl.BlockSpec((SEQ, width), lambda b: (b, 0))
    return pl.pallas_call(
        functools.partial(_attn_kernel, n_parts=1),
        grid=(BATCH,),
        in_specs=[blk(ATTN_W), blk(w), blk(w)],
        out_specs=blk(ATTN_W),
        out_shape=jax.ShapeDtypeStruct((N_TOK, ATTN_W), BF16),
        input_output_aliases={0: 0},
        compiler_params=_params(("parallel",)),
        name=f"attn_ctx_l{layer}",
    )(q, kx, vx)


def _attention_lat(layer, q, kx, vx, kcache, vcache):
    w = Q_CHUNKS * LANES
    nq = DEC_SEQ // LAT_Q_TILE
    first = N_CTX // LAT_Q_TILE
    qblk = pl.BlockSpec((LAT_Q_TILE, ATTN_W), lambda b, i: (first + b * nq + i, 0))
    new = pl.BlockSpec((DEC_SEQ, w), lambda b, i: (N_CTX // DEC_SEQ + b, 0))
    old = pl.BlockSpec((None, PAST_LEN, w), lambda b, i: (b, 0, 0))
    return pl.pallas_call(
        functools.partial(_attn_kernel, n_parts=2),
        grid=(DEC_BATCH, nq),
        in_specs=[qblk, old, new, old, new],
        out_specs=qblk,
        out_shape=jax.ShapeDtypeStruct((N_TOK, ATTN_W), BF16),
        input_output_aliases={0: 0},
        compiler_params=_params(("parallel", "parallel")),
        name=f"attn_lat_l{layer}",
    )(q, kcache, kx, vcache, vx)


def _merge_kernel(a_ref, fc_ref, fs_ref, cl_ref, sl_ref, ga_ref, gf_ref, x_ref, gate_ref,
                  wa_ref, wf_ref, wo_ref, o_ref):
    attn = jnp.dot(a_ref[...], wa_ref[...], preferred_element_type=F32)
    fo = (jnp.dot(cl_ref[...], fc_ref[...], preferred_element_type=F32)
          - jnp.dot(sl_ref[...], fs_ref[...], preferred_element_type=F32))
    four = jnp.dot(fo.astype(BF16), wf_ref[...], preferred_element_type=F32)
    merged = ga_ref[...].astype(F32) * attn + gf_ref[...].astype(F32) * four
    mix = jnp.dot(merged.astype(BF16), wo_ref[...], preferred_element_type=F32)
    o_ref[...] = x_ref[...] + gate_ref[...] * mix


def _merge(layer, stream, a, fcs, cl, sl, ga, gf, x, mods, wa, wf, wo):
    if stream == "ctx":
        seq, n_batch, first_tile, first_seq = SEQ, BATCH, 0, 0
    else:
        seq, n_batch, first_tile, first_seq = DEC_SEQ, DEC_BATCH, CTX_TILES, N_CTX // DEC_SEQ
    nt = seq // ROW_TILE
    tile = lambda b, i: first_tile + b * nt + i
    row = lambda w: pl.BlockSpec((ROW_TILE, w), lambda b, i: (tile(b, i), 0))
    wspec = lambda a_: pl.BlockSpec((None,) + a_.shape[1:], lambda b, i: (layer, 0, 0))
    in_specs = [row(ATTN_W),
                pl.BlockSpec((seq, FOURIER_W), lambda b, i: (first_seq + b, 0)),
                pl.BlockSpec((seq, FOURIER_W), lambda b, i: (first_seq + b, 1)),
                pl.BlockSpec((ROW_TILE, seq), lambda b, i: (i, 0)),
                pl.BlockSpec((ROW_TILE, seq), lambda b, i: (i, 0)),
                row(D_MODEL), row(D_MODEL), row(D_MODEL),
                pl.BlockSpec((None, 1, D_MODEL),
                             lambda b, i: ((layer * N_GROUPS + _tile_group(tile(b, i))) * 6 + 2, 0, 0)),
                wspec(wa), wspec(wf), wspec(wo)]
    args = [a, fcs, fcs, cl, sl, ga, gf, x, mods, wa, wf, wo]
    aliases = {7: 0}
    return pl.pallas_call(
        _merge_kernel,
        grid=(n_batch, nt),
        in_specs=in_specs,
        out_specs=row(D_MODEL),
        out_shape=jax.ShapeDtypeStruct((N_TOK, D_MODEL), F32),
        input_output_aliases=aliases,
        compiler_params=_params(("parallel", "parallel")),
        name=f"merge_{stream}_l{layer}",
    )(*args)


def _router_kernel(x_ref, g_ref, sh_ref, sc_ref, wr_ref, br_ref, tri_ref,
                   h_ref, te_ref, tg_ref, rk_ref, cnt_ref, carry):
    @pl.when(pl.program_id(0) == 0)
    def _():
        carry[...] = jnp.zeros_like(carry)

    x = x_ref[...]
    y = x * lax.rsqrt(jnp.mean(x * x, axis=-1, keepdims=True) + NORM_EPS) * g_ref[...]
    h = y * (1.0 + sc_ref[...]) + sh_ref[...]
    _store_token_rows(h_ref, h)
    logits = lax.dot_general(wr_ref[...], h, _NT, preferred_element_type=F32,
                             precision=lax.Precision.HIGHEST) + br_ref[...]
    sub = lax.broadcasted_iota(I32, logits.shape, 0).astype(F32)
    vals, ids = [], []
    for _ in range(TOP_K):
        m = logits.max(axis=0, keepdims=True)
        idx = jnp.where(logits == m, sub, float(N_EXPERTS)).min(axis=0, keepdims=True)
        vals.append(m)
        ids.append(idx)
        logits = jnp.where(sub == idx, NEG_BIG, logits)
    es = [jnp.exp(v - vals[0]) for v in vals]
    den = es[0] + es[1] + es[2] + es[3]

    chosen = [sub == idx for idx in ids]
    member = jnp.zeros_like(sub)
    for ch in chosen:
        member = member + ch.astype(F32)
    upto = jnp.dot(member.astype(BF16), tri_ref[...], preferred_element_type=F32)
    before = carry[...] + upto - member
    for k in range(TOP_K):
        te_ref[k:k + 1, :] = ids[k].astype(I32)
        tg_ref[k:k + 1, :] = es[k] / den
        rk_ref[k:k + 1, :] = jnp.where(chosen[k], before, 0.0).sum(axis=0, keepdims=True).astype(I32)
    carry[...] = carry[...] + member.sum(axis=1, keepdims=True)
    cnt_ref[...] = carry[:, :LANES].astype(I32)


def _router(layer, x1, norm_g, mods, wr_t, br_t, tri):
    row = lambda w: pl.BlockSpec((ROW_TILE, w), lambda i: (i, 0))
    col = pl.BlockSpec((TOP_K, ROW_TILE), lambda i: (0, i))
    return pl.pallas_call(
        _router_kernel,
        grid=(N_TILES,),
        in_specs=[row(D_MODEL), _layer_vec_spec(layer), _mod_spec(layer, 3), _mod_spec(layer, 4),
                  pl.BlockSpec((None, N_EXPERTS, D_MODEL), lambda i: (layer, 0, 0)),
                  pl.BlockSpec((None, N_EXPERTS, 1), lambda i: (layer, 0, 0)),
                  pl.BlockSpec((ROW_TILE, ROW_TILE), lambda i: (0, 0))],
        out_specs=[pl.BlockSpec((ROW_TILE * ROW_SUBLANES, LANES), lambda i: (i, 0)), col, col, col,
                   pl.BlockSpec((N_EXPERTS, LANES), lambda i: (0, 0))],
        out_shape=[jax.ShapeDtypeStruct((N_TOK * ROW_SUBLANES, LANES), F32),
                   jax.ShapeDtypeStruct((TOP_K, N_TOK), I32),
                   jax.ShapeDtypeStruct((TOP_K, N_TOK), F32),
                   jax.ShapeDtypeStruct((TOP_K, N_TOK), I32),
                   jax.ShapeDtypeStruct((N_EXPERTS, LANES), I32)],
        scratch_shapes=[pltpu.VMEM((N_EXPERTS, ROW_TILE), F32)],
        compiler_params=_params(("arbitrary",)),
        name=f"router_l{layer}",
    )(x1, norm_g, mods, mods, wr_t, br_t, tri)


def _dispatch_kernel(dest_ref, zblk_ref, nblk_ref, h_ref, xs_hbm, zeros, sem):
    i = pl.program_id(0)
    block_rows = EXPERT_TILE * ROW_SUBLANES

    def zero_fill(block):
        rows = pl.ds(pl.multiple_of(block * block_rows, block_rows), block_rows)
        return pltpu.make_async_copy(zeros, xs_hbm.at[rows, :], sem)

    @pl.when(i == 0)
    def _():
        zeros[...] = jnp.zeros_like(zeros)
        for wait in (False, True):
            for e in range(N_EXPERTS):
                @pl.when(zblk_ref[e] >= 0)
                def _():
                    cp = zero_fill(zblk_ref[e])
                    cp.wait() if wait else cp.start()

            def tail(j, carry):
                cp = zero_fill(j)
                cp.wait() if wait else cp.start()
                return carry

            lax.fori_loop(nblk_ref[0], N_EXPERT_BLOCKS, tail, 0)

    def row_copies(chunk):
        copies = []
        for j in range(DMA_CHUNK):
            r = chunk * DMA_CHUNK + j
            for k in range(TOP_K):
                d = dest_ref[k * N_TOK + i * ROW_TILE + r]
                copies.append(pltpu.make_async_copy(h_ref.at[_token_rows(r), :],
                                                    xs_hbm.at[_token_rows(d), :], sem))
        return copies

    def start(chunk, carry):
        for n, cp in enumerate(row_copies(chunk)):
            cp.start(priority=n % 2)
        return carry

    def wait(chunk, carry):
        for cp in row_copies(chunk):
            cp.wait()
        return carry

    lax.fori_loop(0, ROW_TILE // DMA_CHUNK, start, 0)
    lax.fori_loop(0, ROW_TILE // DMA_CHUNK, wait, 0)


def _dispatch(layer, dest, zblk, nblk, h2):
    grid_spec = pltpu.PrefetchScalarGridSpec(
        num_scalar_prefetch=3,
        grid=(N_TILES,),
        in_specs=[pl.BlockSpec((ROW_TILE * ROW_SUBLANES, LANES), lambda i, d, z, n: (i, 0))],
        out_specs=pl.BlockSpec(memory_space=pl.ANY),
        scratch_shapes=[pltpu.VMEM((EXPERT_TILE * ROW_SUBLANES, LANES), F32), pltpu.SemaphoreType.DMA(())],
    )
    return pl.pallas_call(
        _dispatch_kernel,
        grid_spec=grid_spec,
        out_shape=jax.ShapeDtypeStruct((M_PAD * ROW_SUBLANES, LANES), F32),
        compiler_params=_params(("arbitrary",)),
        name=f"dispatch_l{layer}",
    )(dest, zblk, nblk, h2)


def _expert_kernel(be_ref, kind_ref, slot_ref, next_ref, x_ref, wi_hbm, bi_ref, wo_hbm, bo_ref, y_ref,
                   wi_f32, wo_f32, wi_bf, wo_bf, sems, *, layer):
    b = pl.program_id(0)

    def weight_copies(expert, slot):
        return (pltpu.make_async_copy(wi_hbm.at[layer, expert], wi_f32.at[slot], sems.at[0, slot]),
                pltpu.make_async_copy(wo_hbm.at[layer, expert], wo_f32.at[slot], sems.at[1, slot]))

    @pl.when(kind_ref[b] == 0)
    def _():
        y_ref[...] = jnp.zeros_like(y_ref)

    @pl.when(kind_ref[b] == 2)
    def _():
        slot = slot_ref[b]
        own = weight_copies(be_ref[b], slot)

        @pl.when(b == 0)
        def _():
            for cp in own:
                cp.start()

        for cp in own:
            cp.wait()

        @pl.when(next_ref[b] >= 0)
        def _():
            for cp in weight_copies(next_ref[b], 1 - slot):
                cp.start(priority=1)

        wi_bf[...] = wi_f32[slot].astype(BF16)
        wo_bf[...] = wo_f32[slot].astype(BF16)

    @pl.when(kind_ref[b] > 0)
    def _():
        x = _load_token_rows(x_ref).astype(BF16)
        hdn = jnp.dot(x, wi_bf[...], preferred_element_type=F32) + bi_ref[...]
        glu = jnp.minimum(hdn[:, :D_EXPERT], SWIGLU_LIMIT)
        lin = jnp.clip(hdn[:, D_EXPERT:], -SWIGLU_LIMIT, SWIGLU_LIMIT)
        act = glu * jax.nn.sigmoid(SWIGLU_ALPHA * glu) * (lin + 1.0)
        y = jnp.dot(act.astype(BF16), wo_bf[...], preferred_element_type=F32) + bo_ref[...]
        _store_token_rows(y_ref, y)


def _experts(layer, blk_e, blk_kind, blk_slot, blk_next, xs, w_exp_in, b_exp_in, w_exp_out, b_exp_out):
    bias = lambda cols: pl.BlockSpec((None, None, 1, cols), lambda b, be, *_: (layer, be[b], 0, 0))
    blk = pl.BlockSpec((EXPERT_TILE * ROW_SUBLANES, LANES), lambda b, *_: (b, 0))
    hbm = pl.BlockSpec(memory_space=pl.ANY)
    grid_spec = pltpu.PrefetchScalarGridSpec(
        num_scalar_prefetch=4,
        grid=(N_EXPERT_BLOCKS,),
        in_specs=[blk, hbm, bias(2 * D_EXPERT), hbm, bias(D_MODEL)],
        out_specs=blk,
        scratch_shapes=[pltpu.VMEM((2, D_MODEL, 2 * D_EXPERT), F32),
                        pltpu.VMEM((2, D_EXPERT, D_MODEL), F32),
                        pltpu.VMEM((D_MODEL, 2 * D_EXPERT), BF16),
                        pltpu.VMEM((D_EXPERT, D_MODEL), BF16),
                        pltpu.SemaphoreType.DMA((2, 2))],
    )
    return pl.pallas_call(
        functools.partial(_expert_kernel, layer=layer),
        grid_spec=grid_spec,
        out_shape=jax.ShapeDtypeStruct((M_PAD * ROW_SUBLANES, LANES), F32),
        compiler_params=_params(("arbitrary",)),
        name=f"experts_l{layer}",
    )(blk_e, blk_kind, blk_slot, blk_next, xs, w_exp_in,
      b_exp_in.reshape(DEPTH, N_EXPERTS, 1, 2 * D_EXPERT), w_exp_out,
      b_exp_out.reshape(DEPTH, N_EXPERTS, 1, D_MODEL))


def _combine_kernel(dest_ref, y_hbm, x_ref, tg_ref, gate_ref, *rest, final):
    if final:
        fg_ref, o_ref, n_ref, buf, sem = rest
    else:
        o_ref, buf, sem = rest
    first = pl.program_id(0) * ROW_TILE

    def row_copies(chunk):
        copies = []
        for j in range(DMA_CHUNK):
            r = chunk * DMA_CHUNK + j
            for k in range(TOP_K):
                copies.append(pltpu.make_async_copy(y_hbm.at[_token_rows(dest_ref[k * N_TOK + first + r]), :],
                                                    buf.at[k, _token_rows(r), :], sem))
        return copies

    def start(chunk, carry):
        for n, cp in enumerate(row_copies(chunk)):
            cp.start(priority=n % 2)
        return carry

    def wait(chunk, carry):
        for cp in row_copies(chunk):
            cp.wait()
        return carry

    lax.fori_loop(0, ROW_TILE // DMA_CHUNK, start, 0)
    lax.fori_loop(0, ROW_TILE // DMA_CHUNK, wait, 0)

    tg = tg_ref[...]
    acc = tg[:, 0:1] * _load_token_rows(buf.at[0])
    for k in range(1, TOP_K):
        acc = acc + tg[:, k:k + 1] * _load_token_rows(buf.at[k])
    out = x_ref[...] + gate_ref[...] * acc
    o_ref[...] = out
    if final:
        n_ref[...] = out * lax.rsqrt(jnp.mean(out * out, axis=-1, keepdims=True) + NORM_EPS) * fg_ref[...]


def _combine(layer, dest, yb, x1, tg, mods, final_g=None):
    final = final_g is not None
    row = lambda w: pl.BlockSpec((ROW_TILE, w), lambda i, d: (i, 0))
    in_specs = [pl.BlockSpec(memory_space=pl.ANY), row(D_MODEL), row(TOP_K),
                pl.BlockSpec((None, 1, D_MODEL),
                             lambda i, d: ((layer * N_GROUPS + _tile_group(i)) * 6 + 5, 0, 0))]
    args = [dest, yb, x1, tg, mods]
    out_specs = [row(D_MODEL)]
    out_shape = [jax.ShapeDtypeStruct((N_TOK, D_MODEL), F32)]
    if final:
        in_specs.append(pl.BlockSpec((1, D_MODEL), lambda i, d: (0, 0)))
        args.append(final_g)
        out_specs.append(row(D_MODEL))
        out_shape.append(jax.ShapeDtypeStruct((N_TOK, D_MODEL), F32))
    grid_spec = pltpu.PrefetchScalarGridSpec(
        num_scalar_prefetch=1,
        grid=(N_TILES,),
        in_specs=in_specs,
        out_specs=out_specs,
        scratch_shapes=[pltpu.VMEM((TOP_K, ROW_TILE * ROW_SUBLANES, LANES), F32),
                        pltpu.SemaphoreType.DMA(())],
    )
    return pl.pallas_call(
        functools.partial(_combine_kernel, final=final),
        grid_spec=grid_spec,
        out_shape=out_shape,
        compiler_params=_params(("arbitrary",)),
        name=f"combine_l{layer}",
    )(*args)


def _rope_tables():
    rows = DEC_SEQ // GRID_W
    row = jnp.repeat(jnp.arange(rows, dtype=F32), GRID_W)
    col = jnp.tile(jnp.arange(GRID_W, dtype=F32), rows)
    inv_freq = ROPE_THETA ** (-jnp.arange(0, ROPE_AXIS_DIM, 2, dtype=F32) / ROPE_AXIS_DIM)
    ang = jnp.stack([row[:, None] * inv_freq, col[:, None] * inv_freq], axis=1)
    cos, sin = jnp.cos(ang), jnp.sin(ang)
    cos_h = jnp.concatenate([cos, cos], axis=-1).reshape(DEC_SEQ, HEAD_DIM)
    sin_h = jnp.concatenate([-sin, sin], axis=-1).reshape(DEC_SEQ, HEAD_DIM)
    reps = LANES // HEAD_DIM
    cos_t = jnp.concatenate([jnp.ones((IN_TILE, LANES), F32), jnp.tile(cos_h, (1, reps))], axis=0)
    sin_t = jnp.concatenate([jnp.zeros((IN_TILE, LANES), F32), jnp.tile(sin_h, (1, reps))], axis=0)
    return cos_t, sin_t


def _dft_tables(n, scale):
    col = jnp.arange(n, dtype=I32)

    def direct(rows):
        ang = ((rows[:, None] * col[None, :]) % n).astype(F32) * (2.0 * math.pi / n)
        return jnp.cos(ang), jnp.sin(ang)

    if n <= DFT_ROW_BLOCK:
        c, s = direct(col)
        return c * scale, s * scale
    lo_c, lo_s = direct(jnp.arange(DFT_ROW_BLOCK, dtype=I32))
    hi_c, hi_s = direct(jnp.arange(n // DFT_ROW_BLOCK, dtype=I32) * DFT_ROW_BLOCK)
    hi_c, hi_s = (hi_c * scale)[:, None, :], (hi_s * scale)[:, None, :]
    c = hi_c * lo_c[None] - hi_s * lo_s[None]
    s = hi_s * lo_c[None] + hi_c * lo_s[None]
    return c.reshape(n, n), s.reshape(n, n)


def _channel_dft():
    c, s = _dft_tables(FOURIER_GROUP_W, 1.0)
    eye = jnp.eye(N_FOURIER_GROUPS, dtype=F32)
    return jnp.concatenate([jnp.kron(eye, c), jnp.kron(eye, s)], axis=1).astype(BF16)


def _extend_heads(t):
    h = [t[..., i * HEAD_DIM:(i + 1) * HEAD_DIM] for i in range(N_KV_HEADS)]
    order = [0, 0, 0, 1, 1, 1, 2, 2, 2, 3, 3, 3]
    return jnp.concatenate([h[i] for i in order], axis=-1)


def _plan(top_e_t, rank_t, counts):
    cnt = counts[:N_EXPERTS, 0]
    padded = (cnt + EXPERT_TILE - 1) // EXPERT_TILE * EXPERT_TILE
    pad_end = jnp.cumsum(padded)
    pad_start = pad_end - padded
    experts = jnp.arange(N_EXPERTS, dtype=I32)
    start_of = jnp.sum(jnp.where(top_e_t[..., None] == experts, pad_start, 0), axis=-1)
    dest = (start_of + rank_t).reshape(-1).astype(I32)
    blk_start = jnp.arange(N_EXPERT_BLOCKS, dtype=I32) * EXPERT_TILE
    blk_e = jnp.minimum(jnp.sum(pad_end[None, :] <= blk_start[:, None], axis=1), N_EXPERTS - 1).astype(I32)
    valid = blk_start < pad_end[-1]
    first = jnp.logical_and(valid, jnp.concatenate([jnp.ones((1,), bool), blk_e[1:] != blk_e[:-1]]))
    blk_kind = (valid.astype(I32) + first.astype(I32)).astype(I32)
    blk_slot = ((jnp.cumsum(first.astype(I32)) - 1) % 2).astype(I32)
    later = jnp.logical_and(experts[None, :] > experts[:, None], cnt[None, :] > 0)
    next_of = jnp.min(jnp.where(later, experts[None, :], N_EXPERTS), axis=1)
    next_of = jnp.where(next_of == N_EXPERTS, -1, next_of)
    blk_next = jnp.sum(jnp.where(blk_e[:, None] == experts[None, :], next_of[None, :], 0), axis=1).astype(I32)
    zero_blk = jnp.where(cnt % EXPERT_TILE != 0, (pad_start + cnt) // EXPERT_TILE, -1).astype(I32)
    n_blk = (pad_end[-1:] // EXPERT_TILE).astype(I32)
    return dest, (blk_e, blk_kind, blk_slot, blk_next), zero_blk, n_blk


def kernel(x_prompt, x_sample, cache_k, cache_v, c, c_ctx, w_ada, b_ada, norm1_g, w_in, q_norm_g,
           k_norm_g, w_attn_o, w_fourier_o, w_out, norm2_g, w_router, b_router, w_exp_in, b_exp_in,
           w_exp_out, b_exp_out, final_norm_g):
    x = jnp.concatenate([x_prompt.reshape(N_CTX, D_MODEL), x_sample.reshape(N_LAT, D_MODEL)], axis=0)

    cond = jnp.concatenate([c_ctx[None, :], c, jnp.zeros((8 - N_GROUPS, D_MODEL), F32)], axis=0)
    mods = _mods(cond, w_ada, b_ada)[:, :N_GROUPS].reshape(DEPTH * N_GROUPS * 6, 1, D_MODEL)

    cos_t, sin_t = _rope_tables()
    bd = jnp.kron(jnp.eye(LANES // HEAD_DIM, dtype=F32),
                  jnp.full((HEAD_DIM, HEAD_DIM), 1.0 / HEAD_DIM, F32)).astype(BF16)
    cs = _channel_dft()
    dft_ctx = [t.astype(BF16) for t in _dft_tables(SEQ, (SEQ * FOURIER_GROUP_W) ** -0.5)]
    dft_lat = [t.astype(BF16) for t in _dft_tables(DEC_SEQ, (DEC_SEQ * FOURIER_GROUP_W) ** -0.5)]

    w_in_b = w_in.astype(BF16)
    wa_b = w_attn_o.astype(BF16)
    wf_b = w_fourier_o.astype(BF16)
    wo_b = w_out.astype(BF16)
    norm1 = norm1_g.reshape(DEPTH, 1, D_MODEL)
    norm2 = norm2_g.reshape(DEPTH, 1, D_MODEL)
    qg = jnp.tile(q_norm_g, (1, LANES // HEAD_DIM)).reshape(DEPTH, 1, LANES)
    kg = jnp.tile(k_norm_g, (1, LANES // HEAD_DIM)).reshape(DEPTH, 1, LANES)
    wr_t = jnp.swapaxes(w_router, 1, 2)
    br_t = b_router.reshape(DEPTH, N_EXPERTS, 1)
    tok = jnp.arange(ROW_TILE, dtype=I32)
    tri = (tok[:, None] <= tok[None, :]).astype(BF16)
    kcache = _extend_heads(cache_k.reshape(DEC_BATCH, DEPTH, PAST_LEN, KV_W)).astype(BF16)
    vcache = _extend_heads(cache_v.reshape(DEC_BATCH, DEPTH, PAST_LEN, KV_W)).astype(BF16)

    keys, vals = [], []
    y_norm = None
    for l in range(DEPTH):
        q, kx, vx, kf, vf, fcs, ga, gf = _inproj(l, x, norm1, mods, w_in_b, qg, kg, cos_t, sin_t, bd, cs)
        keys.append(kf[:N_CTX])
        vals.append(vf[:N_CTX])
        a = _attention_ctx(l, q, kx, vx)
        a = _attention_lat(l, a, kx, vx, kcache[:, l], vcache[:, l])
        x1 = _merge(l, "ctx", a, fcs, dft_ctx[0], dft_ctx[1], ga, gf, x, mods, wa_b, wf_b, wo_b)
        x1 = _merge(l, "lat", a, fcs, dft_lat[0], dft_lat[1], ga, gf, x1, mods, wa_b, wf_b, wo_b)
        h2, top_e, top_g, rank, counts = _router(l, x1, norm2, mods, wr_t, br_t, tri)
        dest, blk_tables, zero_blk, n_blk = _plan(top_e, rank, counts)
        xs = _dispatch(l, dest, zero_blk, n_blk, h2)
        yb = _experts(l, *blk_tables, xs, w_exp_in, b_exp_in, w_exp_out, b_exp_out)
        top_g = top_g.T
        if l == DEPTH - 1:
            x, y_norm = _combine(l, dest, yb, x1, top_g, mods, final_norm_g.reshape(1, D_MODEL))
        else:
            (x,) = _combine(l, dest, yb, x1, top_g, mods)

    y_prompt = y_norm[:N_CTX].reshape(BATCH, SEQ, D_MODEL)
    y_sample = y_norm[N_CTX:].reshape(DEC_BATCH, DEC_SEQ, D_MODEL)
    shape = (BATCH, SEQ, N_KV_HEADS, HEAD_DIM)
    new_k = jnp.stack([k.reshape(shape) for k in keys], axis=1)
    new_v = jnp.stack([v.reshape(shape) for v in vals], axis=1)
    return (y_prompt, y_sample, new_k, new_v)
```

```python
import functools
import math

import jax
import jax.numpy as jnp
from jax import lax
from jax.experimental import pallas as pl
from jax.experimental.pallas import tpu as pltpu

F32 = jnp.float32
BF16 = jnp.bfloat16
I32 = jnp.int32

D_MODEL = 1024
DEPTH = 4
BATCH = 16
SEQ = 256
DEC_BATCH = 2
DEC_SEQ = 2048
PAST_LEN = 512
GRID_W = 64
HEAD_DIM = 64
N_Q_HEADS = 12
N_KV_HEADS = 4
ATTN_W = N_Q_HEADS * HEAD_DIM
KV_W = N_KV_HEADS * HEAD_DIM
FOURIER_W = D_MODEL // 4
N_FOURIER_GROUPS = 4
FOURIER_GROUP_W = FOURIER_W // N_FOURIER_GROUPS
IN_W = ATTN_W + 2 * KV_W + FOURIER_W + 2 * D_MODEL
ROPE_AXIS_DIM = HEAD_DIM // 2
ROPE_THETA = 10000.0
N_EXPERTS = 32
TOP_K = 4
D_EXPERT = D_MODEL
SWIGLU_ALPHA = 1.702
SWIGLU_LIMIT = 7.0
NORM_EPS = 1e-6

N_CTX = BATCH * SEQ
N_LAT = DEC_BATCH * DEC_SEQ
N_TOK = N_CTX + N_LAT
LANES = 128
ROW_TILE = 256
IN_TILE = 512
LAT_Q_TILE = 256
N_TILES = N_TOK // ROW_TILE
CTX_TILES = N_CTX // ROW_TILE
LAT_TILES_PER_BATCH = DEC_SEQ // ROW_TILE
N_GROUPS = 1 + DEC_BATCH
EXPERT_TILE = 256
N_ASSIGN = N_TOK * TOP_K
N_EXPERT_BLOCKS = N_ASSIGN // EXPERT_TILE + N_EXPERTS
M_PAD = N_EXPERT_BLOCKS * EXPERT_TILE
DMA_CHUNK = 32
DFT_ROW_BLOCK = 64
Q_CHUNKS = ATTN_W // LANES
NEG_BIG = -1e30
VMEM_LIMIT = 56 * 1024 * 1024

_NT = (((1,), (1,)), ((), ()))


def _params(sem, vmem=VMEM_LIMIT):
    return pltpu.CompilerParams(dimension_semantics=sem, vmem_limit_bytes=vmem)


def _tile_group(i, tile=ROW_TILE):
    ctx_tiles = N_CTX // tile
    return jnp.where(i < ctx_tiles, 0, 1 + (i - ctx_tiles) // (DEC_SEQ // tile))


def _rope_block(i, tile):
    ctx_tiles = N_CTX // tile
    return jnp.where(i < ctx_tiles, 0, 1 + (i - ctx_tiles) % (DEC_SEQ // tile))


def _mod_spec(layer, slot, tile=ROW_TILE):
    return pl.BlockSpec((None, 1, D_MODEL),
                        lambda i: ((layer * N_GROUPS + _tile_group(i, tile)) * 6 + slot, 0, 0))


def _layer_vec_spec(layer, width=D_MODEL):
    return pl.BlockSpec((None, 1, width), lambda *_: (layer, 0, 0))


ROW_SUBLANES = D_MODEL // LANES


def _store_token_rows(ref, value):
    n = value.shape[0]
    for s in range(ROW_SUBLANES):
        ref[pl.ds(s, n, stride=ROW_SUBLANES), :] = value[:, s * LANES:(s + 1) * LANES]


def _load_token_rows(ref):
    n = ref.shape[0] // ROW_SUBLANES
    return jnp.concatenate([ref[pl.ds(s, n, stride=ROW_SUBLANES), :] for s in range(ROW_SUBLANES)], axis=1)


def _token_rows(token):
    return pl.ds(pl.multiple_of(token * ROW_SUBLANES, ROW_SUBLANES), ROW_SUBLANES)


def _mods_kernel(cond_ref, w_ref, b_ref, o_ref):
    c = cond_ref[...]
    s = c * jax.nn.sigmoid(c)
    w = w_ref[...]
    o_ref[...] = jnp.broadcast_to(b_ref[...], o_ref.shape)
    for g in range(N_GROUPS):
        o_ref[g:g + 1, :] = jnp.sum(w * s[:, g:g + 1], axis=0, keepdims=True) + b_ref[...]


def _mods(cond, w_ada, b_ada):
    tn = 1536
    return pl.pallas_call(
        _mods_kernel,
        grid=(DEPTH, 6 * D_MODEL // tn),
        in_specs=[pl.BlockSpec((D_MODEL, 8), lambda l, j: (0, 0)),
                  pl.BlockSpec((None, D_MODEL, tn), lambda l, j: (l, 0, j)),
                  pl.BlockSpec((None, 1, tn), lambda l, j: (l, 0, j))],
        out_specs=pl.BlockSpec((None, 8, tn), lambda l, j: (l, 0, j)),
        out_shape=jax.ShapeDtypeStruct((DEPTH, 8, 6 * D_MODEL), F32),
        compiler_params=_params(("parallel", "parallel")),
        name="adaln_mods",
    )(cond, w_ada, b_ada.reshape(DEPTH, 1, 6 * D_MODEL))


def _inproj_kernel(x_ref, g_ref, sh_ref, sc_ref, w_ref, qg_ref, kg_ref, cos_ref, sin_ref,
                   bd_ref, cs_ref,
                   q_ref, kx_ref, vx_ref, kf_ref, vf_ref, fcs_ref, ga_ref, gf_ref, h_scr):
    x = x_ref[...]
    y = x * lax.rsqrt(jnp.mean(x * x, axis=-1, keepdims=True) + NORM_EPS) * g_ref[...]
    h_scr[...] = (y * (1.0 + sc_ref[...]) + sh_ref[...]).astype(BF16)
    hb = h_scr[...]
    lane = lax.broadcasted_iota(I32, (x_ref.shape[0], LANES), 1)
    low_half = lane < HEAD_DIM
    rot_first = (lane % ROPE_AXIS_DIM) < (ROPE_AXIS_DIM // 2)
    cos = cos_ref[...]
    sin = sin_ref[...]
    bd = bd_ref[...]

    def proj(lo, width):
        return jnp.dot(hb, w_ref[:, lo:lo + width], preferred_element_type=F32)

    def head_norm_rope(p, gain):
        pp = p * p
        hi = pp.astype(BF16)
        lo = (pp - hi.astype(F32)).astype(BF16)
        msq = (jnp.dot(hi, bd, preferred_element_type=F32)
               + jnp.dot(lo, bd, preferred_element_type=F32))
        n = p * lax.rsqrt(msq + NORM_EPS) * gain
        partner = jnp.where(rot_first, pltpu.roll(n, LANES - ROPE_AXIS_DIM // 2, 1),
                            pltpu.roll(n, ROPE_AXIS_DIM // 2, 1))
        return n * cos + partner * sin

    def head_pairs(r):
        sw = pltpu.roll(r, HEAD_DIM, 1)
        return jnp.where(low_half, r, sw), r, jnp.where(low_half, sw, r)

    scale = HEAD_DIM ** -0.5
    wide = 2 * LANES
    for c2 in range(ATTN_W // wide):
        p = proj(c2 * wide, wide)
        for t in range(2):
            c = 2 * c2 + t
            r = head_norm_rope(p[:, t * LANES:(t + 1) * LANES], qg_ref[...])
            q_ref[:, c * LANES:(c + 1) * LANES] = (r * scale).astype(BF16)

    p = proj(ATTN_W, KV_W)
    for j in range(KV_W // LANES):
        r = head_norm_rope(p[:, j * LANES:(j + 1) * LANES], kg_ref[...])
        kf_ref[:, j * LANES:(j + 1) * LANES] = r
        for t, piece in enumerate(head_pairs(r)):
            c = 3 * j + t
            kx_ref[:, c * LANES:(c + 1) * LANES] = piece.astype(BF16)

    p = proj(ATTN_W + KV_W, KV_W)
    vf_ref[...] = p
    for j in range(KV_W // LANES):
        for t, piece in enumerate(head_pairs(p[:, j * LANES:(j + 1) * LANES])):
            c = 3 * j + t
            vx_ref[:, c * LANES:(c + 1) * LANES] = piece.astype(BF16)

    f = proj(ATTN_W + 2 * KV_W, FOURIER_W)
    fcs_ref[...] = jnp.dot(f.astype(BF16), cs_ref[...], preferred_element_type=F32).astype(BF16)

    gate_lo = ATTN_W + 2 * KV_W + FOURIER_W
    half = D_MODEL // 2
    for j in range(2):
        ga_ref[:, j * half:(j + 1) * half] = jax.nn.sigmoid(proj(gate_lo + j * half, half)).astype(BF16)
        gf_ref[:, j * half:(j + 1) * half] = jax.nn.sigmoid(
            proj(gate_lo + D_MODEL + j * half, half)).astype(BF16)


def _inproj(layer, x, norm_g, mods, w_in, qg, kg, cos_tab, sin_tab, bd, cs):
    row = lambda w: pl.BlockSpec((IN_TILE, w), lambda i: (i, 0))
    const = lambda a: pl.BlockSpec(a.shape, lambda i: (0,) * a.ndim)
    rope = pl.BlockSpec((IN_TILE, LANES), lambda i: (_rope_block(i, IN_TILE), 0))
    ext_w = Q_CHUNKS * LANES
    return pl.pallas_call(
        _inproj_kernel,
        grid=(N_TOK // IN_TILE,),
        in_specs=[row(D_MODEL), _layer_vec_spec(layer), _mod_spec(layer, 0, IN_TILE), _mod_spec(layer, 1, IN_TILE),
                  pl.BlockSpec((None, D_MODEL, IN_W), lambda i: (layer, 0, 0)),
                  _layer_vec_spec(layer, LANES), _layer_vec_spec(layer, LANES), rope, rope,
                  const(bd), const(cs)],
        out_specs=[row(ATTN_W), row(ext_w), row(ext_w), row(KV_W), row(KV_W),
                   row(2 * FOURIER_W), row(D_MODEL), row(D_MODEL)],
        out_shape=[jax.ShapeDtypeStruct((N_TOK, ATTN_W), BF16),
                   jax.ShapeDtypeStruct((N_TOK, ext_w), BF16),
                   jax.ShapeDtypeStruct((N_TOK, ext_w), BF16),
                   jax.ShapeDtypeStruct((N_TOK, KV_W), F32),
                   jax.ShapeDtypeStruct((N_TOK, KV_W), F32),
                   jax.ShapeDtypeStruct((N_TOK, 2 * FOURIER_W), BF16),
                   jax.ShapeDtypeStruct((N_TOK, D_MODEL), BF16),
                   jax.ShapeDtypeStruct((N_TOK, D_MODEL), BF16)],
        scratch_shapes=[pltpu.VMEM((IN_TILE, D_MODEL), BF16)],
        compiler_params=_params(("parallel",)),
        name=f"inproj_l{layer}",
    )(x, norm_g, mods, mods, w_in, qg, kg, cos_tab, sin_tab, bd, cs)


def _attn_kernel(*refs, n_parts):
    q_ref = refs[0]
    k_refs = refs[1:1 + n_parts]
    v_refs = refs[1 + n_parts:1 + 2 * n_parts]
    o_ref = refs[-1]
    tq = q_ref.shape[0]
    low_half = lax.broadcasted_iota(I32, (tq, LANES), 1) < HEAD_DIM
    for c in range(Q_CHUNKS):
        cols = slice(c * LANES, (c + 1) * LANES)
        qc = q_ref[:, cols]
        zero = jnp.zeros_like(qc)
        outs = []
        for mask in (low_half, jnp.logical_not(low_half)):
            qm = jnp.where(mask, qc, zero)
            ss = [lax.dot_general(qm, k[:, cols], _NT, preferred_element_type=F32) for k in k_refs]
            m = ss[0].max(axis=-1, keepdims=True)
            for s in ss[1:]:
                m = jnp.maximum(m, s.max(axis=-1, keepdims=True))
            ps = [jnp.exp(s - m) for s in ss]
            den = ps[0].sum(axis=-1, keepdims=True)
            for p in ps[1:]:
                den = den + p.sum(axis=-1, keepdims=True)
            acc = jnp.dot(ps[0].astype(BF16), v_refs[0][:, cols], preferred_element_type=F32)
            for p, v in zip(ps[1:], v_refs[1:]):
                acc = acc + jnp.dot(p.astype(BF16), v[:, cols], preferred_element_type=F32)
            outs.append(acc / den)
        o_ref[:, cols] = jnp.where(low_half, outs[0], outs[1]).astype(BF16)


def _attention_ctx(layer, q, kx, vx):
    w = Q_CHUNKS * LANES
    blk = lambda width: pl.BlockSpec((SEQ, width), lambda b: (b, 0))
    return pl.pallas_call(
        functools.partial(_attn_kernel, n_parts=1),
        grid=(BATCH,),
        in_specs=[blk(ATTN_W), blk(w), blk(w)],
        out_specs=blk(ATTN_W),
        out_shape=jax.ShapeDtypeStruct((N_TOK, ATTN_W), BF16),
        input_output_aliases={0: 0},
        compiler_params=_params(("parallel",)),
        name=f"attn_ctx_l{layer}",
    )(q, kx, vx)


def _attention_lat(layer, q, kx, vx, kcache, vcache):
    w = Q_CHUNKS * LANES
    nq = DEC_SEQ // LAT_Q_TILE
    first = N_CTX // LAT_Q_TILE
    qblk = pl.BlockSpec((LAT_Q_TILE, ATTN_W), lambda b, i: (first + b * nq + i, 0))
    new = pl.BlockSpec((DEC_SEQ, w), lambda b, i: (N_CTX // DEC_SEQ + b, 0))
    old = pl.BlockSpec((None, PAST_LEN, w), lambda b, i: (b, 0, 0))
    return pl.pallas_call(
        functools.partial(_attn_kernel, n_parts=2),
        grid=(DEC_BATCH, nq),
        in_specs=[qblk, old, new, old, new],
        out_specs=qblk,
        out_shape=jax.ShapeDtypeStruct((N_TOK, ATTN_W), BF16),
        input_output_aliases={0: 0},
        compiler_params=_params(("parallel", "parallel")),
        name=f"attn_lat_l{layer}",
    )(q, kcache, kx, vcache, vx)


def _merge_kernel(a_ref, fc_ref, fs_ref, cl_ref, sl_ref, ga_ref, gf_ref, x_ref, gate_ref,
                  wa_ref, wf_ref, wo_ref, o_ref):
    attn = jnp.dot(a_ref[...], wa_ref[...], preferred_element_type=F32)
    fo = (jnp.dot(cl_ref[...], fc_ref[...], preferred_element_type=F32)
          - jnp.dot(sl_ref[...], fs_ref[...], preferred_element_type=F32))
    four = jnp.dot(fo.astype(BF16), wf_ref[...], preferred_element_type=F32)
    merged = ga_ref[...].astype(F32) * attn + gf_ref[...].astype(F32) * four
    mix = jnp.dot(merged.astype(BF16), wo_ref[...], preferred_element_type=F32)
    o_ref[...] = x_ref[...] + gate_ref[...] * mix


def _merge(layer, stream, a, fcs, cl, sl, ga, gf, x, mods, wa, wf, wo):
    if stream == "ctx":
        seq, n_batch, first_tile, first_seq = SEQ, BATCH, 0, 0
    else:
        seq, n_batch, first_tile, first_seq = DEC_SEQ, DEC_BATCH, CTX_TILES, N_CTX // DEC_SEQ
    nt = seq // ROW_TILE
    tile = lambda b, i: first_tile + b * nt + i
    row = lambda w: pl.BlockSpec((ROW_TILE, w), lambda b, i: (tile(b, i), 0))
    wspec = lambda a_: pl.BlockSpec((None,) + a_.shape[1:], lambda b, i: (layer, 0, 0))
    in_specs = [row(ATTN_W),
                pl.BlockSpec((seq, FOURIER_W), lambda b, i: (first_seq + b, 0)),
                pl.BlockSpec((seq, FOURIER_W), lambda b, i: (first_seq + b, 1)),
                pl.BlockSpec((ROW_TILE, seq), lambda b, i: (i, 0)),
                pl.BlockSpec((ROW_TILE, seq), lambda b, i: (i, 0)),
                row(D_MODEL), row(D_MODEL), row(D_MODEL),
                pl.BlockSpec((None, 1, D_MODEL),
                             lambda b, i: ((layer * N_GROUPS + _tile_group(tile(b, i))) * 6 + 2, 0, 0)),
                wspec(wa), wspec(wf), wspec(wo)]
    args = [a, fcs, fcs, cl, sl, ga, gf, x, mods, wa, wf, wo]
    aliases = {7: 0}
    return pl.pallas_call(
        _merge_kernel,
        grid=(n_batch, nt),
        in_specs=in_specs,
        out_specs=row(D_MODEL),
        out_shape=jax.ShapeDtypeStruct((N_TOK, D_MODEL), F32),
        input_output_aliases=aliases,
        compiler_params=_params(("parallel", "parallel")),
        name=f"merge_{stream}_l{layer}",
    )(*args)


def _router_kernel(x_ref, g_ref, sh_ref, sc_ref, wr_ref, br_ref, tri_ref,
                   h_ref, te_ref, tg_ref, rk_ref, cnt_ref, carry):
    @pl.when(pl.program_id(0) == 0)
    def _():
        carry[...] = jnp.zeros_like(carry)

    x = x_ref[...]
    y = x * lax.rsqrt(jnp.mean(x * x, axis=-1, keepdims=True) + NORM_EPS) * g_ref[...]
    h = y * (1.0 + sc_ref[...]) + sh_ref[...]
    _store_token_rows(h_ref, h)
    logits = lax.dot_general(wr_ref[...], h, _NT, preferred_element_type=F32,
                             precision=lax.Precision.HIGHEST) + br_ref[...]
    sub = lax.broadcasted_iota(I32, logits.shape, 0).astype(F32)
    vals, ids = [], []
    for _ in range(TOP_K):
        m = logits.max(axis=0, keepdims=True)
        idx = jnp.where(logits == m, sub, float(N_EXPERTS)).min(axis=0, keepdims=True)
        vals.append(m)
        ids.append(idx)
        logits = jnp.where(sub == idx, NEG_BIG, logits)
    es = [jnp.exp(v - vals[0]) for v in vals]
    den = es[0] + es[1] + es[2] + es[3]

    chosen = [sub == idx for idx in ids]
    member = jnp.zeros_like(sub)
    for ch in chosen:
        member = member + ch.astype(F32)
    upto = jnp.dot(member.astype(BF16), tri_ref[...], preferred_element_type=F32)
    before = carry[...] + upto - member
    for k in range(TOP_K):
        te_ref[k:k + 1, :] = ids[k].astype(I32)
        tg_ref[k:k + 1, :] = es[k] / den
        rk_ref[k:k + 1, :] = jnp.where(chosen[k], before, 0.0).sum(axis=0, keepdims=True).astype(I32)
    carry[...] = carry[...] + member.sum(axis=1, keepdims=True)
    cnt_ref[...] = carry[:, :LANES].astype(I32)


def _router(layer, x1, norm_g, mods, wr_t, br_t, tri):
    row = lambda w: pl.BlockSpec((ROW_TILE, w), lambda i: (i, 0))
    col = pl.BlockSpec((TOP_K, ROW_TILE), lambda i: (0, i))
    return pl.pallas_call(
        _router_kernel,
        grid=(N_TILES,),
        in_specs=[row(D_MODEL), _layer_vec_spec(layer), _mod_spec(layer, 3), _mod_spec(layer, 4),
                  pl.BlockSpec((None, N_EXPERTS, D_MODEL), lambda i: (layer, 0, 0)),
                  pl.BlockSpec((None, N_EXPERTS, 1), lambda i: (layer, 0, 0)),
                  pl.BlockSpec((ROW_TILE, ROW_TILE), lambda i: (0, 0))],
        out_specs=[pl.BlockSpec((ROW_TILE * ROW_SUBLANES, LANES), lambda i: (i, 0)), col, col, col,
                   pl.BlockSpec((N_EXPERTS, LANES), lambda i: (0, 0))],
        out_shape=[jax.ShapeDtypeStruct((N_TOK * ROW_SUBLANES, LANES), F32),
                   jax.ShapeDtypeStruct((TOP_K, N_TOK), I32),
                   jax.ShapeDtypeStruct((TOP_K, N_TOK), F32),
                   jax.ShapeDtypeStruct((TOP_K, N_TOK), I32),
                   jax.ShapeDtypeStruct((N_EXPERTS, LANES), I32)],
        scratch_shapes=[pltpu.VMEM((N_EXPERTS, ROW_TILE), F32)],
        compiler_params=_params(("arbitrary",)),
        name=f"router_l{layer}",
    )(x1, norm_g, mods, mods, wr_t, br_t, tri)


def _dispatch_kernel(dest_ref, zblk_ref, nblk_ref, h_ref, xs_hbm, zeros, sem):
    i = pl.program_id(0)
    block_rows = EXPERT_TILE * ROW_SUBLANES

    def zero_fill(block):
        rows = pl.ds(pl.multiple_of(block * block_rows, block_rows), block_rows)
        return pltpu.make_async_copy(zeros, xs_hbm.at[rows, :], sem)

    @pl.when(i == 0)
    def _():
        zeros[...] = jnp.zeros_like(zeros)
        for wait in (False, True):
            for e in range(N_EXPERTS):
                @pl.when(zblk_ref[e] >= 0)
                def _():
                    cp = zero_fill(zblk_ref[e])
                    cp.wait() if wait else cp.start()

            def tail(j, carry):
                cp = zero_fill(j)
                cp.wait() if wait else cp.start()
                return carry

            lax.fori_loop(nblk_ref[0], N_EXPERT_BLOCKS, tail, 0)

    def row_copies(chunk):
        copies = []
        for j in range(DMA_CHUNK):
            r = chunk * DMA_CHUNK + j
            for k in range(TOP_K):
                d = dest_ref[k * N_TOK + i * ROW_TILE + r]
                copies.append(pltpu.make_async_copy(h_ref.at[_token_rows(r), :],
                                                    xs_hbm.at[_token_rows(d), :], sem))
        return copies

    def start(chunk, carry):
        for n, cp in enumerate(row_copies(chunk)):
            cp.start(priority=n % 2)
        return carry

    def wait(chunk, carry):
        for cp in row_copies(chunk):
            cp.wait()
        return carry

    lax.fori_loop(0, ROW_TILE // DMA_CHUNK, start, 0)
    lax.fori_loop(0, ROW_TILE // DMA_CHUNK, wait, 0)


def _dispatch(layer, dest, zblk, nblk, h2):
    grid_spec = pltpu.PrefetchScalarGridSpec(
        num_scalar_prefetch=3,
        grid=(N_TILES,),
        in_specs=[pl.BlockSpec((ROW_TILE * ROW_SUBLANES, LANES), lambda i, d, z, n: (i, 0))],
        out_specs=pl.BlockSpec(memory_space=pl.ANY),
        scratch_shapes=[pltpu.VMEM((EXPERT_TILE * ROW_SUBLANES, LANES), F32), pltpu.SemaphoreType.DMA(())],
    )
    return pl.pallas_call(
        _dispatch_kernel,
        grid_spec=grid_spec,
        out_shape=jax.ShapeDtypeStruct((M_PAD * ROW_SUBLANES, LANES), F32),
        compiler_params=_params(("arbitrary",)),
        name=f"dispatch_l{layer}",
    )(dest, zblk, nblk, h2)


def _expert_kernel(be_ref, kind_ref, slot_ref, next_ref, x_ref, wi_hbm, bi_ref, wo_hbm, bo_ref, y_ref,
                   wi_f32, wo_f32, wi_bf, wo_bf, sems, *, layer):
    b = pl.program_id(0)

    def weight_copies(expert, slot):
        return (pltpu.make_async_copy(wi_hbm.at[layer, expert], wi_f32.at[slot], sems.at[0, slot]),
                pltpu.make_async_copy(wo_hbm.at[layer, expert], wo_f32.at[slot], sems.at[1, slot]))

    @pl.when(kind_ref[b] == 0)
    def _():
        y_ref[...] = jnp.zeros_like(y_ref)

    @pl.when(kind_ref[b] == 2)
    def _():
        slot = slot_ref[b]
        own = weight_copies(be_ref[b], slot)

        @pl.when(b == 0)
        def _():
            for cp in own:
                cp.start()

        for cp in own:
            cp.wait()

        @pl.when(next_ref[b] >= 0)
        def _():
            for cp in weight_copies(next_ref[b], 1 - slot):
                cp.start(priority=1)

        wi_bf[...] = wi_f32[slot].astype(BF16)
        wo_bf[...] = wo_f32[slot].astype(BF16)

    @pl.when(kind_ref[b] > 0)
    def _():
        x = _load_token_rows(x_ref).astype(BF16)
        hdn = jnp.dot(x, wi_bf[...], preferred_element_type=F32) + bi_ref[...]
        glu = jnp.minimum(hdn[:, :D_EXPERT], SWIGLU_LIMIT)
        lin = jnp.clip(hdn[:, D_EXPERT:], -SWIGLU_LIMIT, SWIGLU_LIMIT)
        act = glu * jax.nn.sigmoid(SWIGLU_ALPHA * glu) * (lin + 1.0)
        y = jnp.dot(act.astype(BF16), wo_bf[...], preferred_element_type=F32) + bo_ref[...]
        _store_token_rows(y_ref, y)


def _experts(layer, blk_e, blk_kind, blk_slot, blk_next, xs, w_exp_in, b_exp_in, w_exp_out, b_exp_out):
    bias = lambda cols: pl.BlockSpec((None, None, 1, cols), lambda b, be, *_: (layer, be[b], 0, 0))
    blk = pl.BlockSpec((EXPERT_TILE * ROW_SUBLANES, LANES), lambda b, *_: (b, 0))
    hbm = pl.BlockSpec(memory_space=pl.ANY)
    grid_spec = pltpu.PrefetchScalarGridSpec(
        num_scalar_prefetch=4,
        grid=(N_EXPERT_BLOCKS,),
        in_specs=[blk, hbm, bias(2 * D_EXPERT), hbm, bias(D_MODEL)],
        out_specs=blk,
        scratch_shapes=[pltpu.VMEM((2, D_MODEL, 2 * D_EXPERT), F32),
                        pltpu.VMEM((2, D_EXPERT, D_MODEL), F32),
                        pltpu.VMEM((D_MODEL, 2 * D_EXPERT), BF16),
                        pltpu.VMEM((D_EXPERT, D_MODEL), BF16),
                        pltpu.SemaphoreType.DMA((2, 2))],
    )
    return pl.pallas_call(
        functools.partial(_expert_kernel, layer=layer),
        grid_spec=grid_spec,
        out_shape=jax.ShapeDtypeStruct((M_PAD * ROW_SUBLANES, LANES), F32),
        compiler_params=_params(("arbitrary",)),
        name=f"experts_l{layer}",
    )(blk_e, blk_kind, blk_slot, blk_next, xs, w_exp_in,
      b_exp_in.reshape(DEPTH, N_EXPERTS, 1, 2 * D_EXPERT), w_exp_out,
      b_exp_out.reshape(DEPTH, N_EXPERTS, 1, D_MODEL))


def _combine_kernel(dest_ref, y_hbm, x_ref, tg_ref, gate_ref, *rest, final):
    if final:
        fg_ref, o_ref, n_ref, buf, sems = rest
    else:
        o_ref, buf, sems = rest
    i = pl.program_id(0)
    slot = i % 2

    def row_copies(tile, slot, chunk):
        copies = []
        for j in range(DMA_CHUNK):
            r = chunk * DMA_CHUNK + j
            for k in range(TOP_K):
                src = y_hbm.at[_token_rows(dest_ref[k * N_TOK + tile * ROW_TILE + r]), :]
                copies.append(pltpu.make_async_copy(src, buf.at[slot, k, _token_rows(r), :], sems.at[slot]))
        return copies

    def start_tile(tile, slot):
        def body(chunk, carry):
            for n, cp in enumerate(row_copies(tile, slot, chunk)):
                cp.start(priority=n % 2)
            return carry
        lax.fori_loop(0, ROW_TILE // DMA_CHUNK, body, 0)

    def wait_tile(tile, slot):
        def body(chunk, carry):
            for cp in row_copies(tile, slot, chunk):
                cp.wait()
            return carry
        lax.fori_loop(0, ROW_TILE // DMA_CHUNK, body, 0)

    @pl.when(i == 0)
    def _():
        start_tile(0, 0)

    @pl.when(i + 1 < pl.num_programs(0))
    def _():
        start_tile(i + 1, 1 - slot)

    wait_tile(i, slot)

    tg = tg_ref[...]
    acc = tg[:, 0:1] * _load_token_rows(buf.at[slot, 0])
    for k in range(1, TOP_K):
        acc = acc + tg[:, k:k + 1] * _load_token_rows(buf.at[slot, k])
    out = x_ref[...] + gate_ref[...] * acc
    o_ref[...] = out
    if final:
        n_ref[...] = out * lax.rsqrt(jnp.mean(out * out, axis=-1, keepdims=True) + NORM_EPS) * fg_ref[...]


def _combine(layer, dest, yb, x1, tg, mods, final_g=None):
    final = final_g is not None
    row = lambda w: pl.BlockSpec((ROW_TILE, w), lambda i, d: (i, 0))
    in_specs = [pl.BlockSpec(memory_space=pl.ANY), row(D_MODEL), row(TOP_K),
                pl.BlockSpec((None, 1, D_MODEL),
                             lambda i, d: ((layer * N_GROUPS + _tile_group(i)) * 6 + 5, 0, 0))]
    args = [dest, yb, x1, tg, mods]
    out_specs = [row(D_MODEL)]
    out_shape = [jax.ShapeDtypeStruct((N_TOK, D_MODEL), F32)]
    if final:
        in_specs.append(pl.BlockSpec((1, D_MODEL), lambda i, d: (0, 0)))
        args.append(final_g)
        out_specs.append(row(D_MODEL))
        out_shape.append(jax.ShapeDtypeStruct((N_TOK, D_MODEL), F32))
    grid_spec = pltpu.PrefetchScalarGridSpec(
        num_scalar_prefetch=1,
        grid=(N_TILES,),
        in_specs=in_specs,
        out_specs=out_specs,
        scratch_shapes=[pltpu.VMEM((2, TOP_K, ROW_TILE * ROW_SUBLANES, LANES), F32),
                        pltpu.SemaphoreType.DMA((2,))],
    )
    return pl.pallas_call(
        functools.partial(_combine_kernel, final=final),
        grid_spec=grid_spec,
        out_shape=out_shape,
        compiler_params=_params(("arbitrary",)),
        name=f"combine_l{layer}",
    )(*args)


def _rope_tables():
    rows = DEC_SEQ // GRID_W
    row = jnp.repeat(jnp.arange(rows, dtype=F32), GRID_W)
    col = jnp.tile(jnp.arange(GRID_W, dtype=F32), rows)
    inv_freq = ROPE_THETA ** (-jnp.arange(0, ROPE_AXIS_DIM, 2, dtype=F32) / ROPE_AXIS_DIM)
    ang = jnp.stack([row[:, None] * inv_freq, col[:, None] * inv_freq], axis=1)
    cos, sin = jnp.cos(ang), jnp.sin(ang)
    cos_h = jnp.concatenate([cos, cos], axis=-1).reshape(DEC_SEQ, HEAD_DIM)
    sin_h = jnp.concatenate([-sin, sin], axis=-1).reshape(DEC_SEQ, HEAD_DIM)
    reps = LANES // HEAD_DIM
    cos_t = jnp.concatenate([jnp.ones((IN_TILE, LANES), F32), jnp.tile(cos_h, (1, reps))], axis=0)
    sin_t = jnp.concatenate([jnp.zeros((IN_TILE, LANES), F32), jnp.tile(sin_h, (1, reps))], axis=0)
    return cos_t, sin_t


def _dft_tables(n, scale):
    col = jnp.arange(n, dtype=I32)

    def direct(rows):
        ang = ((rows[:, None] * col[None, :]) % n).astype(F32) * (2.0 * math.pi / n)
        return jnp.cos(ang), jnp.sin(ang)

    if n <= DFT_ROW_BLOCK:
        c, s = direct(col)
        return c * scale, s * scale
    lo_c, lo_s = direct(jnp.arange(DFT_ROW_BLOCK, dtype=I32))
    hi_c, hi_s = direct(jnp.arange(n // DFT_ROW_BLOCK, dtype=I32) * DFT_ROW_BLOCK)
    hi_c, hi_s = (hi_c * scale)[:, None, :], (hi_s * scale)[:, None, :]
    c = hi_c * lo_c[None] - hi_s * lo_s[None]
    s = hi_s * lo_c[None] + hi_c * lo_s[None]
    return c.reshape(n, n), s.reshape(n, n)


def _channel_dft():
    c, s = _dft_tables(FOURIER_GROUP_W, 1.0)
    eye = jnp.eye(N_FOURIER_GROUPS, dtype=F32)
    return jnp.concatenate([jnp.kron(eye, c), jnp.kron(eye, s)], axis=1).astype(BF16)


def _extend_heads(t):
    h = [t[..., i * HEAD_DIM:(i + 1) * HEAD_DIM] for i in range(N_KV_HEADS)]
    order = [0, 0, 0, 1, 1, 1, 2, 2, 2, 3, 3, 3]
    return jnp.concatenate([h[i] for i in order], axis=-1)


def _plan(top_e_t, rank_t, counts):
    cnt = counts[:N_EXPERTS, 0]
    padded = (cnt + EXPERT_TILE - 1) // EXPERT_TILE * EXPERT_TILE
    pad_end = jnp.cumsum(padded)
    pad_start = pad_end - padded
    experts = jnp.arange(N_EXPERTS, dtype=I32)
    start_of = jnp.sum(jnp.where(top_e_t[..., None] == experts, pad_start, 0), axis=-1)
    dest = (start_of + rank_t).reshape(-1).astype(I32)
    blk_start = jnp.arange(N_EXPERT_BLOCKS, dtype=I32) * EXPERT_TILE
    blk_e = jnp.minimum(jnp.sum(pad_end[None, :] <= blk_start[:, None], axis=1), N_EXPERTS - 1).astype(I32)
    valid = blk_start < pad_end[-1]
    first = jnp.logical_and(valid, jnp.concatenate([jnp.ones((1,), bool), blk_e[1:] != blk_e[:-1]]))
    blk_kind = (valid.astype(I32) + first.astype(I32)).astype(I32)
    blk_slot = ((jnp.cumsum(first.astype(I32)) - 1) % 2).astype(I32)
    later = jnp.logical_and(experts[None, :] > experts[:, None], cnt[None, :] > 0)
    next_of = jnp.min(jnp.where(later, experts[None, :], N_EXPERTS), axis=1)
    next_of = jnp.where(next_of == N_EXPERTS, -1, next_of)
    blk_next = jnp.sum(jnp.where(blk_e[:, None] == experts[None, :], next_of[None, :], 0), axis=1).astype(I32)
    zero_blk = jnp.where(cnt % EXPERT_TILE != 0, (pad_start + cnt) // EXPERT_TILE, -1).astype(I32)
    n_blk = (pad_end[-1:] // EXPERT_TILE).astype(I32)
    return dest, (blk_e, blk_kind, blk_slot, blk_next), zero_blk, n_blk


def kernel(x_prompt, x_sample, cache_k, cache_v, c, c_ctx, w_ada, b_ada, norm1_g, w_in, q_norm_g,
           k_norm_g, w_attn_o, w_fourier_o, w_out, norm2_g, w_router, b_router, w_exp_in, b_exp_in,
           w_exp_out, b_exp_out, final_norm_g):
    x = jnp.concatenate([x_prompt.reshape(N_CTX, D_MODEL), x_sample.reshape(N_LAT, D_MODEL)], axis=0)

    cond = jnp.concatenate([c_ctx[None, :], c, jnp.zeros((8 - N_GROUPS, D_MODEL), F32)], axis=0)
    mods = _mods(cond.T, w_ada, b_ada)[:, :N_GROUPS].reshape(DEPTH * N_GROUPS * 6, 1, D_MODEL)

    cos_t, sin_t = _rope_tables()
    bd = jnp.kron(jnp.eye(LANES // HEAD_DIM, dtype=F32),
                  jnp.full((HEAD_DIM, HEAD_DIM), 1.0 / HEAD_DIM, F32)).astype(BF16)
    cs = _channel_dft()
    dft_ctx = [t.astype(BF16) for t in _dft_tables(SEQ, (SEQ * FOURIER_GROUP_W) ** -0.5)]
    dft_lat = [t.astype(BF16) for t in _dft_tables(DEC_SEQ, (DEC_SEQ * FOURIER_GROUP_W) ** -0.5)]

    w_in_b = w_in.astype(BF16)
    wa_b = w_attn_o.astype(BF16)
    wf_b = w_fourier_o.astype(BF16)
    wo_b = w_out.astype(BF16)
    norm1 = norm1_g.reshape(DEPTH, 1, D_MODEL)
    norm2 = norm2_g.reshape(DEPTH, 1, D_MODEL)
    qg = jnp.tile(q_norm_g, (1, LANES // HEAD_DIM)).reshape(DEPTH, 1, LANES)
    kg = jnp.tile(k_norm_g, (1, LANES // HEAD_DIM)).reshape(DEPTH, 1, LANES)
    wr_t = jnp.swapaxes(w_router, 1, 2)
    br_t = b_router.reshape(DEPTH, N_EXPERTS, 1)
    tok = jnp.arange(ROW_TILE, dtype=I32)
    tri = (tok[:, None] <= tok[None, :]).astype(BF16)
    kcache = _extend_heads(cache_k.reshape(DEC_BATCH, DEPTH, PAST_LEN, KV_W)).astype(BF16)
    vcache = _extend_heads(cache_v.reshape(DEC_BATCH, DEPTH, PAST_LEN, KV_W)).astype(BF16)

    keys, vals = [], []
    y_norm = None
    for l in range(DEPTH):
        q, kx, vx, kf, vf, fcs, ga, gf = _inproj(l, x, norm1, mods, w_in_b, qg, kg, cos_t, sin_t, bd, cs)
        keys.append(kf[:N_CTX])
        vals.append(vf[:N_CTX])
        a = _attention_ctx(l, q, kx, vx)
        a = _attention_lat(l, a, kx, vx, kcache[:, l], vcache[:, l])
        x1 = _merge(l, "ctx", a, fcs, dft_ctx[0], dft_ctx[1], ga, gf, x, mods, wa_b, wf_b, wo_b)
        x1 = _merge(l, "lat", a, fcs, dft_lat[0], dft_lat[1], ga, gf, x1, mods, wa_b, wf_b, wo_b)
        h2, top_e, top_g, rank, counts = _router(l, x1, norm2, mods, wr_t, br_t, tri)
        dest, blk_tables, zero_blk, n_blk = _plan(top_e, rank, counts)
        xs = _dispatch(l, dest, zero_blk, n_blk, h2)
        yb = _experts(l, *blk_tables, xs, w_exp_in, b_exp_in, w_exp_out, b_exp_out)
        top_g = top_g.T
        if l == DEPTH - 1:
            x, y_norm = _combine(l, dest, yb, x1, top_g, mods, final_norm_g.reshape(1, D_MODEL))
        else:
            (x,) = _combine(l, dest, yb, x1, top_g, mods)

    y_prompt = y_norm[:N_CTX].reshape(BATCH, SEQ, D_MODEL)
    y_sample = y_norm[N_CTX:].reshape(DEC_BATCH, DEC_SEQ, D_MODEL)
    shape = (BATCH, SEQ, N_KV_HEADS, HEAD_DIM)
    new_k = jnp.stack([k.reshape(shape) for k in keys], axis=1)
    new_v = jnp.stack([v.reshape(shape) for v in vals], axis=1)
    return (y_prompt, y_sample, new_k, new_v)
```

```python
import functools
import math

import jax
import jax.numpy as jnp
from jax import lax
from jax.experimental import pallas as pl
from jax.experimental.pallas import tpu as pltpu

F32 = jnp.float32
BF16 = jnp.bfloat16
I32 = jnp.int32

D_MODEL = 1024
DEPTH = 4
BATCH = 16
SEQ = 256
DEC_BATCH = 2
DEC_SEQ = 2048
PAST_LEN = 512
GRID_W = 64
HEAD_DIM = 64
N_Q_HEADS = 12
N_KV_HEADS = 4
ATTN_W = N_Q_HEADS * HEAD_DIM
KV_W = N_KV_HEADS * HEAD_DIM
FOURIER_W = D_MODEL // 4
N_FOURIER_GROUPS = 4
FOURIER_GROUP_W = FOURIER_W // N_FOURIER_GROUPS
IN_W = ATTN_W + 2 * KV_W + FOURIER_W + 2 * D_MODEL
ROPE_AXIS_DIM = HEAD_DIM // 2
ROPE_THETA = 10000.0
N_EXPERTS = 32
TOP_K = 4
D_EXPERT = D_MODEL
SWIGLU_ALPHA = 1.702
SWIGLU_LIMIT = 7.0
NORM_EPS = 1e-6

N_CTX = BATCH * SEQ
N_LAT = DEC_BATCH * DEC_SEQ
N_TOK = N_CTX + N_LAT
LANES = 128
ROW_TILE = 256
IN_TILE = 1024
LAT_MERGE_TILE = 512
ROUTER_TILE = 512
LAT_Q_TILE = 256
N_TILES = N_TOK // ROW_TILE
CTX_TILES = N_CTX // ROW_TILE
LAT_TILES_PER_BATCH = DEC_SEQ // ROW_TILE
N_GROUPS = 1 + DEC_BATCH
EXPERT_TILE = 256
N_ASSIGN = N_TOK * TOP_K
N_EXPERT_BLOCKS = N_ASSIGN // EXPERT_TILE + N_EXPERTS
M_PAD = N_EXPERT_BLOCKS * EXPERT_TILE
DMA_CHUNK = 32
DFT_ROW_BLOCK = 64
Q_CHUNKS = ATTN_W // LANES
NEG_BIG = -1e30
VMEM_LIMIT = 56 * 1024 * 1024

_NT = (((1,), (1,)), ((), ()))


def _params(sem, vmem=VMEM_LIMIT):
    return pltpu.CompilerParams(dimension_semantics=sem, vmem_limit_bytes=vmem)


def _tile_group(i, tile=ROW_TILE):
    ctx_tiles = N_CTX // tile
    return jnp.where(i < ctx_tiles, 0, 1 + (i - ctx_tiles) // (DEC_SEQ // tile))


def _rope_block(i, tile):
    ctx_tiles = N_CTX // tile
    return jnp.where(i < ctx_tiles, 0, 1 + (i - ctx_tiles) % (DEC_SEQ // tile))


def _mod_spec(layer, slot, tile=ROW_TILE):
    return pl.BlockSpec((None, 1, D_MODEL),
                        lambda i: ((layer * N_GROUPS + _tile_group(i, tile)) * 6 + slot, 0, 0))


def _layer_vec_spec(layer, width=D_MODEL):
    return pl.BlockSpec((None, 1, width), lambda *_: (layer, 0, 0))


ROW_SUBLANES = D_MODEL // LANES


def _store_token_rows(ref, value):
    n = value.shape[0]
    for s in range(ROW_SUBLANES):
        ref[pl.ds(s, n, stride=ROW_SUBLANES), :] = value[:, s * LANES:(s + 1) * LANES]


def _load_token_rows(ref):
    n = ref.shape[0] // ROW_SUBLANES
    return jnp.concatenate([ref[pl.ds(s, n, stride=ROW_SUBLANES), :] for s in range(ROW_SUBLANES)], axis=1)


def _token_rows(token):
    return pl.ds(pl.multiple_of(token * ROW_SUBLANES, ROW_SUBLANES), ROW_SUBLANES)


def _mods_kernel(cond_ref, w_ref, b_ref, o_ref):
    c = cond_ref[...]
    s = c * jax.nn.sigmoid(c)
    w = w_ref[...]
    o_ref[...] = jnp.broadcast_to(b_ref[...], o_ref.shape)
    for g in range(N_GROUPS):
        o_ref[g:g + 1, :] = jnp.sum(w * s[:, g:g + 1], axis=0, keepdims=True) + b_ref[...]


def _mods(cond, w_ada, b_ada):
    tn = 1536
    return pl.pallas_call(
        _mods_kernel,
        grid=(DEPTH, 6 * D_MODEL // tn),
        in_specs=[pl.BlockSpec((D_MODEL, 8), lambda l, j: (0, 0)),
                  pl.BlockSpec((None, D_MODEL, tn), lambda l, j: (l, 0, j)),
                  pl.BlockSpec((None, 1, tn), lambda l, j: (l, 0, j))],
        out_specs=pl.BlockSpec((None, 8, tn), lambda l, j: (l, 0, j)),
        out_shape=jax.ShapeDtypeStruct((DEPTH, 8, 6 * D_MODEL), F32),
        compiler_params=_params(("parallel", "parallel")),
        name="adaln_mods",
    )(cond, w_ada, b_ada.reshape(DEPTH, 1, 6 * D_MODEL))


def _inproj_kernel(x_ref, g_ref, sh_ref, sc_ref, w_ref, qg_ref, kg_ref, cos_ref, sin_ref,
                   bd_ref, cs_ref,
                   q_ref, kx_ref, vx_ref, kf_ref, vf_ref, fcs_ref, ga_ref, gf_ref, h_scr):
    x = x_ref[...]
    y = x * lax.rsqrt(jnp.mean(x * x, axis=-1, keepdims=True) + NORM_EPS) * g_ref[...]
    h_scr[...] = (y * (1.0 + sc_ref[...]) + sh_ref[...]).astype(BF16)
    hb = h_scr[...]
    lane = lax.broadcasted_iota(I32, (x_ref.shape[0], LANES), 1)
    low_half = lane < HEAD_DIM
    rot_first = (lane % ROPE_AXIS_DIM) < (ROPE_AXIS_DIM // 2)
    cos = cos_ref[...]
    sin = sin_ref[...]
    bd = bd_ref[...]

    def proj(lo, width):
        return jnp.dot(hb, w_ref[:, lo:lo + width], preferred_element_type=F32)

    def head_norm_rope(p, gain):
        pp = p * p
        hi = pp.astype(BF16)
        lo = (pp - hi.astype(F32)).astype(BF16)
        msq = jnp.dot(jnp.concatenate([hi, lo], axis=1), bd, preferred_element_type=F32)
        n = p * lax.rsqrt(msq + NORM_EPS) * gain
        partner = jnp.where(rot_first, pltpu.roll(n, LANES - ROPE_AXIS_DIM // 2, 1),
                            pltpu.roll(n, ROPE_AXIS_DIM // 2, 1))
        return n * cos + partner * sin

    def head_pairs(r):
        sw = pltpu.roll(r, HEAD_DIM, 1)
        return jnp.where(low_half, r, sw), r, jnp.where(low_half, sw, r)

    scale = HEAD_DIM ** -0.5
    wide = 2 * LANES
    for c2 in range(ATTN_W // wide):
        p = proj(c2 * wide, wide)
        for t in range(2):
            c = 2 * c2 + t
            r = head_norm_rope(p[:, t * LANES:(t + 1) * LANES], qg_ref[...])
            q_ref[:, c * LANES:(c + 1) * LANES] = (r * scale).astype(BF16)

    p = proj(ATTN_W, KV_W)
    for j in range(KV_W // LANES):
        r = head_norm_rope(p[:, j * LANES:(j + 1) * LANES], kg_ref[...])
        kf_ref[:, j * LANES:(j + 1) * LANES] = r
        for t, piece in enumerate(head_pairs(r)):
            c = 3 * j + t
            kx_ref[:, c * LANES:(c + 1) * LANES] = piece.astype(BF16)

    p = proj(ATTN_W + KV_W, KV_W)
    vf_ref[...] = p
    for j in range(KV_W // LANES):
        for t, piece in enumerate(head_pairs(p[:, j * LANES:(j + 1) * LANES])):
            c = 3 * j + t
            vx_ref[:, c * LANES:(c + 1) * LANES] = piece.astype(BF16)

    f = proj(ATTN_W + 2 * KV_W, FOURIER_W)
    fcs_ref[...] = jnp.dot(f.astype(BF16), cs_ref[...], preferred_element_type=F32).astype(BF16)

    gate_lo = ATTN_W + 2 * KV_W + FOURIER_W
    half = D_MODEL // 2
    for j in range(2):
        ga_ref[:, j * half:(j + 1) * half] = jax.nn.sigmoid(proj(gate_lo + j * half, half)).astype(BF16)
        gf_ref[:, j * half:(j + 1) * half] = jax.nn.sigmoid(
            proj(gate_lo + D_MODEL + j * half, half)).astype(BF16)


def _inproj(layer, x, norm_g, mods, w_in, qg, kg, cos_tab, sin_tab, bd, cs):
    row = lambda w: pl.BlockSpec((IN_TILE, w), lambda i: (i, 0))
    const = lambda a: pl.BlockSpec(a.shape, lambda i: (0,) * a.ndim)
    rope = pl.BlockSpec((IN_TILE, LANES), lambda i: (_rope_block(i, IN_TILE), 0))
    ext_w = Q_CHUNKS * LANES
    return pl.pallas_call(
        _inproj_kernel,
        grid=(N_TOK // IN_TILE,),
        in_specs=[row(D_MODEL), _layer_vec_spec(layer), _mod_spec(layer, 0, IN_TILE), _mod_spec(layer, 1, IN_TILE),
                  pl.BlockSpec((None, D_MODEL, IN_W), lambda i: (layer, 0, 0)),
                  _layer_vec_spec(layer, LANES), _layer_vec_spec(layer, LANES), rope, rope,
                  const(bd), const(cs)],
        out_specs=[row(ATTN_W), row(ext_w), row(ext_w), row(KV_W), row(KV_W),
                   row(2 * FOURIER_W), row(D_MODEL), row(D_MODEL)],
        out_shape=[jax.ShapeDtypeStruct((N_TOK, ATTN_W), BF16),
                   jax.ShapeDtypeStruct((N_TOK, ext_w), BF16),
                   jax.ShapeDtypeStruct((N_TOK, ext_w), BF16),
                   jax.ShapeDtypeStruct((N_TOK, KV_W), F32),
                   jax.ShapeDtypeStruct((N_TOK, KV_W), F32),
                   jax.ShapeDtypeStruct((N_TOK, 2 * FOURIER_W), BF16),
                   jax.ShapeDtypeStruct((N_TOK, D_MODEL), BF16),
                   jax.ShapeDtypeStruct((N_TOK, D_MODEL), BF16)],
        scratch_shapes=[pltpu.VMEM((IN_TILE, D_MODEL), BF16)],
        compiler_params=_params(("parallel",)),
        name=f"inproj_l{layer}",
    )(x, norm_g, mods, mods, w_in, qg, kg, cos_tab, sin_tab, bd, cs)


def _attn_kernel(*refs, n_parts):
    q_ref = refs[0]
    k_refs = refs[1:1 + n_parts]
    v_refs = refs[1 + n_parts:1 + 2 * n_parts]
    o_ref = refs[-1]
    tq = q_ref.shape[0]
    low_half = lax.broadcasted_iota(I32, (tq, LANES), 1) < HEAD_DIM
    for c in range(Q_CHUNKS):
        cols = slice(c * LANES, (c + 1) * LANES)
        qc = q_ref[:, cols]
        zero = jnp.zeros_like(qc)
        outs = []
        for mask in (low_half, jnp.logical_not(low_half)):
            qm = jnp.where(mask, qc, zero)
            ss = [lax.dot_general(qm, k[:, cols], _NT, preferred_element_type=F32) for k in k_refs]
            m = ss[0].max(axis=-1, keepdims=True)
            for s in ss[1:]:
                m = jnp.maximum(m, s.max(axis=-1, keepdims=True))
            ps = [jnp.exp(s - m) for s in ss]
            den = ps[0].sum(axis=-1, keepdims=True)
            for p in ps[1:]:
                den = den + p.sum(axis=-1, keepdims=True)
            acc = jnp.dot(ps[0].astype(BF16), v_refs[0][:, cols], preferred_element_type=F32)
            for p, v in zip(ps[1:], v_refs[1:]):
                acc = acc + jnp.dot(p.astype(BF16), v[:, cols], preferred_element_type=F32)
            outs.append(acc / den)
        o_ref[:, cols] = jnp.where(low_half, outs[0], outs[1]).astype(BF16)


def _attention_ctx(layer, q, kx, vx):
    w = Q_CHUNKS * LANES
    blk = lambda width: pl.BlockSpec((SEQ, width), lambda b: (b, 0))
    return pl.pallas_call(
        functools.partial(_attn_kernel, n_parts=1),
        grid=(BATCH,),
        in_specs=[blk(ATTN_W), blk(w), blk(w)],
        out_specs=blk(ATTN_W),
        out_shape=jax.ShapeDtypeStruct((N_TOK, ATTN_W), BF16),
        input_output_aliases={0: 0},
        compiler_params=_params(("parallel",)),
        name=f"attn_ctx_l{layer}",
    )(q, kx, vx)


def _attention_lat(layer, q, kx, vx, kcache, vcache):
    w = Q_CHUNKS * LANES
    nq = DEC_SEQ // LAT_Q_TILE
    first = N_CTX // LAT_Q_TILE
    qblk = pl.BlockSpec((LAT_Q_TILE, ATTN_W), lambda b, i: (first + b * nq + i, 0))
    new = lambda width: pl.BlockSpec((DEC_SEQ, width), lambda b, i: (N_CTX // DEC_SEQ + b, 0))
    old = lambda width: pl.BlockSpec((None, None, PAST_LEN, width), lambda b, i: (b, layer, 0, 0))
    return pl.pallas_call(
        functools.partial(_attn_kernel, n_parts=2),
        grid=(DEC_BATCH, nq),
        in_specs=[qblk, old(w), new(w), old(w), new(w)],
        out_specs=qblk,
        out_shape=jax.ShapeDtypeStruct((N_TOK, ATTN_W), BF16),
        input_output_aliases={0: 0},
        compiler_params=_params(("parallel", "parallel")),
        name=f"attn_lat_l{layer}",
    )(q, kcache, kx, vcache, vx)


def _merge_kernel(a_ref, fc_ref, fs_ref, cl_ref, sl_ref, ga_ref, gf_ref, x_ref, gate_ref,
                  wa_ref, wf_ref, wo_ref, o_ref):
    attn = jnp.dot(a_ref[...], wa_ref[...], preferred_element_type=F32)
    fo = (jnp.dot(cl_ref[...], fc_ref[...], preferred_element_type=F32)
          - jnp.dot(sl_ref[...], fs_ref[...], preferred_element_type=F32))
    four = jnp.dot(fo.astype(BF16), wf_ref[...], preferred_element_type=F32)
    merged = ga_ref[...].astype(F32) * attn + gf_ref[...].astype(F32) * four
    mix = jnp.dot(merged.astype(BF16), wo_ref[...], preferred_element_type=F32)
    o_ref[...] = x_ref[...] + gate_ref[...] * mix


def _merge(layer, stream, a, fcs, cl, sl, ga, gf, x, mods, wa, wf, wo):
    if stream == "ctx":
        seq, n_batch, first_row, rows = SEQ, BATCH, 0, SEQ
    else:
        seq, n_batch, first_row, rows = DEC_SEQ, DEC_BATCH, N_CTX, LAT_MERGE_TILE
    nt = seq // rows
    tile = lambda b, i: first_row // rows + b * nt + i
    row = lambda w: pl.BlockSpec((rows, w), lambda b, i: (tile(b, i), 0))
    wspec = lambda a_: pl.BlockSpec((None,) + a_.shape[1:], lambda b, i: (layer, 0, 0))
    in_specs = [row(ATTN_W),
                pl.BlockSpec((seq, FOURIER_W), lambda b, i: (first_row // seq + b, 0)),
                pl.BlockSpec((seq, FOURIER_W), lambda b, i: (first_row // seq + b, 1)),
                pl.BlockSpec((rows, seq), lambda b, i: (i, 0)),
                pl.BlockSpec((rows, seq), lambda b, i: (i, 0)),
                row(D_MODEL), row(D_MODEL), row(D_MODEL),
                pl.BlockSpec((None, 1, D_MODEL),
                             lambda b, i: ((layer * N_GROUPS + _tile_group(tile(b, i), rows)) * 6 + 2, 0, 0)),
                wspec(wa), wspec(wf), wspec(wo)]
    args = [a, fcs, fcs, cl, sl, ga, gf, x, mods, wa, wf, wo]
    aliases = {7: 0}
    return pl.pallas_call(
        _merge_kernel,
        grid=(n_batch, nt),
        in_specs=in_specs,
        out_specs=row(D_MODEL),
        out_shape=jax.ShapeDtypeStruct((N_TOK, D_MODEL), F32),
        input_output_aliases=aliases,
        compiler_params=_params(("parallel", "parallel")),
        name=f"merge_{stream}_l{layer}",
    )(*args)


def _router_kernel(x_ref, g_ref, sh_ref, sc_ref, wr_ref, br_ref, tri_ref,
                   h_ref, te_ref, tg_ref, rk_ref, cnt_ref, carry):
    @pl.when(pl.program_id(0) == 0)
    def _():
        carry[...] = jnp.zeros_like(carry)

    x = x_ref[...]
    y = x * lax.rsqrt(jnp.mean(x * x, axis=-1, keepdims=True) + NORM_EPS) * g_ref[...]
    h = y * (1.0 + sc_ref[...]) + sh_ref[...]
    _store_token_rows(h_ref, h)
    logits = lax.dot_general(wr_ref[...], h, _NT, preferred_element_type=F32,
                             precision=lax.Precision.HIGHEST) + br_ref[...]
    sub = lax.broadcasted_iota(I32, logits.shape, 0).astype(F32)
    vals, ids = [], []
    for _ in range(TOP_K):
        m = logits.max(axis=0, keepdims=True)
        idx = jnp.where(logits == m, sub, float(N_EXPERTS)).min(axis=0, keepdims=True)
        vals.append(m)
        ids.append(idx)
        logits = jnp.where(sub == idx, NEG_BIG, logits)
    es = [jnp.exp(v - vals[0]) for v in vals]
    den = es[0] + es[1] + es[2] + es[3]

    chosen = [sub == idx for idx in ids]
    member = jnp.zeros_like(sub)
    for ch in chosen:
        member = member + ch.astype(F32)
    upto = jnp.dot(member.astype(BF16), tri_ref[...], preferred_element_type=F32)
    before = carry[...] + upto - member
    for k in range(TOP_K):
        te_ref[k:k + 1, :] = ids[k].astype(I32)
        tg_ref[k:k + 1, :] = es[k] / den
        rk_ref[k:k + 1, :] = jnp.where(chosen[k], before, 0.0).sum(axis=0, keepdims=True).astype(I32)
    carry[...] = carry[...] + member.sum(axis=1, keepdims=True)
    cnt_ref[...] = carry[:, :LANES].astype(I32)


def _router(layer, x1, norm_g, mods, wr_t, br_t, tri):
    row = lambda w: pl.BlockSpec((ROUTER_TILE, w), lambda i: (i, 0))
    col = pl.BlockSpec((TOP_K, ROUTER_TILE), lambda i: (0, i))
    return pl.pallas_call(
        _router_kernel,
        grid=(N_TOK // ROUTER_TILE,),
        in_specs=[row(D_MODEL), _layer_vec_spec(layer), _mod_spec(layer, 3, ROUTER_TILE), _mod_spec(layer, 4, ROUTER_TILE),
                  pl.BlockSpec((None, N_EXPERTS, D_MODEL), lambda i: (layer, 0, 0)),
                  pl.BlockSpec((None, N_EXPERTS, 1), lambda i: (layer, 0, 0)),
                  pl.BlockSpec((ROUTER_TILE, ROUTER_TILE), lambda i: (0, 0))],
        out_specs=[pl.BlockSpec((ROUTER_TILE * ROW_SUBLANES, LANES), lambda i: (i, 0)), col, col, col,
                   pl.BlockSpec((N_EXPERTS, LANES), lambda i: (0, 0))],
        out_shape=[jax.ShapeDtypeStruct((N_TOK * ROW_SUBLANES, LANES), F32),
                   jax.ShapeDtypeStruct((TOP_K, N_TOK), I32),
                   jax.ShapeDtypeStruct((TOP_K, N_TOK), F32),
                   jax.ShapeDtypeStruct((TOP_K, N_TOK), I32),
                   jax.ShapeDtypeStruct((N_EXPERTS, LANES), I32)],
        scratch_shapes=[pltpu.VMEM((N_EXPERTS, ROUTER_TILE), F32)],
        compiler_params=_params(("arbitrary",)),
        name=f"router_l{layer}",
    )(x1, norm_g, mods, mods, wr_t, br_t, tri)


def _dispatch_kernel(dest_ref, zblk_ref, nblk_ref, h_ref, xs_hbm, zeros, sem):
    i = pl.program_id(0)
    block_rows = EXPERT_TILE * ROW_SUBLANES

    def zero_fill(block):
        rows = pl.ds(pl.multiple_of(block * block_rows, block_rows), block_rows)
        return pltpu.make_async_copy(zeros, xs_hbm.at[rows, :], sem)

    @pl.when(i == 0)
    def _():
        zeros[...] = jnp.zeros_like(zeros)
        for wait in (False, True):
            for e in range(N_EXPERTS):
                @pl.when(zblk_ref[e] >= 0)
                def _():
                    cp = zero_fill(zblk_ref[e])
                    cp.wait() if wait else cp.start()

            def tail(j, carry):
                cp = zero_fill(j)
                cp.wait() if wait else cp.start()
                return carry

            lax.fori_loop(nblk_ref[0], N_EXPERT_BLOCKS, tail, 0)

    def row_copies(chunk):
        copies = []
        for j in range(DMA_CHUNK):
            r = chunk * DMA_CHUNK + j
            for k in range(TOP_K):
                d = dest_ref[k * N_TOK + i * ROW_TILE + r]
                copies.append(pltpu.make_async_copy(h_ref.at[_token_rows(r), :],
                                                    xs_hbm.at[_token_rows(d), :], sem))
        return copies

    def start(chunk, carry):
        for n, cp in enumerate(row_copies(chunk)):
            cp.start(priority=n % 2)
        return carry

    def wait(chunk, carry):
        for cp in row_copies(chunk):
            cp.wait()
        return carry

    lax.fori_loop(0, ROW_TILE // DMA_CHUNK, start, 0)
    lax.fori_loop(0, ROW_TILE // DMA_CHUNK, wait, 0)


def _dispatch(layer, dest, zblk, nblk, h2):
    grid_spec = pltpu.PrefetchScalarGridSpec(
        num_scalar_prefetch=3,
        grid=(N_TILES,),
        in_specs=[pl.BlockSpec((ROW_TILE * ROW_SUBLANES, LANES), lambda i, d, z, n: (i, 0))],
        out_specs=pl.BlockSpec(memory_space=pl.ANY),
        scratch_shapes=[pltpu.VMEM((EXPERT_TILE * ROW_SUBLANES, LANES), F32), pltpu.SemaphoreType.DMA(())],
    )
    return pl.pallas_call(
        _dispatch_kernel,
        grid_spec=grid_spec,
        out_shape=jax.ShapeDtypeStruct((M_PAD * ROW_SUBLANES, LANES), F32),
        compiler_params=_params(("arbitrary",)),
        name=f"dispatch_l{layer}",
    )(dest, zblk, nblk, h2)


def _expert_kernel(be_ref, kind_ref, slot_ref, next_ref, x_ref, wi_hbm, bi_ref, wo_hbm, bo_ref, y_ref,
                   wi_f32, wo_f32, wi_bf, wo_bf, sems, *, layer):
    b = pl.program_id(0)

    def weight_copies(expert, slot):
        return (pltpu.make_async_copy(wi_hbm.at[layer, expert], wi_f32.at[slot], sems.at[0, slot]),
                pltpu.make_async_copy(wo_hbm.at[layer, expert], wo_f32.at[slot], sems.at[1, slot]))

    @pl.when(kind_ref[b] == 0)
    def _():
        y_ref[...] = jnp.zeros_like(y_ref)

    @pl.when(kind_ref[b] == 2)
    def _():
        slot = slot_ref[b]
        own = weight_copies(be_ref[b], slot)

        @pl.when(b == 0)
        def _():
            for cp in own:
                cp.start()

        for cp in own:
            cp.wait()

        @pl.when(next_ref[b] >= 0)
        def _():
            for cp in weight_copies(next_ref[b], 1 - slot):
                cp.start(priority=1)

        wi_bf[...] = wi_f32[slot].astype(BF16)
        wo_bf[...] = wo_f32[slot].astype(BF16)

    @pl.when(kind_ref[b] > 0)
    def _():
        x = _load_token_rows(x_ref).astype(BF16)
        hdn = jnp.dot(x, wi_bf[...], preferred_element_type=F32) + bi_ref[...]
        glu = jnp.minimum(hdn[:, :D_EXPERT], SWIGLU_LIMIT)
        lin = jnp.clip(hdn[:, D_EXPERT:], -SWIGLU_LIMIT, SWIGLU_LIMIT)
        act = glu * jax.nn.sigmoid(SWIGLU_ALPHA * glu) * (lin + 1.0)
        y = jnp.dot(act.astype(BF16), wo_bf[...], preferred_element_type=F32) + bo_ref[...]
        _store_token_rows(y_ref, y)


def _experts(layer, blk_e, blk_kind, blk_slot, blk_next, xs, w_exp_in, b_exp_in, w_exp_out, b_exp_out):
    bias = lambda cols: pl.BlockSpec((None, None, 1, cols), lambda b, be, *_: (layer, be[b], 0, 0))
    blk = pl.BlockSpec((EXPERT_TILE * ROW_SUBLANES, LANES), lambda b, *_: (b, 0))
    hbm = pl.BlockSpec(memory_space=pl.ANY)
    grid_spec = pltpu.PrefetchScalarGridSpec(
        num_scalar_prefetch=4,
        grid=(N_EXPERT_BLOCKS,),
        in_specs=[blk, hbm, bias(2 * D_EXPERT), hbm, bias(D_MODEL)],
        out_specs=blk,
        scratch_shapes=[pltpu.VMEM((2, D_MODEL, 2 * D_EXPERT), F32),
                        pltpu.VMEM((2, D_EXPERT, D_MODEL), F32),
                        pltpu.VMEM((D_MODEL, 2 * D_EXPERT), BF16),
                        pltpu.VMEM((D_EXPERT, D_MODEL), BF16),
                        pltpu.SemaphoreType.DMA((2, 2))],
    )
    return pl.pallas_call(
        functools.partial(_expert_kernel, layer=layer),
        grid_spec=grid_spec,
        out_shape=jax.ShapeDtypeStruct((M_PAD * ROW_SUBLANES, LANES), F32),
        compiler_params=_params(("arbitrary",)),
        name=f"experts_l{layer}",
    )(blk_e, blk_kind, blk_slot, blk_next, xs, w_exp_in,
      b_exp_in.reshape(DEPTH, N_EXPERTS, 1, 2 * D_EXPERT), w_exp_out,
      b_exp_out.reshape(DEPTH, N_EXPERTS, 1, D_MODEL))


def _combine_kernel(dest_ref, y_hbm, x_ref, tg_ref, gate_ref, *rest, final):
    if final:
        fg_ref, o_ref, n_ref, buf, sems = rest
    else:
        o_ref, buf, sems = rest
    i = pl.program_id(0)
    slot = i % 2

    def row_copies(tile, slot, chunk):
        copies = []
        for j in range(DMA_CHUNK):
            r = chunk * DMA_CHUNK + j
            for k in range(TOP_K):
                src = y_hbm.at[_token_rows(dest_ref[k * N_TOK + tile * ROW_TILE + r]), :]
                copies.append(pltpu.make_async_copy(src, buf.at[slot, k, _token_rows(r), :], sems.at[slot]))
        return copies

    def start_tile(tile, slot):
        def body(chunk, carry):
            for n, cp in enumerate(row_copies(tile, slot, chunk)):
                cp.start(priority=n % 2)
            return carry
        lax.fori_loop(0, ROW_TILE // DMA_CHUNK, body, 0)

    def wait_tile(tile, slot):
        def body(chunk, carry):
            for cp in row_copies(tile, slot, chunk):
                cp.wait()
            return carry
        lax.fori_loop(0, ROW_TILE // DMA_CHUNK, body, 0)

    @pl.when(i == 0)
    def _():
        start_tile(0, 0)

    @pl.when(i + 1 < pl.num_programs(0))
    def _():
        start_tile(i + 1, 1 - slot)

    wait_tile(i, slot)

    tg = tg_ref[...]
    acc = tg[:, 0:1] * _load_token_rows(buf.at[slot, 0])
    for k in range(1, TOP_K):
        acc = acc + tg[:, k:k + 1] * _load_token_rows(buf.at[slot, k])
    out = x_ref[...] + gate_ref[...] * acc
    o_ref[...] = out
    if final:
        n_ref[...] = out * lax.rsqrt(jnp.mean(out * out, axis=-1, keepdims=True) + NORM_EPS) * fg_ref[...]


def _combine(layer, dest, yb, x1, tg, mods, final_g=None):
    final = final_g is not None
    row = lambda w: pl.BlockSpec((ROW_TILE, w), lambda i, d: (i, 0))
    in_specs = [pl.BlockSpec(memory_space=pl.ANY), row(D_MODEL), row(TOP_K),
                pl.BlockSpec((None, 1, D_MODEL),
                             lambda i, d: ((layer * N_GROUPS + _tile_group(i)) * 6 + 5, 0, 0))]
    args = [dest, yb, x1, tg, mods]
    out_specs = [row(D_MODEL)]
    out_shape = [jax.ShapeDtypeStruct((N_TOK, D_MODEL), F32)]
    if final:
        in_specs.append(pl.BlockSpec((1, D_MODEL), lambda i, d: (0, 0)))
        args.append(final_g)
        out_specs.append(row(D_MODEL))
        out_shape.append(jax.ShapeDtypeStruct((N_TOK, D_MODEL), F32))
    grid_spec = pltpu.PrefetchScalarGridSpec(
        num_scalar_prefetch=1,
        grid=(N_TILES,),
        in_specs=in_specs,
        out_specs=out_specs,
        scratch_shapes=[pltpu.VMEM((2, TOP_K, ROW_TILE * ROW_SUBLANES, LANES), F32),
                        pltpu.SemaphoreType.DMA((2,))],
    )
    return pl.pallas_call(
        functools.partial(_combine_kernel, final=final),
        grid_spec=grid_spec,
        out_shape=out_shape,
        compiler_params=_params(("arbitrary",)),
        name=f"combine_l{layer}",
    )(*args)


def _rope_tables():
    rows = DEC_SEQ // GRID_W
    row = jnp.repeat(jnp.arange(rows, dtype=F32), GRID_W)
    col = jnp.tile(jnp.arange(GRID_W, dtype=F32), rows)
    inv_freq = ROPE_THETA ** (-jnp.arange(0, ROPE_AXIS_DIM, 2, dtype=F32) / ROPE_AXIS_DIM)
    ang = jnp.stack([row[:, None] * inv_freq, col[:, None] * inv_freq], axis=1)
    cos, sin = jnp.cos(ang), jnp.sin(ang)
    cos_h = jnp.concatenate([cos, cos], axis=-1).reshape(DEC_SEQ, HEAD_DIM)
    sin_h = jnp.concatenate([-sin, sin], axis=-1).reshape(DEC_SEQ, HEAD_DIM)
    reps = LANES // HEAD_DIM
    cos_t = jnp.concatenate([jnp.ones((IN_TILE, LANES), F32), jnp.tile(cos_h, (1, reps))], axis=0)
    sin_t = jnp.concatenate([jnp.zeros((IN_TILE, LANES), F32), jnp.tile(sin_h, (1, reps))], axis=0)
    return cos_t, sin_t


def _dft_tables(n, scale):
    col = jnp.arange(n, dtype=I32)

    def direct(rows):
        ang = ((rows[:, None] * col[None, :]) % n).astype(F32) * (2.0 * math.pi / n)
        return jnp.cos(ang), jnp.sin(ang)

    if n <= DFT_ROW_BLOCK:
        c, s = direct(col)
        return c * scale, s * scale
    lo_c, lo_s = direct(jnp.arange(DFT_ROW_BLOCK, dtype=I32))
    hi_c, hi_s = direct(jnp.arange(n // DFT_ROW_BLOCK, dtype=I32) * DFT_ROW_BLOCK)
    hi_c, hi_s = (hi_c * scale)[:, None, :], (hi_s * scale)[:, None, :]
    c = hi_c * lo_c[None] - hi_s * lo_s[None]
    s = hi_s * lo_c[None] + hi_c * lo_s[None]
    return c.reshape(n, n), s.reshape(n, n)


def _channel_dft():
    c, s = _dft_tables(FOURIER_GROUP_W, 1.0)
    eye = jnp.eye(N_FOURIER_GROUPS, dtype=F32)
    return jnp.concatenate([jnp.kron(eye, c), jnp.kron(eye, s)], axis=1).astype(BF16)


def _extend_heads(t):
    h = [t[..., i * HEAD_DIM:(i + 1) * HEAD_DIM] for i in range(N_KV_HEADS)]
    order = [0, 0, 0, 1, 1, 1, 2, 2, 2, 3, 3, 3]
    return jnp.concatenate([h[i] for i in order], axis=-1)


def _plan(top_e_t, rank_t, counts):
    cnt = counts[:N_EXPERTS, 0]
    padded = (cnt + EXPERT_TILE - 1) // EXPERT_TILE * EXPERT_TILE
    pad_end = jnp.cumsum(padded)
    pad_start = pad_end - padded
    experts = jnp.arange(N_EXPERTS, dtype=I32)
    start_of = jnp.sum(jnp.where(top_e_t[..., None] == experts, pad_start, 0), axis=-1)
    dest = (start_of + rank_t).reshape(-1).astype(I32)
    blk_start = jnp.arange(N_EXPERT_BLOCKS, dtype=I32) * EXPERT_TILE
    blk_e = jnp.minimum(jnp.sum(pad_end[None, :] <= blk_start[:, None], axis=1), N_EXPERTS - 1).astype(I32)
    valid = blk_start < pad_end[-1]
    first = jnp.logical_and(valid, jnp.concatenate([jnp.ones((1,), bool), blk_e[1:] != blk_e[:-1]]))
    blk_kind = (valid.astype(I32) + first.astype(I32)).astype(I32)
    blk_slot = ((jnp.cumsum(first.astype(I32)) - 1) % 2).astype(I32)
    later = jnp.logical_and(experts[None, :] > experts[:, None], cnt[None, :] > 0)
    next_of = jnp.min(jnp.where(later, experts[None, :], N_EXPERTS), axis=1)
    next_of = jnp.where(next_of == N_EXPERTS, -1, next_of)
    blk_next = jnp.sum(jnp.where(blk_e[:, None] == experts[None, :], next_of[None, :], 0), axis=1).astype(I32)
    zero_blk = jnp.where(cnt % EXPERT_TILE != 0, (pad_start + cnt) // EXPERT_TILE, -1).astype(I32)
    n_blk = (pad_end[-1:] // EXPERT_TILE).astype(I32)
    return dest, (blk_e, blk_kind, blk_slot, blk_next), zero_blk, n_blk


def kernel(x_prompt, x_sample, cache_k, cache_v, c, c_ctx, w_ada, b_ada, norm1_g, w_in, q_norm_g,
           k_norm_g, w_attn_o, w_fourier_o, w_out, norm2_g, w_router, b_router, w_exp_in, b_exp_in,
           w_exp_out, b_exp_out, final_norm_g):
    x = jnp.concatenate([x_prompt.reshape(N_CTX, D_MODEL), x_sample.reshape(N_LAT, D_MODEL)], axis=0)

    cond = jnp.concatenate([c_ctx[None, :], c, jnp.zeros((8 - N_GROUPS, D_MODEL), F32)], axis=0)
    mods = _mods(cond.T, w_ada, b_ada)[:, :N_GROUPS].reshape(DEPTH * N_GROUPS * 6, 1, D_MODEL)

    cos_t, sin_t = _rope_tables()
    bd = jnp.kron(jnp.eye(LANES // HEAD_DIM, dtype=F32),
                  jnp.full((HEAD_DIM, HEAD_DIM), 1.0 / HEAD_DIM, F32)).astype(BF16)
    bd = jnp.concatenate([bd, bd], axis=0)
    cs = _channel_dft()
    dft_ctx = [t.astype(BF16) for t in _dft_tables(SEQ, (SEQ * FOURIER_GROUP_W) ** -0.5)]
    dft_lat = [t.astype(BF16) for t in _dft_tables(DEC_SEQ, (DEC_SEQ * FOURIER_GROUP_W) ** -0.5)]

    w_in_b = w_in.astype(BF16)
    wa_b = w_attn_o.astype(BF16)
    wf_b = w_fourier_o.astype(BF16)
    wo_b = w_out.astype(BF16)
    norm1 = norm1_g.reshape(DEPTH, 1, D_MODEL)
    norm2 = norm2_g.reshape(DEPTH, 1, D_MODEL)
    qg = jnp.tile(q_norm_g, (1, LANES // HEAD_DIM)).reshape(DEPTH, 1, LANES)
    kg = jnp.tile(k_norm_g, (1, LANES // HEAD_DIM)).reshape(DEPTH, 1, LANES)
    wr_t = jnp.swapaxes(w_router, 1, 2)
    br_t = b_router.reshape(DEPTH, N_EXPERTS, 1)
    tok = jnp.arange(ROUTER_TILE, dtype=I32)
    tri = (tok[:, None] <= tok[None, :]).astype(BF16)
    kcache = _extend_heads(cache_k.reshape(DEC_BATCH, DEPTH, PAST_LEN, KV_W)).astype(BF16)
    vcache = _extend_heads(cache_v.reshape(DEC_BATCH, DEPTH, PAST_LEN, KV_W)).astype(BF16)

    keys, vals = [], []
    y_norm = None
    for l in range(DEPTH):
        q, kx, vx, kf, vf, fcs, ga, gf = _inproj(l, x, norm1, mods, w_in_b, qg, kg, cos_t, sin_t, bd, cs)
        keys.append(kf[:N_CTX])
        vals.append(vf[:N_CTX])
        a = _attention_ctx(l, q, kx, vx)
        a = _attention_lat(l, a, kx, vx, kcache, vcache)
        x1 = _merge(l, "ctx", a, fcs, dft_ctx[0], dft_ctx[1], ga, gf, x, mods, wa_b, wf_b, wo_b)
        x1 = _merge(l, "lat", a, fcs, dft_lat[0], dft_lat[1], ga, gf, x1, mods, wa_b, wf_b, wo_b)
        h2, top_e, top_g, rank, counts = _router(l, x1, norm2, mods, wr_t, br_t, tri)
        dest, blk_tables, zero_blk, n_blk = _plan(top_e, rank, counts)
        xs = _dispatch(l, dest, zero_blk, n_blk, h2)
        yb = _experts(l, *blk_tables, xs, w_exp_in, b_exp_in, w_exp_out, b_exp_out)
        top_g = top_g.T
        if l == DEPTH - 1:
            x, y_norm = _combine(l, dest, yb, x1, top_g, mods, final_norm_g.reshape(1, D_MODEL))
        else:
            (x,) = _combine(l, dest, yb, x1, top_g, mods)

    y_prompt = y_norm[:N_CTX].reshape(BATCH, SEQ, D_MODEL)
    y_sample = y_norm[N_CTX:].reshape(DEC_BATCH, DEC_SEQ, D_MODEL)
    shape = (BATCH, SEQ, N_KV_HEADS, HEAD_DIM)
    new_k = jnp.stack([k.reshape(shape) for k in keys], axis=1)
    new_v = jnp.stack([v.reshape(shape) for v in vals], axis=1)
    return (y_prompt, y_sample, new_k, new_v)
```

```python
import functools
import math

import jax
import jax.numpy as jnp
from jax import lax
from jax.experimental import pallas as pl
from jax.experimental.pallas import tpu as pltpu

F32 = jnp.float32
BF16 = jnp.bfloat16
I32 = jnp.int32

D_MODEL = 1024
DEPTH = 4
BATCH = 16
SEQ = 256
DEC_BATCH = 2
DEC_SEQ = 2048
PAST_LEN = 512
GRID_W = 64
HEAD_DIM = 64
N_Q_HEADS = 12
N_KV_HEADS = 4
ATTN_W = N_Q_HEADS * HEAD_DIM
KV_W = N_KV_HEADS * HEAD_DIM
FOURIER_W = D_MODEL // 4
N_FOURIER_GROUPS = 4
FOURIER_GROUP_W = FOURIER_W // N_FOURIER_GROUPS
IN_W = ATTN_W + 2 * KV_W + FOURIER_W + 2 * D_MODEL
ROPE_AXIS_DIM = HEAD_DIM // 2
ROPE_THETA = 10000.0
N_EXPERTS = 32
TOP_K = 4
D_EXPERT = D_MODEL
SWIGLU_ALPHA = 1.702
SWIGLU_LIMIT = 7.0
NORM_EPS = 1e-6

N_CTX = BATCH * SEQ
N_LAT = DEC_BATCH * DEC_SEQ
N_TOK = N_CTX + N_LAT
LANES = 128
ROW_TILE = 256
IN_TILE = 1024
LAT_MERGE_TILE = 512
ROUTER_TILE = 512
LAT_Q_TILE = 256
N_TILES = N_TOK // ROW_TILE
CTX_TILES = N_CTX // ROW_TILE
LAT_TILES_PER_BATCH = DEC_SEQ // ROW_TILE
N_GROUPS = 1 + DEC_BATCH
EXPERT_TILE = 256
N_ASSIGN = N_TOK * TOP_K
N_EXPERT_BLOCKS = N_ASSIGN // EXPERT_TILE + N_EXPERTS
M_PAD = N_EXPERT_BLOCKS * EXPERT_TILE
DMA_CHUNK = 32
DISPATCH_SLOTS = 3
DFT_ROW_BLOCK = 64
Q_CHUNKS = ATTN_W // LANES
NEG_BIG = -1e30
VMEM_LIMIT = 56 * 1024 * 1024

_NT = (((1,), (1,)), ((), ()))


def _params(sem, vmem=VMEM_LIMIT):
    return pltpu.CompilerParams(dimension_semantics=sem, vmem_limit_bytes=vmem)


def _tile_group(i, tile=ROW_TILE):
    ctx_tiles = N_CTX // tile
    return jnp.where(i < ctx_tiles, 0, 1 + (i - ctx_tiles) // (DEC_SEQ // tile))


def _rope_block(i, tile):
    ctx_tiles = N_CTX // tile
    return jnp.where(i < ctx_tiles, 0, 1 + (i - ctx_tiles) % (DEC_SEQ // tile))


def _mod_spec(layer, slot, tile=ROW_TILE):
    return pl.BlockSpec((None, 1, D_MODEL),
                        lambda i: ((layer * N_GROUPS + _tile_group(i, tile)) * 6 + slot, 0, 0))


def _layer_vec_spec(layer, width=D_MODEL):
    return pl.BlockSpec((None, 1, width), lambda *_: (layer, 0, 0))


ROW_SUBLANES = D_MODEL // LANES


def _store_token_rows(ref, value):
    n = value.shape[0]
    for s in range(ROW_SUBLANES):
        ref[pl.ds(s, n, stride=ROW_SUBLANES), :] = value[:, s * LANES:(s + 1) * LANES]


def _load_token_rows(ref):
    n = ref.shape[0] // ROW_SUBLANES
    return jnp.concatenate([ref[pl.ds(s, n, stride=ROW_SUBLANES), :] for s in range(ROW_SUBLANES)], axis=1)


def _token_rows(token):
    return pl.ds(pl.multiple_of(token * ROW_SUBLANES, ROW_SUBLANES), ROW_SUBLANES)


def _mods_kernel(cond_ref, w_ref, b_ref, o_ref):
    c = cond_ref[...]
    s = c * jax.nn.sigmoid(c)
    w = w_ref[...]
    o_ref[...] = jnp.broadcast_to(b_ref[...], o_ref.shape)
    for g in range(N_GROUPS):
        o_ref[g:g + 1, :] = jnp.sum(w * s[:, g:g + 1], axis=0, keepdims=True) + b_ref[...]


def _mods(cond, w_ada, b_ada):
    tn = 1536
    return pl.pallas_call(
        _mods_kernel,
        grid=(DEPTH, 6 * D_MODEL // tn),
        in_specs=[pl.BlockSpec((D_MODEL, 8), lambda l, j: (0, 0)),
                  pl.BlockSpec((None, D_MODEL, tn), lambda l, j: (l, 0, j)),
                  pl.BlockSpec((None, 1, tn), lambda l, j: (l, 0, j))],
        out_specs=pl.BlockSpec((None, 8, tn), lambda l, j: (l, 0, j)),
        out_shape=jax.ShapeDtypeStruct((DEPTH, 8, 6 * D_MODEL), F32),
        compiler_params=_params(("parallel", "parallel")),
        name="adaln_mods",
    )(cond, w_ada, b_ada.reshape(DEPTH, 1, 6 * D_MODEL))


def _inproj_kernel(x_ref, g_ref, sh_ref, sc_ref, w_ref, qg_ref, kg_ref, cos_ref, sin_ref,
                   bd_ref, cs_ref,
                   q_ref, kx_ref, vx_ref, kf_ref, vf_ref, fcs_ref, ga_ref, gf_ref, h_scr):
    x = x_ref[...]
    y = x * lax.rsqrt(jnp.mean(x * x, axis=-1, keepdims=True) + NORM_EPS) * g_ref[...]
    h_scr[...] = (y * (1.0 + sc_ref[...]) + sh_ref[...]).astype(BF16)
    hb = h_scr[...]
    lane = lax.broadcasted_iota(I32, (x_ref.shape[0], LANES), 1)
    low_half = lane < HEAD_DIM
    rot_first = (lane % ROPE_AXIS_DIM) < (ROPE_AXIS_DIM // 2)
    cos = cos_ref[...]
    sin = sin_ref[...]
    bd = bd_ref[...]

    def proj(lo, width):
        return jnp.dot(hb, w_ref[:, lo:lo + width], preferred_element_type=F32)

    def head_norm_rope(p, gain):
        pp = p * p
        hi = pp.astype(BF16)
        lo = (pp - hi.astype(F32)).astype(BF16)
        msq = jnp.dot(jnp.concatenate([hi, lo], axis=1), bd, preferred_element_type=F32)
        n = p * lax.rsqrt(msq + NORM_EPS) * gain
        partner = jnp.where(rot_first, pltpu.roll(n, LANES - ROPE_AXIS_DIM // 2, 1),
                            pltpu.roll(n, ROPE_AXIS_DIM // 2, 1))
        return n * cos + partner * sin

    def head_pairs(r):
        sw = pltpu.roll(r, HEAD_DIM, 1)
        return jnp.where(low_half, r, sw), r, jnp.where(low_half, sw, r)

    scale = HEAD_DIM ** -0.5
    wide = 2 * LANES
    for c2 in range(ATTN_W // wide):
        p = proj(c2 * wide, wide)
        for t in range(2):
            c = 2 * c2 + t
            r = head_norm_rope(p[:, t * LANES:(t + 1) * LANES], qg_ref[...])
            q_ref[:, c * LANES:(c + 1) * LANES] = (r * scale).astype(BF16)

    p = proj(ATTN_W, KV_W)
    for j in range(KV_W // LANES):
        r = head_norm_rope(p[:, j * LANES:(j + 1) * LANES], kg_ref[...])
        kf_ref[:, j * LANES:(j + 1) * LANES] = r
        for t, piece in enumerate(head_pairs(r)):
            c = 3 * j + t
            kx_ref[:, c * LANES:(c + 1) * LANES] = piece.astype(BF16)

    p = proj(ATTN_W + KV_W, KV_W)
    vf_ref[...] = p
    for j in range(KV_W // LANES):
        for t, piece in enumerate(head_pairs(p[:, j * LANES:(j + 1) * LANES])):
            c = 3 * j + t
            vx_ref[:, c * LANES:(c + 1) * LANES] = piece.astype(BF16)

    f = proj(ATTN_W + 2 * KV_W, FOURIER_W)
    fcs_ref[...] = jnp.dot(f.astype(BF16), cs_ref[...], preferred_element_type=F32).astype(BF16)

    gate_lo = ATTN_W + 2 * KV_W + FOURIER_W
    half = D_MODEL // 2
    for j in range(2):
        ga_ref[:, j * half:(j + 1) * half] = jax.nn.sigmoid(proj(gate_lo + j * half, half)).astype(BF16)
        gf_ref[:, j * half:(j + 1) * half] = jax.nn.sigmoid(
            proj(gate_lo + D_MODEL + j * half, half)).astype(BF16)


def _inproj(layer, x, norm_g, mods, w_in, qg, kg, cos_tab, sin_tab, bd, cs):
    row = lambda w: pl.BlockSpec((IN_TILE, w), lambda i: (i, 0))
    const = lambda a: pl.BlockSpec(a.shape, lambda i: (0,) * a.ndim)
    rope = pl.BlockSpec((IN_TILE, LANES), lambda i: (_rope_block(i, IN_TILE), 0))
    ext_w = Q_CHUNKS * LANES
    return pl.pallas_call(
        _inproj_kernel,
        grid=(N_TOK // IN_TILE,),
        in_specs=[row(D_MODEL), _layer_vec_spec(layer), _mod_spec(layer, 0, IN_TILE), _mod_spec(layer, 1, IN_TILE),
                  pl.BlockSpec((None, D_MODEL, IN_W), lambda i: (layer, 0, 0)),
                  _layer_vec_spec(layer, LANES), _layer_vec_spec(layer, LANES), rope, rope,
                  const(bd), const(cs)],
        out_specs=[row(ATTN_W), row(ext_w), row(ext_w), row(KV_W), row(KV_W),
                   row(2 * FOURIER_W), row(D_MODEL), row(D_MODEL)],
        out_shape=[jax.ShapeDtypeStruct((N_TOK, ATTN_W), BF16),
                   jax.ShapeDtypeStruct((N_TOK, ext_w), BF16),
                   jax.ShapeDtypeStruct((N_TOK, ext_w), BF16),
                   jax.ShapeDtypeStruct((N_TOK, KV_W), F32),
                   jax.ShapeDtypeStruct((N_TOK, KV_W), F32),
                   jax.ShapeDtypeStruct((N_TOK, 2 * FOURIER_W), BF16),
                   jax.ShapeDtypeStruct((N_TOK, D_MODEL), BF16),
                   jax.ShapeDtypeStruct((N_TOK, D_MODEL), BF16)],
        scratch_shapes=[pltpu.VMEM((IN_TILE, D_MODEL), BF16)],
        compiler_params=_params(("parallel",)),
        name=f"inproj_l{layer}",
    )(x, norm_g, mods, mods, w_in, qg, kg, cos_tab, sin_tab, bd, cs)


def _attn_kernel(*refs, n_parts):
    q_ref = refs[0]
    k_refs = refs[1:1 + n_parts]
    v_refs = refs[1 + n_parts:1 + 2 * n_parts]
    o_ref = refs[-1]
    tq = q_ref.shape[0]
    low_half = lax.broadcasted_iota(I32, (tq, LANES), 1) < HEAD_DIM
    for c in range(Q_CHUNKS):
        cols = slice(c * LANES, (c + 1) * LANES)
        qc = q_ref[:, cols]
        zero = jnp.zeros_like(qc)
        outs = []
        for mask in (low_half, jnp.logical_not(low_half)):
            qm = jnp.where(mask, qc, zero)
            ss = [lax.dot_general(qm, k[:, cols], _NT, preferred_element_type=F32) for k in k_refs]
            m = ss[0].max(axis=-1, keepdims=True)
            for s in ss[1:]:
                m = jnp.maximum(m, s.max(axis=-1, keepdims=True))
            ps = [jnp.exp(s - m) for s in ss]
            den = ps[0].sum(axis=-1, keepdims=True)
            for p in ps[1:]:
                den = den + p.sum(axis=-1, keepdims=True)
            acc = jnp.dot(ps[0].astype(BF16), v_refs[0][:, cols], preferred_element_type=F32)
            for p, v in zip(ps[1:], v_refs[1:]):
                acc = acc + jnp.dot(p.astype(BF16), v[:, cols], preferred_element_type=F32)
            outs.append(acc / den)
        o_ref[:, cols] = jnp.where(low_half, outs[0], outs[1]).astype(BF16)


def _attention_ctx(layer, q, kx, vx):
    w = Q_CHUNKS * LANES
    blk = lambda width: pl.BlockSpec((SEQ, width), lambda b: (b, 0))
    return pl.pallas_call(
        functools.partial(_attn_kernel, n_parts=1),
        grid=(BATCH,),
        in_specs=[blk(ATTN_W), blk(w), blk(w)],
        out_specs=blk(ATTN_W),
        out_shape=jax.ShapeDtypeStruct((N_TOK, ATTN_W), BF16),
        input_output_aliases={0: 0},
        compiler_params=_params(("parallel",)),
        name=f"attn_ctx_l{layer}",
    )(q, kx, vx)


def _attention_lat(layer, q, kx, vx, kcache, vcache):
    w = Q_CHUNKS * LANES
    nq = DEC_SEQ // LAT_Q_TILE
    first = N_CTX // LAT_Q_TILE
    qblk = pl.BlockSpec((LAT_Q_TILE, ATTN_W), lambda b, i: (first + b * nq + i, 0))
    new = lambda width: pl.BlockSpec((DEC_SEQ, width), lambda b, i: (N_CTX // DEC_SEQ + b, 0))
    old = lambda width: pl.BlockSpec((None, None, PAST_LEN, width), lambda b, i: (b, layer, 0, 0))
    return pl.pallas_call(
        functools.partial(_attn_kernel, n_parts=2),
        grid=(DEC_BATCH, nq),
        in_specs=[qblk, old(w), new(w), old(w), new(w)],
        out_specs=qblk,
        out_shape=jax.ShapeDtypeStruct((N_TOK, ATTN_W), BF16),
        input_output_aliases={0: 0},
        compiler_params=_params(("parallel", "parallel")),
        name=f"attn_lat_l{layer}",
    )(q, kcache, kx, vcache, vx)


def _merge_kernel(a_ref, fc_ref, fs_ref, cl_ref, sl_ref, ga_ref, gf_ref, x_ref, gate_ref,
                  wa_ref, wf_ref, wo_ref, o_ref):
    attn = jnp.dot(a_ref[...], wa_ref[...], preferred_element_type=F32)
    fo = (jnp.dot(cl_ref[...], fc_ref[...], preferred_element_type=F32)
          - jnp.dot(sl_ref[...], fs_ref[...], preferred_element_type=F32))
    four = jnp.dot(fo.astype(BF16), wf_ref[...], preferred_element_type=F32)
    merged = ga_ref[...].astype(F32) * attn + gf_ref[...].astype(F32) * four
    mix = jnp.dot(merged.astype(BF16), wo_ref[...], preferred_element_type=F32)
    o_ref[...] = x_ref[...] + gate_ref[...] * mix


def _merge(layer, stream, a, fcs, cl, sl, ga, gf, x, mods, wa, wf, wo):
    if stream == "ctx":
        seq, n_batch, first_row, rows = SEQ, BATCH, 0, SEQ
    else:
        seq, n_batch, first_row, rows = DEC_SEQ, DEC_BATCH, N_CTX, LAT_MERGE_TILE
    nt = seq // rows
    tile = lambda b, i: first_row // rows + b * nt + i
    row = lambda w: pl.BlockSpec((rows, w), lambda b, i: (tile(b, i), 0))
    wspec = lambda a_: pl.BlockSpec((None,) + a_.shape[1:], lambda b, i: (layer, 0, 0))
    in_specs = [row(ATTN_W),
                pl.BlockSpec((seq, FOURIER_W), lambda b, i: (first_row // seq + b, 0)),
                pl.BlockSpec((seq, FOURIER_W), lambda b, i: (first_row // seq + b, 1)),
                pl.BlockSpec((rows, seq), lambda b, i: (i, 0)),
                pl.BlockSpec((rows, seq), lambda b, i: (i, 0)),
                row(D_MODEL), row(D_MODEL), row(D_MODEL),
                pl.BlockSpec((None, 1, D_MODEL),
                             lambda b, i: ((layer * N_GROUPS + _tile_group(tile(b, i), rows)) * 6 + 2, 0, 0)),
                wspec(wa), wspec(wf), wspec(wo)]
    args = [a, fcs, fcs, cl, sl, ga, gf, x, mods, wa, wf, wo]
    aliases = {7: 0}
    return pl.pallas_call(
        _merge_kernel,
        grid=(n_batch, nt),
        in_specs=in_specs,
        out_specs=row(D_MODEL),
        out_shape=jax.ShapeDtypeStruct((N_TOK, D_MODEL), F32),
        input_output_aliases=aliases,
        compiler_params=_params(("parallel", "parallel")),
        name=f"merge_{stream}_l{layer}",
    )(*args)


def _router_kernel(x_ref, g_ref, sh_ref, sc_ref, wr_ref, br_ref, tri_ref,
                   h_ref, te_ref, tg_ref, rk_ref, cnt_ref, carry):
    @pl.when(pl.program_id(0) == 0)
    def _():
        carry[...] = jnp.zeros_like(carry)

    x = x_ref[...]
    y = x * lax.rsqrt(jnp.mean(x * x, axis=-1, keepdims=True) + NORM_EPS) * g_ref[...]
    h = y * (1.0 + sc_ref[...]) + sh_ref[...]
    _store_token_rows(h_ref, h)
    logits = lax.dot_general(wr_ref[...], h, _NT, preferred_element_type=F32,
                             precision=lax.Precision.HIGHEST) + br_ref[...]
    sub = lax.broadcasted_iota(I32, logits.shape, 0).astype(F32)
    vals, ids = [], []
    for _ in range(TOP_K):
        m = logits.max(axis=0, keepdims=True)
        idx = jnp.where(logits == m, sub, float(N_EXPERTS)).min(axis=0, keepdims=True)
        vals.append(m)
        ids.append(idx)
        logits = jnp.where(sub == idx, NEG_BIG, logits)
    es = [jnp.exp(v - vals[0]) for v in vals]
    den = es[0] + es[1] + es[2] + es[3]

    chosen = [sub == idx for idx in ids]
    member = jnp.zeros_like(sub)
    for ch in chosen:
        member = member + ch.astype(F32)
    upto = jnp.dot(member.astype(BF16), tri_ref[...], preferred_element_type=F32)
    before = carry[...] + upto - member
    for k in range(TOP_K):
        te_ref[k:k + 1, :] = ids[k].astype(I32)
        tg_ref[k:k + 1, :] = es[k] / den
        rk_ref[k:k + 1, :] = jnp.where(chosen[k], before, 0.0).sum(axis=0, keepdims=True).astype(I32)
    carry[...] = carry[...] + member.sum(axis=1, keepdims=True)
    cnt_ref[...] = carry[:, :LANES].astype(I32)


def _router(layer, x1, norm_g, mods, wr_t, br_t, tri):
    row = lambda w: pl.BlockSpec((ROUTER_TILE, w), lambda i: (i, 0))
    col = pl.BlockSpec((TOP_K, ROUTER_TILE), lambda i: (0, i))
    return pl.pallas_call(
        _router_kernel,
        grid=(N_TOK // ROUTER_TILE,),
        in_specs=[row(D_MODEL), _layer_vec_spec(layer), _mod_spec(layer, 3, ROUTER_TILE), _mod_spec(layer, 4, ROUTER_TILE),
                  pl.BlockSpec((None, N_EXPERTS, D_MODEL), lambda i: (layer, 0, 0)),
                  pl.BlockSpec((None, N_EXPERTS, 1), lambda i: (layer, 0, 0)),
                  pl.BlockSpec((ROUTER_TILE, ROUTER_TILE), lambda i: (0, 0))],
        out_specs=[pl.BlockSpec((ROUTER_TILE * ROW_SUBLANES, LANES), lambda i: (i, 0)), col, col, col,
                   pl.BlockSpec((N_EXPERTS, LANES), lambda i: (0, 0))],
        out_shape=[jax.ShapeDtypeStruct((N_TOK * ROW_SUBLANES, LANES), F32),
                   jax.ShapeDtypeStruct((TOP_K, N_TOK), I32),
                   jax.ShapeDtypeStruct((TOP_K, N_TOK), F32),
                   jax.ShapeDtypeStruct((TOP_K, N_TOK), I32),
                   jax.ShapeDtypeStruct((N_EXPERTS, LANES), I32)],
        scratch_shapes=[pltpu.VMEM((N_EXPERTS, ROUTER_TILE), F32)],
        compiler_params=_params(("arbitrary",)),
        name=f"router_l{layer}",
    )(x1, norm_g, mods, mods, wr_t, br_t, tri)


def _dispatch_kernel(dest_ref, zblk_ref, nblk_ref, h_hbm, xs_hbm, zeros, hbuf, zsem, hsems, csems):
    i = pl.program_id(0)
    n_steps = pl.num_programs(0)
    block_rows = EXPERT_TILE * ROW_SUBLANES
    tile_rows = ROW_TILE * ROW_SUBLANES

    def zero_fill(block):
        rows = pl.ds(pl.multiple_of(block * block_rows, block_rows), block_rows)
        return pltpu.make_async_copy(zeros, xs_hbm.at[rows, :], zsem)

    def fetch(tile):
        rows = pl.ds(pl.multiple_of(tile * tile_rows, tile_rows), tile_rows)
        slot = tile % DISPATCH_SLOTS
        return pltpu.make_async_copy(h_hbm.at[rows, :], hbuf.at[slot], hsems.at[slot])

    def row_copies(tile, chunk):
        slot = tile % DISPATCH_SLOTS
        copies = []
        for j in range(DMA_CHUNK):
            r = chunk * DMA_CHUNK + j
            for k in range(TOP_K):
                d = dest_ref[k * N_TOK + tile * ROW_TILE + r]
                copies.append(pltpu.make_async_copy(hbuf.at[slot, _token_rows(r), :],
                                                    xs_hbm.at[_token_rows(d), :], csems.at[slot]))
        return copies

    def start_tile(tile):
        def body(chunk, carry):
            for n, cp in enumerate(row_copies(tile, chunk)):
                cp.start(priority=n % 2)
            return carry
        lax.fori_loop(0, ROW_TILE // DMA_CHUNK, body, 0)

    def wait_tile(tile):
        def body(chunk, carry):
            for cp in row_copies(tile, chunk):
                cp.wait()
            return carry
        lax.fori_loop(0, ROW_TILE // DMA_CHUNK, body, 0)

    @pl.when(i == 0)
    def _():
        fetch(0).start()
        fetch(1).start()
        zeros[...] = jnp.zeros_like(zeros)
        for wait in (False, True):
            for e in range(N_EXPERTS):
                @pl.when(zblk_ref[e] >= 0)
                def _():
                    cp = zero_fill(zblk_ref[e])
                    cp.wait() if wait else cp.start()

            def tail(j, carry):
                cp = zero_fill(j)
                cp.wait() if wait else cp.start()
                return carry

            lax.fori_loop(nblk_ref[0], N_EXPERT_BLOCKS, tail, 0)

    fetch(i).wait()
    start_tile(i)

    @pl.when(i >= 1)
    def _():
        wait_tile(i - 1)

    @pl.when(i + 2 < n_steps)
    def _():
        fetch(i + 2).start()

    @pl.when(i == n_steps - 1)
    def _():
        wait_tile(i)


def _dispatch(layer, dest, zblk, nblk, h2):
    grid_spec = pltpu.PrefetchScalarGridSpec(
        num_scalar_prefetch=3,
        grid=(N_TILES,),
        in_specs=[pl.BlockSpec(memory_space=pl.ANY)],
        out_specs=pl.BlockSpec(memory_space=pl.ANY),
        scratch_shapes=[pltpu.VMEM((EXPERT_TILE * ROW_SUBLANES, LANES), F32),
                        pltpu.VMEM((DISPATCH_SLOTS, ROW_TILE * ROW_SUBLANES, LANES), F32),
                        pltpu.SemaphoreType.DMA(()),
                        pltpu.SemaphoreType.DMA((DISPATCH_SLOTS,)),
                        pltpu.SemaphoreType.DMA((DISPATCH_SLOTS,))],
    )
    return pl.pallas_call(
        _dispatch_kernel,
        grid_spec=grid_spec,
        out_shape=jax.ShapeDtypeStruct((M_PAD * ROW_SUBLANES, LANES), F32),
        compiler_params=_params(("arbitrary",)),
        name=f"dispatch_l{layer}",
    )(dest, zblk, nblk, h2)


def _expert_kernel(be_ref, kind_ref, slot_ref, next_ref, x_ref, wi_hbm, bi_ref, wo_hbm, bo_ref, y_ref,
                   wi_f32, wo_f32, wi_bf, wo_bf, sems, *, layer):
    b = pl.program_id(0)

    def weight_copies(expert, slot):
        return (pltpu.make_async_copy(wi_hbm.at[layer, expert], wi_f32.at[slot], sems.at[0, slot]),
                pltpu.make_async_copy(wo_hbm.at[layer, expert], wo_f32.at[slot], sems.at[1, slot]))

    @pl.when(kind_ref[b] == 0)
    def _():
        y_ref[...] = jnp.zeros_like(y_ref)

    @pl.when(kind_ref[b] == 2)
    def _():
        slot = slot_ref[b]
        own = weight_copies(be_ref[b], slot)

        @pl.when(b == 0)
        def _():
            for cp in own:
                cp.start()

        for cp in own:
            cp.wait()

        @pl.when(next_ref[b] >= 0)
        def _():
            for cp in weight_copies(next_ref[b], 1 - slot):
                cp.start(priority=1)

        wi_bf[...] = wi_f32[slot].astype(BF16)
        wo_bf[...] = wo_f32[slot].astype(BF16)

    @pl.when(kind_ref[b] > 0)
    def _():
        x = _load_token_rows(x_ref).astype(BF16)
        hdn = jnp.dot(x, wi_bf[...], preferred_element_type=F32) + bi_ref[...]
        glu = jnp.minimum(hdn[:, :D_EXPERT], SWIGLU_LIMIT)
        lin = jnp.clip(hdn[:, D_EXPERT:], -SWIGLU_LIMIT, SWIGLU_LIMIT)
        act = glu * jax.nn.sigmoid(SWIGLU_ALPHA * glu) * (lin + 1.0)
        y = jnp.dot(act.astype(BF16), wo_bf[...], preferred_element_type=F32) + bo_ref[...]
        _store_token_rows(y_ref, y)


def _experts(layer, blk_e, blk_kind, blk_slot, blk_next, xs, w_exp_in, b_exp_in, w_exp_out, b_exp_out):
    bias = lambda cols: pl.BlockSpec((None, None, 1, cols), lambda b, be, *_: (layer, be[b], 0, 0))
    blk = pl.BlockSpec((EXPERT_TILE * ROW_SUBLANES, LANES), lambda b, *_: (b, 0))
    hbm = pl.BlockSpec(memory_space=pl.ANY)
    grid_spec = pltpu.PrefetchScalarGridSpec(
        num_scalar_prefetch=4,
        grid=(N_EXPERT_BLOCKS,),
        in_specs=[blk, hbm, bias(2 * D_EXPERT), hbm, bias(D_MODEL)],
        out_specs=blk,
        scratch_shapes=[pltpu.VMEM((2, D_MODEL, 2 * D_EXPERT), F32),
                        pltpu.VMEM((2, D_EXPERT, D_MODEL), F32),
                        pltpu.VMEM((D_MODEL, 2 * D_EXPERT), BF16),
                        pltpu.VMEM((D_EXPERT, D_MODEL), BF16),
                        pltpu.SemaphoreType.DMA((2, 2))],
    )
    return pl.pallas_call(
        functools.partial(_expert_kernel, layer=layer),
        grid_spec=grid_spec,
        out_shape=jax.ShapeDtypeStruct((M_PAD * ROW_SUBLANES, LANES), F32),
        compiler_params=_params(("arbitrary",)),
        name=f"experts_l{layer}",
    )(blk_e, blk_kind, blk_slot, blk_next, xs, w_exp_in,
      b_exp_in.reshape(DEPTH, N_EXPERTS, 1, 2 * D_EXPERT), w_exp_out,
      b_exp_out.reshape(DEPTH, N_EXPERTS, 1, D_MODEL))


def _combine_kernel(dest_ref, y_hbm, x_ref, tg_ref, gate_ref, *rest, final):
    if final:
        fg_ref, o_ref, n_ref, buf, sems = rest
    else:
        o_ref, buf, sems = rest
    i = pl.program_id(0)
    slot = i % 2

    def row_copies(tile, slot, chunk):
        copies = []
        for j in range(DMA_CHUNK):
            r = chunk * DMA_CHUNK + j
            for k in range(TOP_K):
                src = y_hbm.at[_token_rows(dest_ref[k * N_TOK + tile * ROW_TILE + r]), :]
                copies.append(pltpu.make_async_copy(src, buf.at[slot, k, _token_rows(r), :], sems.at[slot]))
        return copies

    def start_tile(tile, slot):
        def body(chunk, carry):
            for n, cp in enumerate(row_copies(tile, slot, chunk)):
                cp.start(priority=n % 2)
            return carry
        lax.fori_loop(0, ROW_TILE // DMA_CHUNK, body, 0)

    def wait_tile(tile, slot):
        def body(chunk, carry):
            for cp in row_copies(tile, slot, chunk):
                cp.wait()
            return carry
        lax.fori_loop(0, ROW_TILE // DMA_CHUNK, body, 0)

    @pl.when(i == 0)
    def _():
        start_tile(0, 0)

    @pl.when(i + 1 < pl.num_programs(0))
    def _():
        start_tile(i + 1, 1 - slot)

    wait_tile(i, slot)

    tg = tg_ref[...]
    acc = tg[:, 0:1] * _load_token_rows(buf.at[slot, 0])
    for k in range(1, TOP_K):
        acc = acc + tg[:, k:k + 1] * _load_token_rows(buf.at[slot, k])
    out = x_ref[...] + gate_ref[...] * acc
    o_ref[...] = out
    if final:
        n_ref[...] = out * lax.rsqrt(jnp.mean(out * out, axis=-1, keepdims=True) + NORM_EPS) * fg_ref[...]


def _combine(layer, dest, yb, x1, tg, mods, final_g=None):
    final = final_g is not None
    row = lambda w: pl.BlockSpec((ROW_TILE, w), lambda i, d: (i, 0))
    in_specs = [pl.BlockSpec(memory_space=pl.ANY), row(D_MODEL), row(TOP_K),
                pl.BlockSpec((None, 1, D_MODEL),
                             lambda i, d: ((layer * N_GROUPS + _tile_group(i)) * 6 + 5, 0, 0))]
    args = [dest, yb, x1, tg, mods]
    out_specs = [row(D_MODEL)]
    out_shape = [jax.ShapeDtypeStruct((N_TOK, D_MODEL), F32)]
    if final:
        in_specs.append(pl.BlockSpec((1, D_MODEL), lambda i, d: (0, 0)))
        args.append(final_g)
        out_specs.append(row(D_MODEL))
        out_shape.append(jax.ShapeDtypeStruct((N_TOK, D_MODEL), F32))
    grid_spec = pltpu.PrefetchScalarGridSpec(
        num_scalar_prefetch=1,
        grid=(N_TILES,),
        in_specs=in_specs,
        out_specs=out_specs,
        scratch_shapes=[pltpu.VMEM((2, TOP_K, ROW_TILE * ROW_SUBLANES, LANES), F32),
                        pltpu.SemaphoreType.DMA((2,))],
    )
    return pl.pallas_call(
        functools.partial(_combine_kernel, final=final),
        grid_spec=grid_spec,
        out_shape=out_shape,
        compiler_params=_params(("arbitrary",)),
        name=f"combine_l{layer}",
    )(*args)


def _rope_tables():
    rows = DEC_SEQ // GRID_W
    row = jnp.repeat(jnp.arange(rows, dtype=F32), GRID_W)
    col = jnp.tile(jnp.arange(GRID_W, dtype=F32), rows)
    inv_freq = ROPE_THETA ** (-jnp.arange(0, ROPE_AXIS_DIM, 2, dtype=F32) / ROPE_AXIS_DIM)
    ang = jnp.stack([row[:, None] * inv_freq, col[:, None] * inv_freq], axis=1)
    cos, sin = jnp.cos(ang), jnp.sin(ang)
    cos_h = jnp.concatenate([cos, cos], axis=-1).reshape(DEC_SEQ, HEAD_DIM)
    sin_h = jnp.concatenate([-sin, sin], axis=-1).reshape(DEC_SEQ, HEAD_DIM)
    reps = LANES // HEAD_DIM
    cos_t = jnp.concatenate([jnp.ones((IN_TILE, LANES), F32), jnp.tile(cos_h, (1, reps))], axis=0)
    sin_t = jnp.concatenate([jnp.zeros((IN_TILE, LANES), F32), jnp.tile(sin_h, (1, reps))], axis=0)
    return cos_t, sin_t


def _dft_tables(n, scale):
    col = jnp.arange(n, dtype=I32)

    def direct(rows):
        ang = ((rows[:, None] * col[None, :]) % n).astype(F32) * (2.0 * math.pi / n)
        return jnp.cos(ang), jnp.sin(ang)

    if n <= DFT_ROW_BLOCK:
        c, s = direct(col)
        return c * scale, s * scale
    lo_c, lo_s = direct(jnp.arange(DFT_ROW_BLOCK, dtype=I32))
    hi_c, hi_s = direct(jnp.arange(n // DFT_ROW_BLOCK, dtype=I32) * DFT_ROW_BLOCK)
    hi_c, hi_s = (hi_c * scale)[:, None, :], (hi_s * scale)[:, None, :]
    c = hi_c * lo_c[None] - hi_s * lo_s[None]
    s = hi_s * lo_c[None] + hi_c * lo_s[None]
    return c.reshape(n, n), s.reshape(n, n)


def _channel_dft():
    c, s = _dft_tables(FOURIER_GROUP_W, 1.0)
    eye = jnp.eye(N_FOURIER_GROUPS, dtype=F32)
    return jnp.concatenate([jnp.kron(eye, c), jnp.kron(eye, s)], axis=1).astype(BF16)


def _extend_heads(t):
    h = [t[..., i * HEAD_DIM:(i + 1) * HEAD_DIM] for i in range(N_KV_HEADS)]
    order = [0, 0, 0, 1, 1, 1, 2, 2, 2, 3, 3, 3]
    return jnp.concatenate([h[i] for i in order], axis=-1)


def _plan(top_e_t, rank_t, counts):
    cnt = counts[:N_EXPERTS, 0]
    padded = (cnt + EXPERT_TILE - 1) // EXPERT_TILE * EXPERT_TILE
    pad_end = jnp.cumsum(padded)
    pad_start = pad_end - padded
    experts = jnp.arange(N_EXPERTS, dtype=I32)
    start_of = jnp.sum(jnp.where(top_e_t[..., None] == experts, pad_start, 0), axis=-1)
    dest = (start_of + rank_t).reshape(-1).astype(I32)
    blk_start = jnp.arange(N_EXPERT_BLOCKS, dtype=I32) * EXPERT_TILE
    blk_e = jnp.minimum(jnp.sum(pad_end[None, :] <= blk_start[:, None], axis=1), N_EXPERTS - 1).astype(I32)
    valid = blk_start < pad_end[-1]
    first = jnp.logical_and(valid, jnp.concatenate([jnp.ones((1,), bool), blk_e[1:] != blk_e[:-1]]))
    blk_kind = (valid.astype(I32) + first.astype(I32)).astype(I32)
    blk_slot = ((jnp.cumsum(first.astype(I32)) - 1) % 2).astype(I32)
    later = jnp.logical_and(experts[None, :] > experts[:, None], cnt[None, :] > 0)
    next_of = jnp.min(jnp.where(later, experts[None, :], N_EXPERTS), axis=1)
    next_of = jnp.where(next_of == N_EXPERTS, -1, next_of)
    blk_next = jnp.sum(jnp.where(blk_e[:, None] == experts[None, :], next_of[None, :], 0), axis=1).astype(I32)
    zero_blk = jnp.where(cnt % EXPERT_TILE != 0, (pad_start + cnt) // EXPERT_TILE, -1).astype(I32)
    n_blk = (pad_end[-1:] // EXPERT_TILE).astype(I32)
    return dest, (blk_e, blk_kind, blk_slot, blk_next), zero_blk, n_blk


def kernel(x_prompt, x_sample, cache_k, cache_v, c, c_ctx, w_ada, b_ada, norm1_g, w_in, q_norm_g,
           k_norm_g, w_attn_o, w_fourier_o, w_out, norm2_g, w_router, b_router, w_exp_in, b_exp_in,
           w_exp_out, b_exp_out, final_norm_g):
    x = jnp.concatenate([x_prompt.reshape(N_CTX, D_MODEL), x_sample.reshape(N_LAT, D_MODEL)], axis=0)

    cond = jnp.concatenate([c_ctx[None, :], c, jnp.zeros((8 - N_GROUPS, D_MODEL), F32)], axis=0)
    mods = _mods(cond.T, w_ada, b_ada)[:, :N_GROUPS].reshape(DEPTH * N_GROUPS * 6, 1, D_MODEL)

    cos_t, sin_t = _rope_tables()
    bd = jnp.kron(jnp.eye(LANES // HEAD_DIM, dtype=F32),
                  jnp.full((HEAD_DIM, HEAD_DIM), 1.0 / HEAD_DIM, F32)).astype(BF16)
    bd = jnp.concatenate([bd, bd], axis=0)
    cs = _channel_dft()
    dft_ctx = [t.astype(BF16) for t in _dft_tables(SEQ, (SEQ * FOURIER_GROUP_W) ** -0.5)]
    dft_lat = [t.astype(BF16) for t in _dft_tables(DEC_SEQ, (DEC_SEQ * FOURIER_GROUP_W) ** -0.5)]

    w_in_b = w_in.astype(BF16)
    wa_b = w_attn_o.astype(BF16)
    wf_b = w_fourier_o.astype(BF16)
    wo_b = w_out.astype(BF16)
    norm1 = norm1_g.reshape(DEPTH, 1, D_MODEL)
    norm2 = norm2_g.reshape(DEPTH, 1, D_MODEL)
    qg = jnp.tile(q_norm_g, (1, LANES // HEAD_DIM)).reshape(DEPTH, 1, LANES)
    kg = jnp.tile(k_norm_g, (1, LANES // HEAD_DIM)).reshape(DEPTH, 1, LANES)
    wr_t = jnp.swapaxes(w_router, 1, 2)
    br_t = b_router.reshape(DEPTH, N_EXPERTS, 1)
    tok = jnp.arange(ROUTER_TILE, dtype=I32)
    tri = (tok[:, None] <= tok[None, :]).astype(BF16)
    kcache = _extend_heads(cache_k.reshape(DEC_BATCH, DEPTH, PAST_LEN, KV_W)).astype(BF16)
    vcache = _extend_heads(cache_v.reshape(DEC_BATCH, DEPTH, PAST_LEN, KV_W)).astype(BF16)

    keys, vals = [], []
    y_norm = None
    for l in range(DEPTH):
        q, kx, vx, kf, vf, fcs, ga, gf = _inproj(l, x, norm1, mods, w_in_b, qg, kg, cos_t, sin_t, bd, cs)
        keys.append(kf[:N_CTX])
        vals.append(vf[:N_CTX])
        a = _attention_ctx(l, q, kx, vx)
        a = _attention_lat(l, a, kx, vx, kcache, vcache)
        x1 = _merge(l, "ctx", a, fcs, dft_ctx[0], dft_ctx[1], ga, gf, x, mods, wa_b, wf_b, wo_b)
        x1 = _merge(l, "lat", a, fcs, dft_lat[0], dft_lat[1], ga, gf, x1, mods, wa_b, wf_b, wo_b)
        h2, top_e, top_g, rank, counts = _router(l, x1, norm2, mods, wr_t, br_t, tri)
        dest, blk_tables, zero_blk, n_blk = _plan(top_e, rank, counts)
        xs = _dispatch(l, dest, zero_blk, n_blk, h2)
        yb = _experts(l, *blk_tables, xs, w_exp_in, b_exp_in, w_exp_out, b_exp_out)
        top_g = top_g.T
        if l == DEPTH - 1:
            x, y_norm = _combine(l, dest, yb, x1, top_g, mods, final_norm_g.reshape(1, D_MODEL))
        else:
            (x,) = _combine(l, dest, yb, x1, top_g, mods)

    y_prompt = y_norm[:N_CTX].reshape(BATCH, SEQ, D_MODEL)
    y_sample = y_norm[N_CTX:].reshape(DEC_BATCH, DEC_SEQ, D_MODEL)
    shape = (BATCH, SEQ, N_KV_HEADS, HEAD_DIM)
    new_k = jnp.stack([k.reshape(shape) for k in keys], axis=1)
    new_v = jnp.stack([v.reshape(shape) for v in vals], axis=1)
    return (y_prompt, y_sample, new_k, new_v)
```

```python
import functools
import math

import jax
import jax.numpy as jnp
from jax import lax
from jax.experimental import pallas as pl
from jax.experimental.pallas import tpu as pltpu

F32 = jnp.float32
BF16 = jnp.bfloat16
I32 = jnp.int32

D_MODEL = 1024
DEPTH = 4
BATCH = 16
SEQ = 256
DEC_BATCH = 2
DEC_SEQ = 2048
PAST_LEN = 512
GRID_W = 64
HEAD_DIM = 64
N_Q_HEADS = 12
N_KV_HEADS = 4
ATTN_W = N_Q_HEADS * HEAD_DIM
KV_W = N_KV_HEADS * HEAD_DIM
FOURIER_W = D_MODEL // 4
N_FOURIER_GROUPS = 4
FOURIER_GROUP_W = FOURIER_W // N_FOURIER_GROUPS
IN_W = ATTN_W + 2 * KV_W + FOURIER_W + 2 * D_MODEL
ROPE_AXIS_DIM = HEAD_DIM // 2
ROPE_THETA = 10000.0
N_EXPERTS = 32
TOP_K = 4
D_EXPERT = D_MODEL
SWIGLU_ALPHA = 1.702
SWIGLU_LIMIT = 7.0
NORM_EPS = 1e-6

N_CTX = BATCH * SEQ
N_LAT = DEC_BATCH * DEC_SEQ
N_TOK = N_CTX + N_LAT
LANES = 128
ROW_TILE = 256
IN_TILE = 1024
LAT_MERGE_TILE = 512
ROUTER_TILE = 512
LAT_Q_TILE = 256
N_TILES = N_TOK // ROW_TILE
CTX_TILES = N_CTX // ROW_TILE
LAT_TILES_PER_BATCH = DEC_SEQ // ROW_TILE
N_GROUPS = 1 + DEC_BATCH
EXPERT_TILE = 256
N_ASSIGN = N_TOK * TOP_K
N_EXPERT_BLOCKS = N_ASSIGN // EXPERT_TILE + N_EXPERTS
M_PAD = N_EXPERT_BLOCKS * EXPERT_TILE
DMA_CHUNK = 32
DISPATCH_SLOTS = 3
DFT_ROW_BLOCK = 64
Q_CHUNKS = ATTN_W // LANES
NEG_BIG = -1e30
VMEM_LIMIT = 56 * 1024 * 1024

_NT = (((1,), (1,)), ((), ()))


def _params(sem, vmem=VMEM_LIMIT):
    return pltpu.CompilerParams(dimension_semantics=sem, vmem_limit_bytes=vmem)


def _tile_group(i, tile=ROW_TILE):
    ctx_tiles = N_CTX // tile
    return jnp.where(i < ctx_tiles, 0, 1 + (i - ctx_tiles) // (DEC_SEQ // tile))


def _rope_block(i, tile):
    ctx_tiles = N_CTX // tile
    return jnp.where(i < ctx_tiles, 0, 1 + (i - ctx_tiles) % (DEC_SEQ // tile))


def _mod_spec(layer, slot, tile=ROW_TILE):
    return pl.BlockSpec((None, 1, D_MODEL),
                        lambda i: ((layer * N_GROUPS + _tile_group(i, tile)) * 6 + slot, 0, 0))


def _layer_vec_spec(layer, width=D_MODEL):
    return pl.BlockSpec((None, 1, width), lambda *_: (layer, 0, 0))


ROW_SUBLANES = D_MODEL // LANES


def _store_token_rows(ref, value):
    n = value.shape[0]
    for s in range(ROW_SUBLANES):
        ref[pl.ds(s, n, stride=ROW_SUBLANES), :] = value[:, s * LANES:(s + 1) * LANES]


def _load_token_rows(ref):
    n = ref.shape[0] // ROW_SUBLANES
    return jnp.concatenate([ref[pl.ds(s, n, stride=ROW_SUBLANES), :] for s in range(ROW_SUBLANES)], axis=1)


def _token_rows(token):
    return pl.ds(pl.multiple_of(token * ROW_SUBLANES, ROW_SUBLANES), ROW_SUBLANES)


def _mods_kernel(cond_ref, w_ref, b_ref, o_ref):
    c = cond_ref[...]
    s = c * jax.nn.sigmoid(c)
    w = w_ref[...]
    o_ref[...] = jnp.broadcast_to(b_ref[...], o_ref.shape)
    for g in range(N_GROUPS):
        o_ref[g:g + 1, :] = jnp.sum(w * s[:, g:g + 1], axis=0, keepdims=True) + b_ref[...]


def _mods(cond, w_ada, b_ada):
    tn = 1536
    return pl.pallas_call(
        _mods_kernel,
        grid=(DEPTH, 6 * D_MODEL // tn),
        in_specs=[pl.BlockSpec((D_MODEL, 8), lambda l, j: (0, 0)),
                  pl.BlockSpec((None, D_MODEL, tn), lambda l, j: (l, 0, j)),
                  pl.BlockSpec((None, 1, tn), lambda l, j: (l, 0, j))],
        out_specs=pl.BlockSpec((None, 8, tn), lambda l, j: (l, 0, j)),
        out_shape=jax.ShapeDtypeStruct((DEPTH, 8, 6 * D_MODEL), F32),
        compiler_params=_params(("parallel", "parallel")),
        name="adaln_mods",
    )(cond, w_ada, b_ada.reshape(DEPTH, 1, 6 * D_MODEL))


def _inproj_kernel(x_ref, g_ref, sh_ref, sc_ref, w_ref, qg_ref, kg_ref, cos_ref, sin_ref,
                   bd_ref, cs_ref,
                   kc_in, vc_in,
                   q_ref, kx_ref, vx_ref, kf_ref, vf_ref, fcs_ref, ga_ref, gf_ref, h_scr):
    del kc_in, vc_in
    is_ctx = pl.program_id(0) < N_CTX // IN_TILE
    x = x_ref[...]
    y = x * lax.rsqrt(jnp.mean(x * x, axis=-1, keepdims=True) + NORM_EPS) * g_ref[...]
    h_scr[...] = (y * (1.0 + sc_ref[...]) + sh_ref[...]).astype(BF16)
    hb = h_scr[...]
    lane = lax.broadcasted_iota(I32, (x_ref.shape[0], LANES), 1)
    low_half = lane < HEAD_DIM
    rot_first = (lane % ROPE_AXIS_DIM) < (ROPE_AXIS_DIM // 2)
    cos = cos_ref[...]
    sin = sin_ref[...]
    bd = bd_ref[...]

    def proj(lo, width):
        return jnp.dot(hb, w_ref[:, lo:lo + width], preferred_element_type=F32)

    def head_norm_rope(p, gain):
        pp = p * p
        hi = pp.astype(BF16)
        lo = (pp - hi.astype(F32)).astype(BF16)
        msq = jnp.dot(jnp.concatenate([hi, lo], axis=1), bd, preferred_element_type=F32)
        n = p * lax.rsqrt(msq + NORM_EPS) * gain
        partner = jnp.where(rot_first, pltpu.roll(n, LANES - ROPE_AXIS_DIM // 2, 1),
                            pltpu.roll(n, ROPE_AXIS_DIM // 2, 1))
        return n * cos + partner * sin

    def head_pairs(r):
        sw = pltpu.roll(r, HEAD_DIM, 1)
        return jnp.where(low_half, r, sw), r, jnp.where(low_half, sw, r)

    scale = HEAD_DIM ** -0.5
    wide = 2 * LANES
    for c2 in range(ATTN_W // wide):
        p = proj(c2 * wide, wide)
        for t in range(2):
            c = 2 * c2 + t
            r = head_norm_rope(p[:, t * LANES:(t + 1) * LANES], qg_ref[...])
            q_ref[:, c * LANES:(c + 1) * LANES] = (r * scale).astype(BF16)

    p = proj(ATTN_W, KV_W)
    for j in range(KV_W // LANES):
        r = head_norm_rope(p[:, j * LANES:(j + 1) * LANES], kg_ref[...])
        @pl.when(is_ctx)
        def _():
            kf_ref[:, :, j * LANES:(j + 1) * LANES] = r.reshape(kf_ref.shape[0], SEQ, LANES)

        for t, piece in enumerate(head_pairs(r)):
            c = 3 * j + t
            kx_ref[:, c * LANES:(c + 1) * LANES] = piece.astype(BF16)

    p = proj(ATTN_W + KV_W, KV_W)
    @pl.when(is_ctx)
    def _():
        vf_ref[...] = p.reshape(vf_ref.shape)

    for j in range(KV_W // LANES):
        for t, piece in enumerate(head_pairs(p[:, j * LANES:(j + 1) * LANES])):
            c = 3 * j + t
            vx_ref[:, c * LANES:(c + 1) * LANES] = piece.astype(BF16)

    f = proj(ATTN_W + 2 * KV_W, FOURIER_W)
    fcs_ref[...] = jnp.dot(f.astype(BF16), cs_ref[...], preferred_element_type=F32).astype(BF16)

    gate_lo = ATTN_W + 2 * KV_W + FOURIER_W
    half = D_MODEL // 2
    for j in range(2):
        ga_ref[:, j * half:(j + 1) * half] = jax.nn.sigmoid(proj(gate_lo + j * half, half)).astype(BF16)
        gf_ref[:, j * half:(j + 1) * half] = jax.nn.sigmoid(
            proj(gate_lo + D_MODEL + j * half, half)).astype(BF16)


def _inproj(layer, x, norm_g, mods, w_in, qg, kg, cos_tab, sin_tab, bd, cs, k_cache, v_cache):
    row = lambda w: pl.BlockSpec((IN_TILE, w), lambda i: (i, 0))
    const = lambda a: pl.BlockSpec(a.shape, lambda i: (0,) * a.ndim)
    rope = pl.BlockSpec((IN_TILE, LANES), lambda i: (_rope_block(i, IN_TILE), 0))
    ext_w = Q_CHUNKS * LANES
    last_ctx = N_CTX // IN_TILE - 1
    cache = pl.BlockSpec((IN_TILE // SEQ, None, SEQ, KV_W), lambda i: (jnp.minimum(i, last_ctx), layer, 0, 0))
    cache_shape = jax.ShapeDtypeStruct((BATCH, DEPTH, SEQ, KV_W), F32)
    hbm = pl.BlockSpec(memory_space=pl.ANY)
    return pl.pallas_call(
        _inproj_kernel,
        grid=(N_TOK // IN_TILE,),
        in_specs=[row(D_MODEL), _layer_vec_spec(layer), _mod_spec(layer, 0, IN_TILE), _mod_spec(layer, 1, IN_TILE),
                  pl.BlockSpec((None, D_MODEL, IN_W), lambda i: (layer, 0, 0)),
                  _layer_vec_spec(layer, LANES), _layer_vec_spec(layer, LANES), rope, rope,
                  const(bd), const(cs), hbm, hbm],
        out_specs=[row(ATTN_W), row(ext_w), row(ext_w), cache, cache,
                   row(2 * FOURIER_W), row(D_MODEL), row(D_MODEL)],
        out_shape=[jax.ShapeDtypeStruct((N_TOK, ATTN_W), BF16),
                   jax.ShapeDtypeStruct((N_TOK, ext_w), BF16),
                   jax.ShapeDtypeStruct((N_TOK, ext_w), BF16),
                   cache_shape, cache_shape,
                   jax.ShapeDtypeStruct((N_TOK, 2 * FOURIER_W), BF16),
                   jax.ShapeDtypeStruct((N_TOK, D_MODEL), BF16),
                   jax.ShapeDtypeStruct((N_TOK, D_MODEL), BF16)],
        scratch_shapes=[pltpu.VMEM((IN_TILE, D_MODEL), BF16)],
        input_output_aliases={11: 3, 12: 4},
        compiler_params=_params(("arbitrary",)),
        name=f"inproj_l{layer}",
    )(x, norm_g, mods, mods, w_in, qg, kg, cos_tab, sin_tab, bd, cs, k_cache, v_cache)


def _attn_kernel(*refs, n_parts):
    q_ref = refs[0]
    k_refs = refs[1:1 + n_parts]
    v_refs = refs[1 + n_parts:1 + 2 * n_parts]
    o_ref = refs[-1]
    tq = q_ref.shape[0]
    low_half = lax.broadcasted_iota(I32, (tq, LANES), 1) < HEAD_DIM
    for c in range(Q_CHUNKS):
        cols = slice(c * LANES, (c + 1) * LANES)
        qc = q_ref[:, cols]
        zero = jnp.zeros_like(qc)
        outs = []
        for mask in (low_half, jnp.logical_not(low_half)):
            qm = jnp.where(mask, qc, zero)
            ss = [lax.dot_general(qm, k[:, cols], _NT, preferred_element_type=F32) for k in k_refs]
            m = ss[0].max(axis=-1, keepdims=True)
            for s in ss[1:]:
                m = jnp.maximum(m, s.max(axis=-1, keepdims=True))
            ps = [jnp.exp(s - m) for s in ss]
            den = ps[0].sum(axis=-1, keepdims=True)
            for p in ps[1:]:
                den = den + p.sum(axis=-1, keepdims=True)
            acc = jnp.dot(ps[0].astype(BF16), v_refs[0][:, cols], preferred_element_type=F32)
            for p, v in zip(ps[1:], v_refs[1:]):
                acc = acc + jnp.dot(p.astype(BF16), v[:, cols], preferred_element_type=F32)
            outs.append(acc / den)
        o_ref[:, cols] = jnp.where(low_half, outs[0], outs[1]).astype(BF16)


def _attention_ctx(layer, q, kx, vx):
    w = Q_CHUNKS * LANES
    blk = lambda width: pl.BlockSpec((SEQ, width), lambda b: (b, 0))
    return pl.pallas_call(
        functools.partial(_attn_kernel, n_parts=1),
        grid=(BATCH,),
        in_specs=[blk(ATTN_W), blk(w), blk(w)],
        out_specs=blk(ATTN_W),
        out_shape=jax.ShapeDtypeStruct((N_TOK, ATTN_W), BF16),
        input_output_aliases={0: 0},
        compiler_params=_params(("parallel",)),
        name=f"attn_ctx_l{layer}",
    )(q, kx, vx)


def _attention_lat(layer, q, kx, vx, kcache, vcache):
    w = Q_CHUNKS * LANES
    nq = DEC_SEQ // LAT_Q_TILE
    first = N_CTX // LAT_Q_TILE
    qblk = pl.BlockSpec((LAT_Q_TILE, ATTN_W), lambda b, i: (first + b * nq + i, 0))
    new = lambda width: pl.BlockSpec((DEC_SEQ, width), lambda b, i: (N_CTX // DEC_SEQ + b, 0))
    old = lambda width: pl.BlockSpec((None, None, PAST_LEN, width), lambda b, i: (b, layer, 0, 0))
    return pl.pallas_call(
        functools.partial(_attn_kernel, n_parts=2),
        grid=(DEC_BATCH, nq),
        in_specs=[qblk, old(w), new(w), old(w), new(w)],
        out_specs=qblk,
        out_shape=jax.ShapeDtypeStruct((N_TOK, ATTN_W), BF16),
        input_output_aliases={0: 0},
        compiler_params=_params(("parallel", "parallel")),
        name=f"attn_lat_l{layer}",
    )(q, kcache, kx, vcache, vx)


def _merge_kernel(a_ref, fc_ref, fs_ref, cl_ref, sl_ref, ga_ref, gf_ref, x_ref, gate_ref,
                  wa_ref, wf_ref, wo_ref, o_ref):
    attn = jnp.dot(a_ref[...], wa_ref[...], preferred_element_type=F32)
    fo = (jnp.dot(cl_ref[...], fc_ref[...], preferred_element_type=F32)
          - jnp.dot(sl_ref[...], fs_ref[...], preferred_element_type=F32))
    four = jnp.dot(fo.astype(BF16), wf_ref[...], preferred_element_type=F32)
    merged = ga_ref[...].astype(F32) * attn + gf_ref[...].astype(F32) * four
    mix = jnp.dot(merged.astype(BF16), wo_ref[...], preferred_element_type=F32)
    o_ref[...] = x_ref[...] + gate_ref[...] * mix


def _merge(layer, stream, a, fcs, cl, sl, ga, gf, x, mods, wa, wf, wo):
    if stream == "ctx":
        seq, n_batch, first_row, rows = SEQ, BATCH, 0, SEQ
    else:
        seq, n_batch, first_row, rows = DEC_SEQ, DEC_BATCH, N_CTX, LAT_MERGE_TILE
    nt = seq // rows
    tile = lambda b, i: first_row // rows + b * nt + i
    row = lambda w: pl.BlockSpec((rows, w), lambda b, i: (tile(b, i), 0))
    wspec = lambda a_: pl.BlockSpec((None,) + a_.shape[1:], lambda b, i: (layer, 0, 0))
    in_specs = [row(ATTN_W),
                pl.BlockSpec((seq, FOURIER_W), lambda b, i: (first_row // seq + b, 0)),
                pl.BlockSpec((seq, FOURIER_W), lambda b, i: (first_row // seq + b, 1)),
                pl.BlockSpec((rows, seq), lambda b, i: (i, 0)),
                pl.BlockSpec((rows, seq), lambda b, i: (i, 0)),
                row(D_MODEL), row(D_MODEL), row(D_MODEL),
                pl.BlockSpec((None, 1, D_MODEL),
                             lambda b, i: ((layer * N_GROUPS + _tile_group(tile(b, i), rows)) * 6 + 2, 0, 0)),
                wspec(wa), wspec(wf), wspec(wo)]
    args = [a, fcs, fcs, cl, sl, ga, gf, x, mods, wa, wf, wo]
    aliases = {7: 0}
    return pl.pallas_call(
        _merge_kernel,
        grid=(n_batch, nt),
        in_specs=in_specs,
        out_specs=row(D_MODEL),
        out_shape=jax.ShapeDtypeStruct((N_TOK, D_MODEL), F32),
        input_output_aliases=aliases,
        compiler_params=_params(("parallel", "parallel")),
        name=f"merge_{stream}_l{layer}",
    )(*args)


def _router_kernel(x_ref, g_ref, sh_ref, sc_ref, wr_ref, br_ref, tri_ref,
                   h_ref, te_ref, tg_ref, rk_ref, cnt_ref, carry):
    @pl.when(pl.program_id(0) == 0)
    def _():
        carry[...] = jnp.zeros_like(carry)

    x = x_ref[...]
    y = x * lax.rsqrt(jnp.mean(x * x, axis=-1, keepdims=True) + NORM_EPS) * g_ref[...]
    h = y * (1.0 + sc_ref[...]) + sh_ref[...]
    _store_token_rows(h_ref, h)
    logits = lax.dot_general(wr_ref[...], h, _NT, preferred_element_type=F32,
                             precision=lax.Precision.HIGHEST) + br_ref[...]
    sub = lax.broadcasted_iota(I32, logits.shape, 0).astype(F32)
    vals, ids = [], []
    for _ in range(TOP_K):
        m = logits.max(axis=0, keepdims=True)
        idx = jnp.where(logits == m, sub, float(N_EXPERTS)).min(axis=0, keepdims=True)
        vals.append(m)
        ids.append(idx)
        logits = jnp.where(sub == idx, NEG_BIG, logits)
    es = [jnp.exp(v - vals[0]) for v in vals]
    den = es[0] + es[1] + es[2] + es[3]

    chosen = [sub == idx for idx in ids]
    member = jnp.zeros_like(sub)
    for ch in chosen:
        member = member + ch.astype(F32)
    upto = jnp.dot(member.astype(BF16), tri_ref[...], preferred_element_type=F32)
    before = carry[...] + upto - member
    for k in range(TOP_K):
        te_ref[k:k + 1, :] = ids[k].astype(I32)
        tg_ref[k:k + 1, :] = es[k] / den
        rk_ref[k:k + 1, :] = jnp.where(chosen[k], before, 0.0).sum(axis=0, keepdims=True).astype(I32)
    carry[...] = carry[...] + member.sum(axis=1, keepdims=True)
    cnt_ref[...] = carry[:, :LANES].astype(I32)


def _router(layer, x1, norm_g, mods, wr_t, br_t, tri):
    row = lambda w: pl.BlockSpec((ROUTER_TILE, w), lambda i: (i, 0))
    col = pl.BlockSpec((TOP_K, ROUTER_TILE), lambda i: (0, i))
    return pl.pallas_call(
        _router_kernel,
        grid=(N_TOK // ROUTER_TILE,),
        in_specs=[row(D_MODEL), _layer_vec_spec(layer), _mod_spec(layer, 3, ROUTER_TILE), _mod_spec(layer, 4, ROUTER_TILE),
                  pl.BlockSpec((None, N_EXPERTS, D_MODEL), lambda i: (layer, 0, 0)),
                  pl.BlockSpec((None, N_EXPERTS, 1), lambda i: (layer, 0, 0)),
                  pl.BlockSpec((ROUTER_TILE, ROUTER_TILE), lambda i: (0, 0))],
        out_specs=[pl.BlockSpec((ROUTER_TILE * ROW_SUBLANES, LANES), lambda i: (i, 0)), col, col, col,
                   pl.BlockSpec((N_EXPERTS, LANES), lambda i: (0, 0))],
        out_shape=[jax.ShapeDtypeStruct((N_TOK * ROW_SUBLANES, LANES), F32),
                   jax.ShapeDtypeStruct((TOP_K, N_TOK), I32),
                   jax.ShapeDtypeStruct((TOP_K, N_TOK), F32),
                   jax.ShapeDtypeStruct((TOP_K, N_TOK), I32),
                   jax.ShapeDtypeStruct((N_EXPERTS, LANES), I32)],
        scratch_shapes=[pltpu.VMEM((N_EXPERTS, ROUTER_TILE), F32)],
        compiler_params=_params(("arbitrary",)),
        name=f"router_l{layer}",
    )(x1, norm_g, mods, mods, wr_t, br_t, tri)


def _dispatch_kernel(dest_ref, zblk_ref, nblk_ref, h_hbm, xs_hbm, zeros, hbuf, zsem, hsems, csems):
    i = pl.program_id(0)
    n_steps = pl.num_programs(0)
    block_rows = EXPERT_TILE * ROW_SUBLANES
    tile_rows = ROW_TILE * ROW_SUBLANES

    def zero_fill(block):
        rows = pl.ds(pl.multiple_of(block * block_rows, block_rows), block_rows)
        return pltpu.make_async_copy(zeros, xs_hbm.at[rows, :], zsem)

    def fetch(tile):
        rows = pl.ds(pl.multiple_of(tile * tile_rows, tile_rows), tile_rows)
        slot = tile % DISPATCH_SLOTS
        return pltpu.make_async_copy(h_hbm.at[rows, :], hbuf.at[slot], hsems.at[slot])

    def row_copies(tile, chunk):
        slot = tile % DISPATCH_SLOTS
        copies = []
        for j in range(DMA_CHUNK):
            r = chunk * DMA_CHUNK + j
            for k in range(TOP_K):
                d = dest_ref[k * N_TOK + tile * ROW_TILE + r]
                copies.append(pltpu.make_async_copy(hbuf.at[slot, _token_rows(r), :],
                                                    xs_hbm.at[_token_rows(d), :], csems.at[slot]))
        return copies

    def start_tile(tile):
        def body(chunk, carry):
            for n, cp in enumerate(row_copies(tile, chunk)):
                cp.start(priority=n % 2)
            return carry
        lax.fori_loop(0, ROW_TILE // DMA_CHUNK, body, 0)

    def wait_tile(tile):
        def body(chunk, carry):
            for cp in row_copies(tile, chunk):
                cp.wait()
            return carry
        lax.fori_loop(0, ROW_TILE // DMA_CHUNK, body, 0)

    @pl.when(i == 0)
    def _():
        fetch(0).start()
        fetch(1).start()
        zeros[...] = jnp.zeros_like(zeros)
        for wait in (False, True):
            for e in range(N_EXPERTS):
                @pl.when(zblk_ref[e] >= 0)
                def _():
                    cp = zero_fill(zblk_ref[e])
                    cp.wait() if wait else cp.start()

            def tail(j, carry):
                cp = zero_fill(j)
                cp.wait() if wait else cp.start()
                return carry

            lax.fori_loop(nblk_ref[0], N_EXPERT_BLOCKS, tail, 0)

    fetch(i).wait()
    start_tile(i)

    @pl.when(i >= 1)
    def _():
        wait_tile(i - 1)

    @pl.when(i + 2 < n_steps)
    def _():
        fetch(i + 2).start()

    @pl.when(i == n_steps - 1)
    def _():
        wait_tile(i)


def _dispatch(layer, dest, zblk, nblk, h2):
    grid_spec = pltpu.PrefetchScalarGridSpec(
        num_scalar_prefetch=3,
        grid=(N_TILES,),
        in_specs=[pl.BlockSpec(memory_space=pl.ANY)],
        out_specs=pl.BlockSpec(memory_space=pl.ANY),
        scratch_shapes=[pltpu.VMEM((EXPERT_TILE * ROW_SUBLANES, LANES), F32),
                        pltpu.VMEM((DISPATCH_SLOTS, ROW_TILE * ROW_SUBLANES, LANES), F32),
                        pltpu.SemaphoreType.DMA(()),
                        pltpu.SemaphoreType.DMA((DISPATCH_SLOTS,)),
                        pltpu.SemaphoreType.DMA((DISPATCH_SLOTS,))],
    )
    return pl.pallas_call(
        _dispatch_kernel,
        grid_spec=grid_spec,
        out_shape=jax.ShapeDtypeStruct((M_PAD * ROW_SUBLANES, LANES), F32),
        compiler_params=_params(("arbitrary",)),
        name=f"dispatch_l{layer}",
    )(dest, zblk, nblk, h2)


def _expert_kernel(be_ref, kind_ref, slot_ref, next_ref, x_ref, wi_hbm, bi_ref, wo_hbm, bo_ref, y_ref,
                   wi_f32, wo_f32, wi_bf, wo_bf, sems, *, layer):
    b = pl.program_id(0)

    def weight_copies(expert, slot):
        return (pltpu.make_async_copy(wi_hbm.at[layer, expert], wi_f32.at[slot], sems.at[0, slot]),
                pltpu.make_async_copy(wo_hbm.at[layer, expert], wo_f32.at[slot], sems.at[1, slot]))

    @pl.when(kind_ref[b] == 0)
    def _():
        y_ref[...] = jnp.zeros_like(y_ref)

    @pl.when(kind_ref[b] == 2)
    def _():
        slot = slot_ref[b]
        own = weight_copies(be_ref[b], slot)

        @pl.when(b == 0)
        def _():
            for cp in own:
                cp.start()

        for cp in own:
            cp.wait()

        @pl.when(next_ref[b] >= 0)
        def _():
            for cp in weight_copies(next_ref[b], 1 - slot):
                cp.start(priority=1)

        wi_bf[...] = wi_f32[slot].astype(BF16)
        wo_bf[...] = wo_f32[slot].astype(BF16)

    @pl.when(kind_ref[b] > 0)
    def _():
        x = _load_token_rows(x_ref).astype(BF16)
        hdn = jnp.dot(x, wi_bf[...], preferred_element_type=F32) + bi_ref[...]
        glu = jnp.minimum(hdn[:, :D_EXPERT], SWIGLU_LIMIT)
        lin = jnp.clip(hdn[:, D_EXPERT:], -SWIGLU_LIMIT, SWIGLU_LIMIT)
        act = glu * jax.nn.sigmoid(SWIGLU_ALPHA * glu) * (lin + 1.0)
        y = jnp.dot(act.astype(BF16), wo_bf[...], preferred_element_type=F32) + bo_ref[...]
        _store_token_rows(y_ref, y)


def _experts(layer, blk_e, blk_kind, blk_slot, blk_next, xs, w_exp_in, b_exp_in, w_exp_out, b_exp_out):
    bias = lambda cols: pl.BlockSpec((None, None, 1, cols), lambda b, be, *_: (layer, be[b], 0, 0))
    blk = pl.BlockSpec((EXPERT_TILE * ROW_SUBLANES, LANES), lambda b, *_: (b, 0))
    hbm = pl.BlockSpec(memory_space=pl.ANY)
    grid_spec = pltpu.PrefetchScalarGridSpec(
        num_scalar_prefetch=4,
        grid=(N_EXPERT_BLOCKS,),
        in_specs=[blk, hbm, bias(2 * D_EXPERT), hbm, bias(D_MODEL)],
        out_specs=blk,
        scratch_shapes=[pltpu.VMEM((2, D_MODEL, 2 * D_EXPERT), F32),
                        pltpu.VMEM((2, D_EXPERT, D_MODEL), F32),
                        pltpu.VMEM((D_MODEL, 2 * D_EXPERT), BF16),
                        pltpu.VMEM((D_EXPERT, D_MODEL), BF16),
                        pltpu.SemaphoreType.DMA((2, 2))],
    )
    return pl.pallas_call(
        functools.partial(_expert_kernel, layer=layer),
        grid_spec=grid_spec,
        out_shape=jax.ShapeDtypeStruct((M_PAD * ROW_SUBLANES, LANES), F32),
        compiler_params=_params(("arbitrary",)),
        name=f"experts_l{layer}",
    )(blk_e, blk_kind, blk_slot, blk_next, xs, w_exp_in,
      b_exp_in.reshape(DEPTH, N_EXPERTS, 1, 2 * D_EXPERT), w_exp_out,
      b_exp_out.reshape(DEPTH, N_EXPERTS, 1, D_MODEL))


def _combine_kernel(dest_ref, y_hbm, x_ref, tg_ref, gate_ref, *rest, final):
    if final:
        fg_ref, o_ref, nc_ref, nl_ref, buf, sems = rest
    else:
        o_ref, buf, sems = rest
    i = pl.program_id(0)
    slot = i % 2

    def row_copies(tile, slot, chunk):
        copies = []
        for j in range(DMA_CHUNK):
            r = chunk * DMA_CHUNK + j
            for k in range(TOP_K):
                src = y_hbm.at[_token_rows(dest_ref[k * N_TOK + tile * ROW_TILE + r]), :]
                copies.append(pltpu.make_async_copy(src, buf.at[slot, k, _token_rows(r), :], sems.at[slot]))
        return copies

    def start_tile(tile, slot):
        def body(chunk, carry):
            for n, cp in enumerate(row_copies(tile, slot, chunk)):
                cp.start(priority=n % 2)
            return carry
        lax.fori_loop(0, ROW_TILE // DMA_CHUNK, body, 0)

    def wait_tile(tile, slot):
        def body(chunk, carry):
            for cp in row_copies(tile, slot, chunk):
                cp.wait()
            return carry
        lax.fori_loop(0, ROW_TILE // DMA_CHUNK, body, 0)

    @pl.when(i == 0)
    def _():
        start_tile(0, 0)

    @pl.when(i + 1 < pl.num_programs(0))
    def _():
        start_tile(i + 1, 1 - slot)

    wait_tile(i, slot)

    tg = tg_ref[...]
    acc = tg[:, 0:1] * _load_token_rows(buf.at[slot, 0])
    for k in range(1, TOP_K):
        acc = acc + tg[:, k:k + 1] * _load_token_rows(buf.at[slot, k])
    out = x_ref[...] + gate_ref[...] * acc
    o_ref[...] = out
    if final:
        normed = out * lax.rsqrt(jnp.mean(out * out, axis=-1, keepdims=True) + NORM_EPS) * fg_ref[...]

        @pl.when(i < CTX_TILES)
        def _():
            nc_ref[...] = normed

        @pl.when(i >= CTX_TILES)
        def _():
            nl_ref[...] = normed


def _combine(layer, dest, yb, x1, tg, mods, final_g=None):
    final = final_g is not None
    row = lambda w: pl.BlockSpec((ROW_TILE, w), lambda i, d: (i, 0))
    in_specs = [pl.BlockSpec(memory_space=pl.ANY), row(D_MODEL), row(TOP_K),
                pl.BlockSpec((None, 1, D_MODEL),
                             lambda i, d: ((layer * N_GROUPS + _tile_group(i)) * 6 + 5, 0, 0))]
    args = [dest, yb, x1, tg, mods]
    out_specs = [row(D_MODEL)]
    out_shape = [jax.ShapeDtypeStruct((N_TOK, D_MODEL), F32)]
    if final:
        in_specs.append(pl.BlockSpec((1, D_MODEL), lambda i, d: (0, 0)))
        args.append(final_g)
        out_specs.append(pl.BlockSpec((ROW_TILE, D_MODEL), lambda i, d: (jnp.minimum(i, CTX_TILES - 1), 0)))
        out_specs.append(pl.BlockSpec((ROW_TILE, D_MODEL), lambda i, d: (jnp.maximum(i - CTX_TILES, 0), 0)))
        out_shape.append(jax.ShapeDtypeStruct((N_CTX, D_MODEL), F32))
        out_shape.append(jax.ShapeDtypeStruct((N_LAT, D_MODEL), F32))
    grid_spec = pltpu.PrefetchScalarGridSpec(
        num_scalar_prefetch=1,
        grid=(N_TILES,),
        in_specs=in_specs,
        out_specs=out_specs,
        scratch_shapes=[pltpu.VMEM((2, TOP_K, ROW_TILE * ROW_SUBLANES, LANES), F32),
                        pltpu.SemaphoreType.DMA((2,))],
    )
    return pl.pallas_call(
        functools.partial(_combine_kernel, final=final),
        grid_spec=grid_spec,
        out_shape=out_shape,
        compiler_params=_params(("arbitrary",)),
        name=f"combine_l{layer}",
    )(*args)


def _rope_tables():
    rows = DEC_SEQ // GRID_W
    row = jnp.repeat(jnp.arange(rows, dtype=F32), GRID_W)
    col = jnp.tile(jnp.arange(GRID_W, dtype=F32), rows)
    inv_freq = ROPE_THETA ** (-jnp.arange(0, ROPE_AXIS_DIM, 2, dtype=F32) / ROPE_AXIS_DIM)
    ang = jnp.stack([row[:, None] * inv_freq, col[:, None] * inv_freq], axis=1)
    cos, sin = jnp.cos(ang), jnp.sin(ang)
    cos_h = jnp.concatenate([cos, cos], axis=-1).reshape(DEC_SEQ, HEAD_DIM)
    sin_h = jnp.concatenate([-sin, sin], axis=-1).reshape(DEC_SEQ, HEAD_DIM)
    reps = LANES // HEAD_DIM
    cos_t = jnp.concatenate([jnp.ones((IN_TILE, LANES), F32), jnp.tile(cos_h, (1, reps))], axis=0)
    sin_t = jnp.concatenate([jnp.zeros((IN_TILE, LANES), F32), jnp.tile(sin_h, (1, reps))], axis=0)
    return cos_t, sin_t


def _dft_tables(n, scale):
    col = jnp.arange(n, dtype=I32)

    def direct(rows):
        ang = ((rows[:, None] * col[None, :]) % n).astype(F32) * (2.0 * math.pi / n)
        return jnp.cos(ang), jnp.sin(ang)

    if n <= DFT_ROW_BLOCK:
        c, s = direct(col)
        return c * scale, s * scale
    lo_c, lo_s = direct(jnp.arange(DFT_ROW_BLOCK, dtype=I32))
    hi_c, hi_s = direct(jnp.arange(n // DFT_ROW_BLOCK, dtype=I32) * DFT_ROW_BLOCK)
    hi_c, hi_s = (hi_c * scale)[:, None, :], (hi_s * scale)[:, None, :]
    c = hi_c * lo_c[None] - hi_s * lo_s[None]
    s = hi_s * lo_c[None] + hi_c * lo_s[None]
    return c.reshape(n, n), s.reshape(n, n)


def _channel_dft():
    c, s = _dft_tables(FOURIER_GROUP_W, 1.0)
    eye = jnp.eye(N_FOURIER_GROUPS, dtype=F32)
    return jnp.concatenate([jnp.kron(eye, c), jnp.kron(eye, s)], axis=1).astype(BF16)


def _extend_heads(t):
    h = [t[..., i * HEAD_DIM:(i + 1) * HEAD_DIM] for i in range(N_KV_HEADS)]
    order = [0, 0, 0, 1, 1, 1, 2, 2, 2, 3, 3, 3]
    return jnp.concatenate([h[i] for i in order], axis=-1)


def _plan(top_e_t, rank_t, counts):
    cnt = counts[:N_EXPERTS, 0]
    padded = (cnt + EXPERT_TILE - 1) // EXPERT_TILE * EXPERT_TILE
    pad_end = jnp.cumsum(padded)
    pad_start = pad_end - padded
    experts = jnp.arange(N_EXPERTS, dtype=I32)
    start_of = jnp.sum(jnp.where(top_e_t[..., None] == experts, pad_start, 0), axis=-1)
    dest = (start_of + rank_t).reshape(-1).astype(I32)
    blk_start = jnp.arange(N_EXPERT_BLOCKS, dtype=I32) * EXPERT_TILE
    blk_e = jnp.minimum(jnp.sum(pad_end[None, :] <= blk_start[:, None], axis=1), N_EXPERTS - 1).astype(I32)
    valid = blk_start < pad_end[-1]
    first = jnp.logical_and(valid, jnp.concatenate([jnp.ones((1,), bool), blk_e[1:] != blk_e[:-1]]))
    blk_kind = (valid.astype(I32) + first.astype(I32)).astype(I32)
    blk_slot = ((jnp.cumsum(first.astype(I32)) - 1) % 2).astype(I32)
    later = jnp.logical_and(experts[None, :] > experts[:, None], cnt[None, :] > 0)
    next_of = jnp.min(jnp.where(later, experts[None, :], N_EXPERTS), axis=1)
    next_of = jnp.where(next_of == N_EXPERTS, -1, next_of)
    blk_next = jnp.sum(jnp.where(blk_e[:, None] == experts[None, :], next_of[None, :], 0), axis=1).astype(I32)
    zero_blk = jnp.where(cnt % EXPERT_TILE != 0, (pad_start + cnt) // EXPERT_TILE, -1).astype(I32)
    n_blk = (pad_end[-1:] // EXPERT_TILE).astype(I32)
    return dest, (blk_e, blk_kind, blk_slot, blk_next), zero_blk, n_blk


def kernel(x_prompt, x_sample, cache_k, cache_v, c, c_ctx, w_ada, b_ada, norm1_g, w_in, q_norm_g,
           k_norm_g, w_attn_o, w_fourier_o, w_out, norm2_g, w_router, b_router, w_exp_in, b_exp_in,
           w_exp_out, b_exp_out, final_norm_g):
    x = jnp.concatenate([x_prompt.reshape(N_CTX, D_MODEL), x_sample.reshape(N_LAT, D_MODEL)], axis=0)

    cond = jnp.concatenate([c_ctx[None, :], c, jnp.zeros((8 - N_GROUPS, D_MODEL), F32)], axis=0)
    mods = _mods(cond.T, w_ada, b_ada)[:, :N_GROUPS].reshape(DEPTH * N_GROUPS * 6, 1, D_MODEL)

    cos_t, sin_t = _rope_tables()
    bd = jnp.kron(jnp.eye(LANES // HEAD_DIM, dtype=F32),
                  jnp.full((HEAD_DIM, HEAD_DIM), 1.0 / HEAD_DIM, F32)).astype(BF16)
    bd = jnp.concatenate([bd, bd], axis=0)
    cs = _channel_dft()
    dft_ctx = [t.astype(BF16) for t in _dft_tables(SEQ, (SEQ * FOURIER_GROUP_W) ** -0.5)]
    dft_lat = [t.astype(BF16) for t in _dft_tables(DEC_SEQ, (DEC_SEQ * FOURIER_GROUP_W) ** -0.5)]

    w_in_b = w_in.astype(BF16)
    wa_b = w_attn_o.astype(BF16)
    wf_b = w_fourier_o.astype(BF16)
    wo_b = w_out.astype(BF16)
    norm1 = norm1_g.reshape(DEPTH, 1, D_MODEL)
    norm2 = norm2_g.reshape(DEPTH, 1, D_MODEL)
    qg = jnp.tile(q_norm_g, (1, LANES // HEAD_DIM)).reshape(DEPTH, 1, LANES)
    kg = jnp.tile(k_norm_g, (1, LANES // HEAD_DIM)).reshape(DEPTH, 1, LANES)
    wr_t = jnp.swapaxes(w_router, 1, 2)
    br_t = b_router.reshape(DEPTH, N_EXPERTS, 1)
    tok = jnp.arange(ROUTER_TILE, dtype=I32)
    tri = (tok[:, None] <= tok[None, :]).astype(BF16)
    kcache = _extend_heads(cache_k.reshape(DEC_BATCH, DEPTH, PAST_LEN, KV_W)).astype(BF16)
    vcache = _extend_heads(cache_v.reshape(DEC_BATCH, DEPTH, PAST_LEN, KV_W)).astype(BF16)

    new_k = jnp.zeros((BATCH, DEPTH, SEQ, KV_W), F32)
    new_v = jnp.zeros((BATCH, DEPTH, SEQ, KV_W), F32)
    y_ctx = y_lat = None
    for l in range(DEPTH):
        q, kx, vx, new_k, new_v, fcs, ga, gf = _inproj(l, x, norm1, mods, w_in_b, qg, kg, cos_t, sin_t, bd, cs,
                                                        new_k, new_v)
        a = _attention_ctx(l, q, kx, vx)
        a = _attention_lat(l, a, kx, vx, kcache, vcache)
        x1 = _merge(l, "ctx", a, fcs, dft_ctx[0], dft_ctx[1], ga, gf, x, mods, wa_b, wf_b, wo_b)
        x1 = _merge(l, "lat", a, fcs, dft_lat[0], dft_lat[1], ga, gf, x1, mods, wa_b, wf_b, wo_b)
        h2, top_e, top_g, rank, counts = _router(l, x1, norm2, mods, wr_t, br_t, tri)
        dest, blk_tables, zero_blk, n_blk = _plan(top_e, rank, counts)
        xs = _dispatch(l, dest, zero_blk, n_blk, h2)
        yb = _experts(l, *blk_tables, xs, w_exp_in, b_exp_in, w_exp_out, b_exp_out)
        top_g = top_g.T
        if l == DEPTH - 1:
            x, y_ctx, y_lat = _combine(l, dest, yb, x1, top_g, mods, final_norm_g.reshape(1, D_MODEL))
        else:
            (x,) = _combine(l, dest, yb, x1, top_g, mods)

    y_prompt = y_ctx.reshape(BATCH, SEQ, D_MODEL)
    y_sample = y_lat.reshape(DEC_BATCH, DEC_SEQ, D_MODEL)
    shape = (BATCH, DEPTH, SEQ, N_KV_HEADS, HEAD_DIM)
    return (y_prompt, y_sample, new_k.reshape(shape), new_v.reshape(shape))
```

```python
import functools
import math

import jax
import jax.numpy as jnp
from jax import lax
from jax.experimental import pallas as pl
from jax.experimental.pallas import tpu as pltpu

F32 = jnp.float32
BF16 = jnp.bfloat16
I32 = jnp.int32

D_MODEL = 1024
DEPTH = 4
BATCH = 16
SEQ = 256
DEC_BATCH = 2
DEC_SEQ = 2048
PAST_LEN = 512
GRID_W = 64
HEAD_DIM = 64
N_Q_HEADS = 12
N_KV_HEADS = 4
ATTN_W = N_Q_HEADS * HEAD_DIM
KV_W = N_KV_HEADS * HEAD_DIM
FOURIER_W = D_MODEL // 4
N_FOURIER_GROUPS = 4
FOURIER_GROUP_W = FOURIER_W // N_FOURIER_GROUPS
IN_W = ATTN_W + 2 * KV_W + FOURIER_W + 2 * D_MODEL
ROPE_AXIS_DIM = HEAD_DIM // 2
ROPE_THETA = 10000.0
N_EXPERTS = 32
TOP_K = 4
D_EXPERT = D_MODEL
SWIGLU_ALPHA = 1.702
SWIGLU_LIMIT = 7.0
NORM_EPS = 1e-6

N_CTX = BATCH * SEQ
N_LAT = DEC_BATCH * DEC_SEQ
N_TOK = N_CTX + N_LAT
LANES = 128
ROW_TILE = 256
IN_TILE = 1024
LAT_MERGE_TILE = 512
ROUTER_TILE = 512
LAT_Q_TILE = 256
N_TILES = N_TOK // ROW_TILE
CTX_TILES = N_CTX // ROW_TILE
LAT_TILES_PER_BATCH = DEC_SEQ // ROW_TILE
N_GROUPS = 1 + DEC_BATCH
EXPERT_TILE = 256
N_ASSIGN = N_TOK * TOP_K
N_EXPERT_BLOCKS = N_ASSIGN // EXPERT_TILE + N_EXPERTS
M_PAD = N_EXPERT_BLOCKS * EXPERT_TILE
DMA_CHUNK = 32
DISPATCH_SLOTS = 3
DFT_ROW_BLOCK = 64
Q_CHUNKS = ATTN_W // LANES
NEG_BIG = -1e30
VMEM_LIMIT = 56 * 1024 * 1024

_NT = (((1,), (1,)), ((), ()))


def _params(sem, vmem=VMEM_LIMIT):
    return pltpu.CompilerParams(dimension_semantics=sem, vmem_limit_bytes=vmem)


def _tile_group(i, tile=ROW_TILE):
    ctx_tiles = N_CTX // tile
    return jnp.where(i < ctx_tiles, 0, 1 + (i - ctx_tiles) // (DEC_SEQ // tile))


def _rope_block(i, tile):
    ctx_tiles = N_CTX // tile
    return jnp.where(i < ctx_tiles, 0, 1 + (i - ctx_tiles) % (DEC_SEQ // tile))


def _mod_spec(layer, slot, tile=ROW_TILE):
    return pl.BlockSpec((None, 1, D_MODEL),
                        lambda i: ((layer * N_GROUPS + _tile_group(i, tile)) * 6 + slot, 0, 0))


def _layer_vec_spec(layer, width=D_MODEL):
    return pl.BlockSpec((None, 1, width), lambda *_: (layer, 0, 0))


ROW_SUBLANES = D_MODEL // LANES


def _store_token_rows(ref, value):
    n = value.shape[0]
    for s in range(ROW_SUBLANES):
        ref[pl.ds(s, n, stride=ROW_SUBLANES), :] = value[:, s * LANES:(s + 1) * LANES]


def _load_token_rows(ref):
    n = ref.shape[0] // ROW_SUBLANES
    return jnp.concatenate([ref[pl.ds(s, n, stride=ROW_SUBLANES), :] for s in range(ROW_SUBLANES)], axis=1)


def _token_rows(token):
    return pl.ds(pl.multiple_of(token * ROW_SUBLANES, ROW_SUBLANES), ROW_SUBLANES)


def _mods_kernel(cond_ref, w_ref, b_ref, o_ref):
    c = cond_ref[...]
    s = c * jax.nn.sigmoid(c)
    w = w_ref[...]
    o_ref[...] = jnp.broadcast_to(b_ref[...], o_ref.shape)
    for g in range(N_GROUPS):
        o_ref[g:g + 1, :] = jnp.sum(w * s[:, g:g + 1], axis=0, keepdims=True) + b_ref[...]


def _mods(cond, w_ada, b_ada):
    tn = 1536
    return pl.pallas_call(
        _mods_kernel,
        grid=(DEPTH, 6 * D_MODEL // tn),
        in_specs=[pl.BlockSpec((D_MODEL, 8), lambda l, j: (0, 0)),
                  pl.BlockSpec((None, D_MODEL, tn), lambda l, j: (l, 0, j)),
                  pl.BlockSpec((None, 1, tn), lambda l, j: (l, 0, j))],
        out_specs=pl.BlockSpec((None, 8, tn), lambda l, j: (l, 0, j)),
        out_shape=jax.ShapeDtypeStruct((DEPTH, 8, 6 * D_MODEL), F32),
        compiler_params=_params(("parallel", "parallel")),
        name="adaln_mods",
    )(cond, w_ada, b_ada.reshape(DEPTH, 1, 6 * D_MODEL))


def _inproj_kernel(x_ref, g_ref, sh_ref, sc_ref, w_ref, qg_ref, kg_ref, cos_ref, sin_ref,
                   bd_ref, cs_ref,
                   kc_in, vc_in,
                   q_ref, kx_ref, vx_ref, kf_ref, vf_ref, fcs_ref, ga_ref, gf_ref, h_scr):
    del kc_in, vc_in
    is_ctx = pl.program_id(0) < N_CTX // IN_TILE
    x = x_ref[...]
    y = x * lax.rsqrt(jnp.mean(x * x, axis=-1, keepdims=True) + NORM_EPS) * g_ref[...]
    h_scr[...] = (y * (1.0 + sc_ref[...]) + sh_ref[...]).astype(BF16)
    hb = h_scr[...]
    lane = lax.broadcasted_iota(I32, (x_ref.shape[0], LANES), 1)
    low_half = lane < HEAD_DIM
    rot_first = (lane % ROPE_AXIS_DIM) < (ROPE_AXIS_DIM // 2)
    cos = cos_ref[...]
    sin = sin_ref[...]
    bd = bd_ref[...]

    def proj(lo, width):
        return jnp.dot(hb, w_ref[:, lo:lo + width], preferred_element_type=F32)

    def head_norm_rope(p, gain):
        pp = p * p
        hi = pp.astype(BF16)
        lo = (pp - hi.astype(F32)).astype(BF16)
        msq = jnp.dot(jnp.concatenate([hi, lo], axis=1), bd, preferred_element_type=F32)
        n = p * lax.rsqrt(msq + NORM_EPS) * gain
        partner = jnp.where(rot_first, pltpu.roll(n, LANES - ROPE_AXIS_DIM // 2, 1),
                            pltpu.roll(n, ROPE_AXIS_DIM // 2, 1))
        return n * cos + partner * sin

    def head_pairs(r):
        sw = pltpu.roll(r, HEAD_DIM, 1)
        return jnp.where(low_half, r, sw), r, jnp.where(low_half, sw, r)

    scale = HEAD_DIM ** -0.5
    wide = 2 * LANES
    for c2 in range(ATTN_W // wide):
        p = proj(c2 * wide, wide)
        for t in range(2):
            c = 2 * c2 + t
            r = head_norm_rope(p[:, t * LANES:(t + 1) * LANES], qg_ref[...])
            q_ref[:, c * LANES:(c + 1) * LANES] = (r * scale).astype(BF16)

    p = proj(ATTN_W, KV_W)
    for j in range(KV_W // LANES):
        r = head_norm_rope(p[:, j * LANES:(j + 1) * LANES], kg_ref[...])
        @pl.when(is_ctx)
        def _():
            kf_ref[:, :, j * LANES:(j + 1) * LANES] = r.reshape(kf_ref.shape[0], SEQ, LANES)

        for t, piece in enumerate(head_pairs(r)):
            c = 3 * j + t
            kx_ref[:, c * LANES:(c + 1) * LANES] = piece.astype(BF16)

    p = proj(ATTN_W + KV_W, KV_W)
    @pl.when(is_ctx)
    def _():
        vf_ref[...] = p.reshape(vf_ref.shape)

    for j in range(KV_W // LANES):
        for t, piece in enumerate(head_pairs(p[:, j * LANES:(j + 1) * LANES])):
            c = 3 * j + t
            vx_ref[:, c * LANES:(c + 1) * LANES] = piece.astype(BF16)

    f = proj(ATTN_W + 2 * KV_W, FOURIER_W)
    fcs_ref[...] = jnp.dot(f.astype(BF16), cs_ref[...], preferred_element_type=F32).astype(BF16)

    gate_lo = ATTN_W + 2 * KV_W + FOURIER_W
    half = D_MODEL // 2
    for j in range(2):
        ga_ref[:, j * half:(j + 1) * half] = jax.nn.sigmoid(proj(gate_lo + j * half, half)).astype(BF16)
        gf_ref[:, j * half:(j + 1) * half] = jax.nn.sigmoid(
            proj(gate_lo + D_MODEL + j * half, half)).astype(BF16)


def _inproj(layer, x, norm_g, mods, w_in, qg, kg, cos_tab, sin_tab, bd, cs, k_cache, v_cache):
    row = lambda w: pl.BlockSpec((IN_TILE, w), lambda i: (i, 0))
    const = lambda a: pl.BlockSpec(a.shape, lambda i: (0,) * a.ndim)
    rope = pl.BlockSpec((IN_TILE, LANES), lambda i: (_rope_block(i, IN_TILE), 0))
    ext_w = Q_CHUNKS * LANES
    last_ctx = N_CTX // IN_TILE - 1
    cache = pl.BlockSpec((IN_TILE // SEQ, None, SEQ, KV_W), lambda i: (jnp.minimum(i, last_ctx), layer, 0, 0))
    cache_shape = jax.ShapeDtypeStruct((BATCH, DEPTH, SEQ, KV_W), F32)
    hbm = pl.BlockSpec(memory_space=pl.ANY)
    return pl.pallas_call(
        _inproj_kernel,
        grid=(N_TOK // IN_TILE,),
        in_specs=[row(D_MODEL), _layer_vec_spec(layer), _mod_spec(layer, 0, IN_TILE), _mod_spec(layer, 1, IN_TILE),
                  pl.BlockSpec((None, D_MODEL, IN_W), lambda i: (layer, 0, 0)),
                  _layer_vec_spec(layer, LANES), _layer_vec_spec(layer, LANES), rope, rope,
                  const(bd), const(cs), hbm, hbm],
        out_specs=[row(ATTN_W), row(ext_w), row(ext_w), cache, cache,
                   row(2 * FOURIER_W), row(D_MODEL), row(D_MODEL)],
        out_shape=[jax.ShapeDtypeStruct((N_TOK, ATTN_W), BF16),
                   jax.ShapeDtypeStruct((N_TOK, ext_w), BF16),
                   jax.ShapeDtypeStruct((N_TOK, ext_w), BF16),
                   cache_shape, cache_shape,
                   jax.ShapeDtypeStruct((N_TOK, 2 * FOURIER_W), BF16),
                   jax.ShapeDtypeStruct((N_TOK, D_MODEL), BF16),
                   jax.ShapeDtypeStruct((N_TOK, D_MODEL), BF16)],
        scratch_shapes=[pltpu.VMEM((IN_TILE, D_MODEL), BF16)],
        input_output_aliases={11: 3, 12: 4},
        compiler_params=_params(("arbitrary",)),
        name=f"inproj_l{layer}",
    )(x, norm_g, mods, mods, w_in, qg, kg, cos_tab, sin_tab, bd, cs, k_cache, v_cache)


def _attn_kernel(*refs, n_parts):
    q_ref = refs[0]
    k_refs = refs[1:1 + n_parts]
    v_refs = refs[1 + n_parts:1 + 2 * n_parts]
    o_ref = refs[-1]
    tq = q_ref.shape[0]
    low_half = lax.broadcasted_iota(I32, (tq, LANES), 1) < HEAD_DIM
    for c in range(Q_CHUNKS):
        cols = slice(c * LANES, (c + 1) * LANES)
        qc = q_ref[:, cols]
        zero = jnp.zeros_like(qc)
        outs = []
        for mask in (low_half, jnp.logical_not(low_half)):
            qm = jnp.where(mask, qc, zero)
            ss = [lax.dot_general(qm, k[:, cols], _NT, preferred_element_type=F32) for k in k_refs]
            m = ss[0].max(axis=-1, keepdims=True)
            for s in ss[1:]:
                m = jnp.maximum(m, s.max(axis=-1, keepdims=True))
            ps = [jnp.exp(s - m) for s in ss]
            den = ps[0].sum(axis=-1, keepdims=True)
            for p in ps[1:]:
                den = den + p.sum(axis=-1, keepdims=True)
            acc = jnp.dot(ps[0].astype(BF16), v_refs[0][:, cols], preferred_element_type=F32)
            for p, v in zip(ps[1:], v_refs[1:]):
                acc = acc + jnp.dot(p.astype(BF16), v[:, cols], preferred_element_type=F32)
            outs.append(acc / den)
        o_ref[:, cols] = jnp.where(low_half, outs[0], outs[1]).astype(BF16)


def _attention_ctx(layer, q, kx, vx):
    w = Q_CHUNKS * LANES
    blk = lambda width: pl.BlockSpec((SEQ, width), lambda b: (b, 0))
    return pl.pallas_call(
        functools.partial(_attn_kernel, n_parts=1),
        grid=(BATCH,),
        in_specs=[blk(ATTN_W), blk(w), blk(w)],
        out_specs=blk(ATTN_W),
        out_shape=jax.ShapeDtypeStruct((N_TOK, ATTN_W), BF16),
        input_output_aliases={0: 0},
        compiler_params=_params(("parallel",)),
        name=f"attn_ctx_l{layer}",
    )(q, kx, vx)


def _attention_lat(layer, q, kx, vx, kcache, vcache):
    w = Q_CHUNKS * LANES
    nq = DEC_SEQ // LAT_Q_TILE
    first = N_CTX // LAT_Q_TILE
    qblk = pl.BlockSpec((LAT_Q_TILE, ATTN_W), lambda b, i: (first + b * nq + i, 0))
    new = lambda width: pl.BlockSpec((DEC_SEQ, width), lambda b, i: (N_CTX // DEC_SEQ + b, 0))
    old = lambda width: pl.BlockSpec((None, None, PAST_LEN, width), lambda b, i: (b, layer, 0, 0))
    return pl.pallas_call(
        functools.partial(_attn_kernel, n_parts=2),
        grid=(DEC_BATCH, nq),
        in_specs=[qblk, old(w), new(w), old(w), new(w)],
        out_specs=qblk,
        out_shape=jax.ShapeDtypeStruct((N_TOK, ATTN_W), BF16),
        input_output_aliases={0: 0},
        compiler_params=_params(("parallel", "parallel")),
        name=f"attn_lat_l{layer}",
    )(q, kcache, kx, vcache, vx)


def _merge_kernel(a_ref, fc_ref, fs_ref, cl_ref, sl_ref, ga_ref, gf_ref, x_ref, gate_ref,
                  wa_ref, wf_ref, wo_ref, o_ref):
    attn = jnp.dot(a_ref[...], wa_ref[...], preferred_element_type=F32)
    fo = (jnp.dot(cl_ref[...], fc_ref[...], preferred_element_type=F32)
          - jnp.dot(sl_ref[...], fs_ref[...], preferred_element_type=F32))
    four = jnp.dot(fo.astype(BF16), wf_ref[...], preferred_element_type=F32)
    merged = ga_ref[...].astype(F32) * attn + gf_ref[...].astype(F32) * four
    mix = jnp.dot(merged.astype(BF16), wo_ref[...], preferred_element_type=F32)
    o_ref[...] = x_ref[...] + gate_ref[...] * mix


def _merge(layer, stream, a, fcs, cl, sl, ga, gf, x, mods, wa, wf, wo):
    if stream == "ctx":
        seq, n_batch, first_row, rows = SEQ, BATCH, 0, SEQ
    else:
        seq, n_batch, first_row, rows = DEC_SEQ, DEC_BATCH, N_CTX, LAT_MERGE_TILE
    nt = seq // rows
    tile = lambda b, i: first_row // rows + b * nt + i
    row = lambda w: pl.BlockSpec((rows, w), lambda b, i: (tile(b, i), 0))
    wspec = lambda a_: pl.BlockSpec((None,) + a_.shape[1:], lambda b, i: (layer, 0, 0))
    in_specs = [row(ATTN_W),
                pl.BlockSpec((seq, FOURIER_W), lambda b, i: (first_row // seq + b, 0)),
                pl.BlockSpec((seq, FOURIER_W), lambda b, i: (first_row // seq + b, 1)),
                pl.BlockSpec((rows, seq), lambda b, i: (i, 0)),
                pl.BlockSpec((rows, seq), lambda b, i: (i, 0)),
                row(D_MODEL), row(D_MODEL), row(D_MODEL),
                pl.BlockSpec((None, 1, D_MODEL),
                             lambda b, i: ((layer * N_GROUPS + _tile_group(tile(b, i), rows)) * 6 + 2, 0, 0)),
                wspec(wa), wspec(wf), wspec(wo)]
    args = [a, fcs, fcs, cl, sl, ga, gf, x, mods, wa, wf, wo]
    aliases = {7: 0}
    return pl.pallas_call(
        _merge_kernel,
        grid=(n_batch, nt),
        in_specs=in_specs,
        out_specs=row(D_MODEL),
        out_shape=jax.ShapeDtypeStruct((N_TOK, D_MODEL), F32),
        input_output_aliases=aliases,
        compiler_params=_params(("parallel", "parallel")),
        name=f"merge_{stream}_l{layer}",
    )(*args)


def _router_kernel(x_ref, g_ref, sh_ref, sc_ref, wr_ref, br_ref, tri_ref,
                   h_ref, te_ref, tg_ref, rk_ref, cnt_ref, carry):
    @pl.when(pl.program_id(0) == 0)
    def _():
        carry[...] = jnp.zeros_like(carry)

    x = x_ref[...]
    y = x * lax.rsqrt(jnp.mean(x * x, axis=-1, keepdims=True) + NORM_EPS) * g_ref[...]
    h = y * (1.0 + sc_ref[...]) + sh_ref[...]
    _store_token_rows(h_ref, h)
    logits = lax.dot_general(wr_ref[...], h, _NT, preferred_element_type=F32,
                             precision=lax.Precision.HIGHEST) + br_ref[...]
    sub = lax.broadcasted_iota(I32, logits.shape, 0).astype(F32)
    vals, ids = [], []
    for _ in range(TOP_K):
        m = logits.max(axis=0, keepdims=True)
        idx = jnp.where(logits == m, sub, float(N_EXPERTS)).min(axis=0, keepdims=True)
        vals.append(m)
        ids.append(idx)
        logits = jnp.where(sub == idx, NEG_BIG, logits)
    es = [jnp.exp(v - vals[0]) for v in vals]
    den = es[0] + es[1] + es[2] + es[3]

    chosen = [sub == idx for idx in ids]
    member = jnp.zeros_like(sub)
    for ch in chosen:
        member = member + ch.astype(F32)
    upto = jnp.dot(member.astype(BF16), tri_ref[...], preferred_element_type=F32)
    before = carry[...] + upto - member
    for k in range(TOP_K):
        te_ref[k:k + 1, :] = ids[k].astype(I32)
        tg_ref[k:k + 1, :] = es[k] / den
        rk_ref[k:k + 1, :] = jnp.where(chosen[k], before, 0.0).sum(axis=0, keepdims=True).astype(I32)
    carry[...] = carry[...] + member.sum(axis=1, keepdims=True)
    cnt_ref[...] = carry[:, :LANES].astype(I32)


def _router(layer, x1, norm_g, mods, wr_t, br_t, tri):
    row = lambda w: pl.BlockSpec((ROUTER_TILE, w), lambda i: (i, 0))
    col = pl.BlockSpec((TOP_K, ROUTER_TILE), lambda i: (0, i))
    return pl.pallas_call(
        _router_kernel,
        grid=(N_TOK // ROUTER_TILE,),
        in_specs=[row(D_MODEL), _layer_vec_spec(layer), _mod_spec(layer, 3, ROUTER_TILE), _mod_spec(layer, 4, ROUTER_TILE),
                  pl.BlockSpec((None, N_EXPERTS, D_MODEL), lambda i: (layer, 0, 0)),
                  pl.BlockSpec((None, N_EXPERTS, 1), lambda i: (layer, 0, 0)),
                  pl.BlockSpec((ROUTER_TILE, ROUTER_TILE), lambda i: (0, 0))],
        out_specs=[pl.BlockSpec((ROUTER_TILE * ROW_SUBLANES, LANES), lambda i: (i, 0)), col, col, col,
                   pl.BlockSpec((N_EXPERTS, LANES), lambda i: (0, 0))],
        out_shape=[jax.ShapeDtypeStruct((N_TOK * ROW_SUBLANES, LANES), F32),
                   jax.ShapeDtypeStruct((TOP_K, N_TOK), I32),
                   jax.ShapeDtypeStruct((TOP_K, N_TOK), F32),
                   jax.ShapeDtypeStruct((TOP_K, N_TOK), I32),
                   jax.ShapeDtypeStruct((N_EXPERTS, LANES), I32)],
        scratch_shapes=[pltpu.VMEM((N_EXPERTS, ROUTER_TILE), F32)],
        compiler_params=_params(("arbitrary",)),
        name=f"router_l{layer}",
    )(x1, norm_g, mods, mods, wr_t, br_t, tri)


def _dispatch_kernel(dest_ref, zblk_ref, nblk_ref, h_hbm, xs_hbm, zeros, hbuf, zsem, tsem, hsems, csems):
    i = pl.program_id(0)
    n_steps = pl.num_programs(0)
    block_rows = EXPERT_TILE * ROW_SUBLANES
    tile_rows = ROW_TILE * ROW_SUBLANES

    def zero_fill(block, sem):
        rows = pl.ds(pl.multiple_of(block * block_rows, block_rows), block_rows)
        return pltpu.make_async_copy(zeros, xs_hbm.at[rows, :], sem)

    def unused_blocks(wait):
        def body(j, carry):
            cp = zero_fill(j, tsem)
            cp.wait() if wait else cp.start()
            return carry
        lax.fori_loop(nblk_ref[0], N_EXPERT_BLOCKS, body, 0)

    def fetch(tile):
        rows = pl.ds(pl.multiple_of(tile * tile_rows, tile_rows), tile_rows)
        slot = tile % DISPATCH_SLOTS
        return pltpu.make_async_copy(h_hbm.at[rows, :], hbuf.at[slot], hsems.at[slot])

    def row_copies(tile, chunk):
        slot = tile % DISPATCH_SLOTS
        copies = []
        for j in range(DMA_CHUNK):
            r = chunk * DMA_CHUNK + j
            for k in range(TOP_K):
                d = dest_ref[k * N_TOK + tile * ROW_TILE + r]
                copies.append(pltpu.make_async_copy(hbuf.at[slot, _token_rows(r), :],
                                                    xs_hbm.at[_token_rows(d), :], csems.at[slot]))
        return copies

    def start_tile(tile):
        def body(chunk, carry):
            for n, cp in enumerate(row_copies(tile, chunk)):
                cp.start(priority=n % 2)
            return carry
        lax.fori_loop(0, ROW_TILE // DMA_CHUNK, body, 0)

    def wait_tile(tile):
        def body(chunk, carry):
            for cp in row_copies(tile, chunk):
                cp.wait()
            return carry
        lax.fori_loop(0, ROW_TILE // DMA_CHUNK, body, 0)

    @pl.when(i == 0)
    def _():
        fetch(0).start()
        fetch(1).start()
        zeros[...] = jnp.zeros_like(zeros)
        unused_blocks(wait=False)
        for wait in (False, True):
            for e in range(N_EXPERTS):
                @pl.when(zblk_ref[e] >= 0)
                def _():
                    cp = zero_fill(zblk_ref[e], zsem)
                    cp.wait() if wait else cp.start()

    fetch(i).wait()
    start_tile(i)

    @pl.when(i >= 1)
    def _():
        wait_tile(i - 1)

    @pl.when(i + 2 < n_steps)
    def _():
        fetch(i + 2).start()

    @pl.when(i == n_steps - 1)
    def _():
        wait_tile(i)
        unused_blocks(wait=True)


def _dispatch(layer, dest, zblk, nblk, h2):
    grid_spec = pltpu.PrefetchScalarGridSpec(
        num_scalar_prefetch=3,
        grid=(N_TILES,),
        in_specs=[pl.BlockSpec(memory_space=pl.ANY)],
        out_specs=pl.BlockSpec(memory_space=pl.ANY),
        scratch_shapes=[pltpu.VMEM((EXPERT_TILE * ROW_SUBLANES, LANES), F32),
                        pltpu.VMEM((DISPATCH_SLOTS, ROW_TILE * ROW_SUBLANES, LANES), F32),
                        pltpu.SemaphoreType.DMA(()),
                        pltpu.SemaphoreType.DMA(()),
                        pltpu.SemaphoreType.DMA((DISPATCH_SLOTS,)),
                        pltpu.SemaphoreType.DMA((DISPATCH_SLOTS,))],
    )
    return pl.pallas_call(
        _dispatch_kernel,
        grid_spec=grid_spec,
        out_shape=jax.ShapeDtypeStruct((M_PAD * ROW_SUBLANES, LANES), F32),
        compiler_params=_params(("arbitrary",)),
        name=f"dispatch_l{layer}",
    )(dest, zblk, nblk, h2)


def _expert_kernel(be_ref, kind_ref, slot_ref, next_ref, x_ref, wi_hbm, bi_ref, wo_hbm, bo_ref, y_ref,
                   wi_f32, wo_f32, wi_bf, wo_bf, sems, *, layer):
    b = pl.program_id(0)

    def weight_copies(expert, slot):
        return (pltpu.make_async_copy(wi_hbm.at[layer, expert], wi_f32.at[slot], sems.at[0, slot]),
                pltpu.make_async_copy(wo_hbm.at[layer, expert], wo_f32.at[slot], sems.at[1, slot]))

    @pl.when(kind_ref[b] == 0)
    def _():
        y_ref[...] = jnp.zeros_like(y_ref)

    @pl.when(kind_ref[b] == 2)
    def _():
        slot = slot_ref[b]
        own = weight_copies(be_ref[b], slot)

        @pl.when(b == 0)
        def _():
            for cp in own:
                cp.start()

        for cp in own:
            cp.wait()

        @pl.when(next_ref[b] >= 0)
        def _():
            for cp in weight_copies(next_ref[b], 1 - slot):
                cp.start(priority=1)

        wi_bf[...] = wi_f32[slot].astype(BF16)
        wo_bf[...] = wo_f32[slot].astype(BF16)

    @pl.when(kind_ref[b] > 0)
    def _():
        x = _load_token_rows(x_ref).astype(BF16)
        hdn = jnp.dot(x, wi_bf[...], preferred_element_type=F32) + bi_ref[...]
        glu = jnp.minimum(hdn[:, :D_EXPERT], SWIGLU_LIMIT)
        lin = jnp.clip(hdn[:, D_EXPERT:], -SWIGLU_LIMIT, SWIGLU_LIMIT)
        act = glu * jax.nn.sigmoid(SWIGLU_ALPHA * glu) * (lin + 1.0)
        y = jnp.dot(act.astype(BF16), wo_bf[...], preferred_element_type=F32) + bo_ref[...]
        _store_token_rows(y_ref, y)


def _experts(layer, blk_e, blk_kind, blk_slot, blk_next, xs, w_exp_in, b_exp_in, w_exp_out, b_exp_out):
    bias = lambda cols: pl.BlockSpec((None, None, 1, cols), lambda b, be, *_: (layer, be[b], 0, 0))
    blk = pl.BlockSpec((EXPERT_TILE * ROW_SUBLANES, LANES), lambda b, *_: (b, 0))
    hbm = pl.BlockSpec(memory_space=pl.ANY)
    grid_spec = pltpu.PrefetchScalarGridSpec(
        num_scalar_prefetch=4,
        grid=(N_EXPERT_BLOCKS,),
        in_specs=[blk, hbm, bias(2 * D_EXPERT), hbm, bias(D_MODEL)],
        out_specs=blk,
        scratch_shapes=[pltpu.VMEM((2, D_MODEL, 2 * D_EXPERT), F32),
                        pltpu.VMEM((2, D_EXPERT, D_MODEL), F32),
                        pltpu.VMEM((D_MODEL, 2 * D_EXPERT), BF16),
                        pltpu.VMEM((D_EXPERT, D_MODEL), BF16),
                        pltpu.SemaphoreType.DMA((2, 2))],
    )
    return pl.pallas_call(
        functools.partial(_expert_kernel, layer=layer),
        grid_spec=grid_spec,
        out_shape=jax.ShapeDtypeStruct((M_PAD * ROW_SUBLANES, LANES), F32),
        compiler_params=_params(("arbitrary",)),
        name=f"experts_l{layer}",
    )(blk_e, blk_kind, blk_slot, blk_next, xs, w_exp_in,
      b_exp_in.reshape(DEPTH, N_EXPERTS, 1, 2 * D_EXPERT), w_exp_out,
      b_exp_out.reshape(DEPTH, N_EXPERTS, 1, D_MODEL))


def _combine_kernel(dest_ref, y_hbm, x_ref, tg_ref, gate_ref, *rest, final):
    if final:
        fg_ref, o_ref, nc_ref, nl_ref, buf, sems = rest
    else:
        o_ref, buf, sems = rest
    i = pl.program_id(0)
    slot = i % 2

    def row_copies(tile, slot, chunk):
        copies = []
        for j in range(DMA_CHUNK):
            r = chunk * DMA_CHUNK + j
            for k in range(TOP_K):
                src = y_hbm.at[_token_rows(dest_ref[k * N_TOK + tile * ROW_TILE + r]), :]
                copies.append(pltpu.make_async_copy(src, buf.at[slot, k, _token_rows(r), :], sems.at[slot]))
        return copies

    def start_tile(tile, slot):
        def body(chunk, carry):
            for n, cp in enumerate(row_copies(tile, slot, chunk)):
                cp.start(priority=n % 2)
            return carry
        lax.fori_loop(0, ROW_TILE // DMA_CHUNK, body, 0)

    def wait_tile(tile, slot):
        def body(chunk, carry):
            for cp in row_copies(tile, slot, chunk):
                cp.wait()
            return carry
        lax.fori_loop(0, ROW_TILE // DMA_CHUNK, body, 0)

    @pl.when(i == 0)
    def _():
        start_tile(0, 0)

    @pl.when(i + 1 < pl.num_programs(0))
    def _():
        start_tile(i + 1, 1 - slot)

    wait_tile(i, slot)

    tg = tg_ref[...]
    acc = tg[:, 0:1] * _load_token_rows(buf.at[slot, 0])
    for k in range(1, TOP_K):
        acc = acc + tg[:, k:k + 1] * _load_token_rows(buf.at[slot, k])
    out = x_ref[...] + gate_ref[...] * acc
    o_ref[...] = out
    if final:
        normed = out * lax.rsqrt(jnp.mean(out * out, axis=-1, keepdims=True) + NORM_EPS) * fg_ref[...]

        @pl.when(i < CTX_TILES)
        def _():
            nc_ref[...] = normed

        @pl.when(i >= CTX_TILES)
        def _():
            nl_ref[...] = normed


def _combine(layer, dest, yb, x1, tg, mods, final_g=None):
    final = final_g is not None
    row = lambda w: pl.BlockSpec((ROW_TILE, w), lambda i, d: (i, 0))
    in_specs = [pl.BlockSpec(memory_space=pl.ANY), row(D_MODEL), row(TOP_K),
                pl.BlockSpec((None, 1, D_MODEL),
                             lambda i, d: ((layer * N_GROUPS + _tile_group(i)) * 6 + 5, 0, 0))]
    args = [dest, yb, x1, tg, mods]
    out_specs = [row(D_MODEL)]
    out_shape = [jax.ShapeDtypeStruct((N_TOK, D_MODEL), F32)]
    if final:
        in_specs.append(pl.BlockSpec((1, D_MODEL), lambda i, d: (0, 0)))
        args.append(final_g)
        out_specs.append(pl.BlockSpec((ROW_TILE, D_MODEL), lambda i, d: (jnp.minimum(i, CTX_TILES - 1), 0)))
        out_specs.append(pl.BlockSpec((ROW_TILE, D_MODEL), lambda i, d: (jnp.maximum(i - CTX_TILES, 0), 0)))
        out_shape.append(jax.ShapeDtypeStruct((N_CTX, D_MODEL), F32))
        out_shape.append(jax.ShapeDtypeStruct((N_LAT, D_MODEL), F32))
    grid_spec = pltpu.PrefetchScalarGridSpec(
        num_scalar_prefetch=1,
        grid=(N_TILES,),
        in_specs=in_specs,
        out_specs=out_specs,
        scratch_shapes=[pltpu.VMEM((2, TOP_K, ROW_TILE * ROW_SUBLANES, LANES), F32),
                        pltpu.SemaphoreType.DMA((2,))],
    )
    return pl.pallas_call(
        functools.partial(_combine_kernel, final=final),
        grid_spec=grid_spec,
        out_shape=out_shape,
        compiler_params=_params(("arbitrary",)),
        name=f"combine_l{layer}",
    )(*args)


def _rope_tables():
    rows = DEC_SEQ // GRID_W
    row = jnp.repeat(jnp.arange(rows, dtype=F32), GRID_W)
    col = jnp.tile(jnp.arange(GRID_W, dtype=F32), rows)
    inv_freq = ROPE_THETA ** (-jnp.arange(0, ROPE_AXIS_DIM, 2, dtype=F32) / ROPE_AXIS_DIM)
    ang = jnp.stack([row[:, None] * inv_freq, col[:, None] * inv_freq], axis=1)
    cos, sin = jnp.cos(ang), jnp.sin(ang)
    cos_h = jnp.concatenate([cos, cos], axis=-1).reshape(DEC_SEQ, HEAD_DIM)
    sin_h = jnp.concatenate([-sin, sin], axis=-1).reshape(DEC_SEQ, HEAD_DIM)
    reps = LANES // HEAD_DIM
    cos_t = jnp.concatenate([jnp.ones((IN_TILE, LANES), F32), jnp.tile(cos_h, (1, reps))], axis=0)
    sin_t = jnp.concatenate([jnp.zeros((IN_TILE, LANES), F32), jnp.tile(sin_h, (1, reps))], axis=0)
    return cos_t, sin_t


def _dft_tables(n, scale):
    col = jnp.arange(n, dtype=I32)

    def direct(rows):
        ang = ((rows[:, None] * col[None, :]) % n).astype(F32) * (2.0 * math.pi / n)
        return jnp.cos(ang), jnp.sin(ang)

    if n <= DFT_ROW_BLOCK:
        c, s = direct(col)
        return c * scale, s * scale
    lo_c, lo_s = direct(jnp.arange(DFT_ROW_BLOCK, dtype=I32))
    hi_c, hi_s = direct(jnp.arange(n // DFT_ROW_BLOCK, dtype=I32) * DFT_ROW_BLOCK)
    hi_c, hi_s = (hi_c * scale)[:, None, :], (hi_s * scale)[:, None, :]
    c = hi_c * lo_c[None] - hi_s * lo_s[None]
    s = hi_s * lo_c[None] + hi_c * lo_s[None]
    return c.reshape(n, n), s.reshape(n, n)


def _channel_dft():
    c, s = _dft_tables(FOURIER_GROUP_W, 1.0)
    eye = jnp.eye(N_FOURIER_GROUPS, dtype=F32)
    return jnp.concatenate([jnp.kron(eye, c), jnp.kron(eye, s)], axis=1).astype(BF16)


def _extend_heads(t):
    h = [t[..., i * HEAD_DIM:(i + 1) * HEAD_DIM] for i in range(N_KV_HEADS)]
    order = [0, 0, 0, 1, 1, 1, 2, 2, 2, 3, 3, 3]
    return jnp.concatenate([h[i] for i in order], axis=-1)


def _plan(top_e_t, rank_t, counts):
    cnt = counts[:N_EXPERTS, 0]
    padded = (cnt + EXPERT_TILE - 1) // EXPERT_TILE * EXPERT_TILE
    pad_end = jnp.cumsum(padded)
    pad_start = pad_end - padded
    experts = jnp.arange(N_EXPERTS, dtype=I32)
    start_of = jnp.sum(jnp.where(top_e_t[..., None] == experts, pad_start, 0), axis=-1)
    dest = (start_of + rank_t).reshape(-1).astype(I32)
    blk_start = jnp.arange(N_EXPERT_BLOCKS, dtype=I32) * EXPERT_TILE
    blk_e = jnp.minimum(jnp.sum(pad_end[None, :] <= blk_start[:, None], axis=1), N_EXPERTS - 1).astype(I32)
    valid = blk_start < pad_end[-1]
    first = jnp.logical_and(valid, jnp.concatenate([jnp.ones((1,), bool), blk_e[1:] != blk_e[:-1]]))
    blk_kind = (valid.astype(I32) + first.astype(I32)).astype(I32)
    blk_slot = ((jnp.cumsum(first.astype(I32)) - 1) % 2).astype(I32)
    later = jnp.logical_and(experts[None, :] > experts[:, None], cnt[None, :] > 0)
    next_of = jnp.min(jnp.where(later, experts[None, :], N_EXPERTS), axis=1)
    next_of = jnp.where(next_of == N_EXPERTS, -1, next_of)
    blk_next = jnp.sum(jnp.where(blk_e[:, None] == experts[None, :], next_of[None, :], 0), axis=1).astype(I32)
    zero_blk = jnp.where(cnt % EXPERT_TILE != 0, (pad_start + cnt) // EXPERT_TILE, -1).astype(I32)
    n_blk = (pad_end[-1:] // EXPERT_TILE).astype(I32)
    return dest, (blk_e, blk_kind, blk_slot, blk_next), zero_blk, n_blk


def kernel(x_prompt, x_sample, cache_k, cache_v, c, c_ctx, w_ada, b_ada, norm1_g, w_in, q_norm_g,
           k_norm_g, w_attn_o, w_fourier_o, w_out, norm2_g, w_router, b_router, w_exp_in, b_exp_in,
           w_exp_out, b_exp_out, final_norm_g):
    x = jnp.concatenate([x_prompt.reshape(N_CTX, D_MODEL), x_sample.reshape(N_LAT, D_MODEL)], axis=0)

    cond = jnp.concatenate([c_ctx[None, :], c, jnp.zeros((8 - N_GROUPS, D_MODEL), F32)], axis=0)
    mods = _mods(cond.T, w_ada, b_ada)[:, :N_GROUPS].reshape(DEPTH * N_GROUPS * 6, 1, D_MODEL)

    cos_t, sin_t = _rope_tables()
    bd = jnp.kron(jnp.eye(LANES // HEAD_DIM, dtype=F32),
                  jnp.full((HEAD_DIM, HEAD_DIM), 1.0 / HEAD_DIM, F32)).astype(BF16)
    bd = jnp.concatenate([bd, bd], axis=0)
    cs = _channel_dft()
    dft_ctx = [t.astype(BF16) for t in _dft_tables(SEQ, (SEQ * FOURIER_GROUP_W) ** -0.5)]
    dft_lat = [t.astype(BF16) for t in _dft_tables(DEC_SEQ, (DEC_SEQ * FOURIER_GROUP_W) ** -0.5)]

    w_in_b = w_in.astype(BF16)
    wa_b = w_attn_o.astype(BF16)
    wf_b = w_fourier_o.astype(BF16)
    wo_b = w_out.astype(BF16)
    norm1 = norm1_g.reshape(DEPTH, 1, D_MODEL)
    norm2 = norm2_g.reshape(DEPTH, 1, D_MODEL)
    qg = jnp.tile(q_norm_g, (1, LANES // HEAD_DIM)).reshape(DEPTH, 1, LANES)
    kg = jnp.tile(k_norm_g, (1, LANES // HEAD_DIM)).reshape(DEPTH, 1, LANES)
    wr_t = jnp.swapaxes(w_router, 1, 2)
    br_t = b_router.reshape(DEPTH, N_EXPERTS, 1)
    tok = jnp.arange(ROUTER_TILE, dtype=I32)
    tri = (tok[:, None] <= tok[None, :]).astype(BF16)
    kcache = _extend_heads(cache_k.reshape(DEC_BATCH, DEPTH, PAST_LEN, KV_W)).astype(BF16)
    vcache = _extend_heads(cache_v.reshape(DEC_BATCH, DEPTH, PAST_LEN, KV_W)).astype(BF16)

    new_k = jnp.zeros((BATCH, DEPTH, SEQ, KV_W), F32)
    new_v = jnp.zeros((BATCH, DEPTH, SEQ, KV_W), F32)
    y_ctx = y_lat = None
    for l in range(DEPTH):
        q, kx, vx, new_k, new_v, fcs, ga, gf = _inproj(l, x, norm1, mods, w_in_b, qg, kg, cos_t, sin_t, bd, cs,
                                                        new_k, new_v)
        a = _attention_ctx(l, q, kx, vx)
        a = _attention_lat(l, a, kx, vx, kcache, vcache)
        x1 = _merge(l, "ctx", a, fcs, dft_ctx[0], dft_ctx[1], ga, gf, x, mods, wa_b, wf_b, wo_b)
        x1 = _merge(l, "lat", a, fcs, dft_lat[0], dft_lat[1], ga, gf, x1, mods, wa_b, wf_b, wo_b)
        h2, top_e, top_g, rank, counts = _router(l, x1, norm2, mods, wr_t, br_t, tri)
        dest, blk_tables, zero_blk, n_blk = _plan(top_e, rank, counts)
        xs = _dispatch(l, dest, zero_blk, n_blk, h2)
        yb = _experts(l, *blk_tables, xs, w_exp_in, b_exp_in, w_exp_out, b_exp_out)
        top_g = top_g.T
        if l == DEPTH - 1:
            x, y_ctx, y_lat = _combine(l, dest, yb, x1, top_g, mods, final_norm_g.reshape(1, D_MODEL))
        else:
            (x,) = _combine(l, dest, yb, x1, top_g, mods)

    y_prompt = y_ctx.reshape(BATCH, SEQ, D_MODEL)
    y_sample = y_lat.reshape(DEC_BATCH, DEC_SEQ, D_MODEL)
    shape = (BATCH, DEPTH, SEQ, N_KV_HEADS, HEAD_DIM)
    return (y_prompt, y_sample, new_k.reshape(shape), new_v.reshape(shape))
```
